```python
import jax, jax.numpy as jnp
from jax import lax
import numpy as np

D_MODEL = 1024
BATCH = 1
SEQ = 16384
DEPTH = 1
DEC_BATCH = 16
DEC_SEQ = 16
PAST_LEN = 1024

CHUNK = 64
Q_BLOCK = 128
FOX_HEADS = 8
FOX_HEAD_DIM = D_MODEL // FOX_HEADS
FOX_WIDTH = FOX_HEADS * FOX_HEAD_DIM
FOX_SCALE = FOX_HEAD_DIM ** -0.5
SSM_INNER = D_MODEL
SSM_HEAD_DIM = 64
SSM_HEADS = SSM_INNER // SSM_HEAD_DIM
SSM_GROUPS = 4
SSM_STATE = 128
CONV_WIDTH = 4
CONV_DIM = SSM_INNER + 2 * SSM_GROUPS * SSM_STATE
D_FF = 4 * D_MODEL
N_MOD = 6
EPS = 1e-6
IN_SIZES = (FOX_WIDTH, FOX_WIDTH, FOX_WIDTH, FOX_HEADS, SSM_INNER, CONV_DIM, SSM_HEADS, D_MODEL, D_MODEL)
IN_WIDTH = sum(IN_SIZES)

kernel_name = "fox_ssd_parallel_streaming_step"


def _rmsnorm(x, g):
    xf = x.astype(jnp.float32)
    y = xf * lax.rsqrt(jnp.mean(xf * xf, axis=-1, keepdims=True) + EPS)
    return (y * g.astype(jnp.float32)).astype(x.dtype)


def _causal_conv(u, prev, w, b):
    full = jnp.concatenate([prev.astype(u.dtype), u], axis=1)
    L = u.shape[1]
    y = b.astype(u.dtype)
    for i in range(CONV_WIDTH):
        y = y + full[:, i:i + L] * w[i].astype(u.dtype)
    return y, full[:, full.shape[1] - (CONV_WIDTH - 1):]


def _fox_block(q, k, v, fq, fk, qpos, kpos):
    s = jnp.einsum("bqhd,bkhd->bhqk", q, k).astype(jnp.float32) * FOX_SCALE
    s = s + jnp.swapaxes(fq, 1, 2)[..., :, None] - jnp.swapaxes(fk, 1, 2)[..., None, :]
    visible = kpos[None, :] <= qpos[:, None]
    s = jnp.where(visible, s, -jnp.inf)
    p = jax.nn.softmax(s, axis=-1).astype(v.dtype)
    return jnp.einsum("bhqk,bkhd->bqhd", p, v)


def _fox_attention(q, k_all, v_all, logf_all):
    b, L, h, d = q.shape
    T = k_all.shape[1]
    P = T - L
    F = jnp.cumsum(logf_all.astype(jnp.float32), axis=1)
    fq = F[:, P:]
    kpos = jnp.arange(T)
    qpos = P + jnp.arange(L)
    if L % Q_BLOCK != 0:
        return _fox_block(q, k_all, v_all, fq, F, qpos, kpos)
    nb = L // Q_BLOCK
    qb = jnp.swapaxes(q.reshape(b, nb, Q_BLOCK, h, d), 0, 1)
    fb = jnp.swapaxes(fq.reshape(b, nb, Q_BLOCK, h), 0, 1)
    pb = qpos.reshape(nb, Q_BLOCK)
    out = lax.map(lambda blk: _fox_block(blk[0], k_all, v_all, blk[1], F, blk[2], kpos), (qb, fb, pb))
    return jnp.swapaxes(out, 0, 1).reshape(b, L, h, d)


def _ssd(x, dt, A, Bm, Cm, d_skip, init_state):
    b, L, h, p = x.shape
    g, n = Bm.shape[2], Bm.shape[3]
    r = h // g
    Q = CHUNK if L % CHUNK == 0 else L
    c = L // Q
    xf = x.astype(jnp.float32)
    xd = (xf * dt[..., None]).reshape(b, c, Q, g, r, p)
    a = (dt * A).reshape(b, c, Q, g, r).transpose(0, 1, 3, 4, 2)
    a_cs = jnp.cumsum(a, axis=-1)
    Bc = Bm.astype(jnp.float32).reshape(b, c, Q, g, n)
    Cc = Cm.astype(jnp.float32).reshape(b, c, Q, g, n)
    causal = jnp.tril(jnp.ones((Q, Q), dtype=bool))
    lmat = jnp.exp(jnp.where(causal, a_cs[..., :, None] - a_cs[..., None, :], -jnp.inf))
    cb = jnp.einsum("bclgn,bcsgn->bcgls", Cc, Bc)
    y_diag = jnp.einsum("bcgls,bcgrls,bcsgrp->bclgrp", cb, lmat, xd)
    decay = jnp.exp(a_cs[..., -1:] - a_cs)
    chunk_states = jnp.einsum("bclgn,bcgrl,bclgrp->bcgrpn", Bc, decay, xd)
    chunk_decay = jnp.exp(a_cs[..., -1])

    def step(s, inp):
        cs, dk = inp
        return s * dk[..., None, None] + cs, s

    s0 = init_state.astype(jnp.float32).reshape(b, g, r, p, n)
    s_final, s_prev = lax.scan(step, s0, (jnp.moveaxis(chunk_states, 1, 0), jnp.moveaxis(chunk_decay, 1, 0)))
    s_prev = jnp.moveaxis(s_prev, 0, 1)
    y_off = jnp.einsum("bclgn,bcgrpn,bcgrl->bclgrp", Cc, s_prev, jnp.exp(a_cs))
    y = (y_diag + y_off).reshape(b, L, h, p) + xf * d_skip.astype(jnp.float32)[:, None]
    return y, s_final.reshape(b, h, p, n)


def _token_mixer(h, k_past, v_past, logf_past, conv_past, ssm_past,
                 w_in, b_f, conv_w, conv_b, dt_bias, a_log, d_skip, ssm_norm_g, w_out):
    b, L, _ = h.shape
    u = h @ w_in
    offs = np.cumsum(IN_SIZES)[:-1].tolist()
    q, k, v, f_raw, z, xbc, dt_raw, ga, gb = jnp.split(u, offs, axis=-1)
    q = q.reshape(b, L, FOX_HEADS, FOX_HEAD_DIM)
    k = k.reshape(b, L, FOX_HEADS, FOX_HEAD_DIM)
    v = v.reshape(b, L, FOX_HEADS, FOX_HEAD_DIM)
    logf = jax.nn.log_sigmoid(f_raw.astype(jnp.float32) + b_f.astype(jnp.float32))
    k_all = jnp.concatenate([k_past.astype(k.dtype), k], axis=1)
    v_all = jnp.concatenate([v_past.astype(v.dtype), v], axis=1)
    logf_all = jnp.concatenate([logf_past.astype(jnp.float32), logf], axis=1)
    y_att = _fox_attention(q, k_all, v_all, logf_all).reshape(b, L, FOX_WIDTH)
    xbc_c, conv_new = _causal_conv(xbc, conv_past, conv_w, conv_b)
    xbc_c = jax.nn.silu(xbc_c)
    xs, bm, cm = jnp.split(xbc_c, [SSM_INNER, SSM_INNER + SSM_GROUPS * SSM_STATE], axis=-1)
    dt = jax.nn.softplus(dt_raw.astype(jnp.float32) + dt_bias.astype(jnp.float32))
    A = -jnp.exp(a_log.astype(jnp.float32))
    y_ssm, ssm_new = _ssd(xs.reshape(b, L, SSM_HEADS, SSM_HEAD_DIM), dt, A,
                          bm.reshape(b, L, SSM_GROUPS, SSM_STATE), cm.reshape(b, L, SSM_GROUPS, SSM_STATE),
                          d_skip, ssm_past)
    y_ssm = y_ssm.reshape(b, L, SSM_INNER) * jax.nn.silu(z.astype(jnp.float32))
    y_ssm = y_ssm.reshape(b, L, SSM_GROUPS, SSM_INNER // SSM_GROUPS)
    y_ssm = y_ssm * lax.rsqrt(jnp.mean(y_ssm * y_ssm, axis=-1, keepdims=True) + EPS)
    y_ssm = (y_ssm.reshape(b, L, SSM_INNER) * ssm_norm_g.astype(jnp.float32)).astype(h.dtype)
    merged = jax.nn.sigmoid(ga) * y_att + jax.nn.sigmoid(gb) * y_ssm
    return merged @ w_out, k, v, logf, conv_new, ssm_new.astype(ssm_past.dtype)


def _layer(x, c, k_past, v_past, logf_past, conv_past, ssm_past,
           w_ada, b_ada, g_pre_mix, g_post_mix, g_pre_ffn, g_post_ffn,
           w_in, b_f, conv_w, conv_b, dt_bias, a_log, d_skip, ssm_norm_g, w_out, w_up, w_down):
    mod = jax.nn.silu(c) @ w_ada + b_ada
    sh1, sc1, gt1, sh2, sc2, gt2 = [m[:, None, :] for m in jnp.split(mod, N_MOD, axis=-1)]
    h = _rmsnorm(x, g_pre_mix) * (1 + sc1) + sh1
    m, k, v, logf, conv_new, ssm_new = _token_mixer(h, k_past, v_past, logf_past, conv_past, ssm_past,
                                                    w_in, b_f, conv_w, conv_b, dt_bias, a_log, d_skip,
                                                    ssm_norm_g, w_out)
    x = x + gt1 * _rmsnorm(m, g_post_mix)
    h = _rmsnorm(x, g_pre_ffn) * (1 + sc2) + sh2
    f = jnp.square(jax.nn.relu(h @ w_up)) @ w_down
    x = x + gt2 * _rmsnorm(f, g_post_ffn)
    return x, k, v, logf, conv_new, ssm_new


def setup_inputs(seed: int = 0) -> dict:
    key = jax.random.key(seed)
    ks = jax.random.split(key, 32)
    f32 = jnp.float32
    nrm = lambda k, shape, s: jax.random.normal(k, shape, f32) * s
    dt0 = jnp.exp(jax.random.uniform(ks[20], (DEPTH, SSM_HEADS), f32) * (np.log(0.1) - np.log(0.001)) + np.log(0.001))
    return {
        "x_prompt": nrm(ks[0], (BATCH, SEQ, D_MODEL), 1.0),
        "x_sample": nrm(ks[1], (DEC_BATCH, DEC_SEQ, D_MODEL), 1.0),
        "c_prompt": nrm(ks[2], (BATCH, D_MODEL), 1.0),
        "c_sample": nrm(ks[3], (DEC_BATCH, D_MODEL), 1.0),
        "cache_fox_k": nrm(ks[4], (DEPTH, DEC_BATCH, PAST_LEN, FOX_HEADS, FOX_HEAD_DIM), 1.0),
        "cache_fox_v": nrm(ks[5], (DEPTH, DEC_BATCH, PAST_LEN, FOX_HEADS, FOX_HEAD_DIM), 1.0),
        "cache_fox_logf": jax.nn.log_sigmoid(3.0 + nrm(ks[6], (DEPTH, DEC_BATCH, PAST_LEN, FOX_HEADS), 1.0)),
        "state_ssm_conv": nrm(ks[7], (DEPTH, DEC_BATCH, CONV_WIDTH - 1, CONV_DIM), 1.0),
        "state_ssm": nrm(ks[8], (DEPTH, DEC_BATCH, SSM_HEADS, SSM_HEAD_DIM, SSM_STATE), 0.1),
        "w_ada": nrm(ks[9], (DEPTH, D_MODEL, N_MOD * D_MODEL), 0.5 * D_MODEL ** -0.5),
        "b_ada": nrm(ks[10], (DEPTH, N_MOD * D_MODEL), 0.02),
        "g_pre_mix": 1.0 + nrm(ks[11], (DEPTH, D_MODEL), 0.05),
        "g_post_mix": 1.0 + nrm(ks[12], (DEPTH, D_MODEL), 0.05),
        "g_pre_ffn": 1.0 + nrm(ks[13], (DEPTH, D_MODEL), 0.05),
        "g_post_ffn": 1.0 + nrm(ks[14], (DEPTH, D_MODEL), 0.05),
        "w_in": nrm(ks[15], (DEPTH, D_MODEL, IN_WIDTH), D_MODEL ** -0.5),
        "b_f": 3.0 + nrm(ks[16], (DEPTH, FOX_HEADS), 0.5),
        "conv_w": nrm(ks[17], (DEPTH, CONV_WIDTH, CONV_DIM), CONV_WIDTH ** -0.5),
        "conv_b": nrm(ks[18], (DEPTH, CONV_DIM), 0.02),
        "dt_bias": dt0 + jnp.log(-jnp.expm1(-dt0)),
        "a_log": jnp.log(jax.random.uniform(ks[21], (DEPTH, SSM_HEADS), f32, 1.0, 16.0)),
        "d_skip": 1.0 + nrm(ks[22], (DEPTH, SSM_HEADS), 0.1),
        "ssm_norm_g": 1.0 + nrm(ks[23], (DEPTH, SSM_INNER), 0.05),
        "w_out": nrm(ks[24], (DEPTH, D_MODEL, D_MODEL), D_MODEL ** -0.5),
        "w_up": nrm(ks[25], (DEPTH, D_MODEL, D_FF), D_MODEL ** -0.5),
        "w_down": nrm(ks[26], (DEPTH, D_FF, D_MODEL), D_FF ** -0.5),
    }


def reference(x_prompt, x_sample, c_prompt, c_sample, cache_fox_k, cache_fox_v, cache_fox_logf,
              state_ssm_conv, state_ssm, w_ada, b_ada, g_pre_mix, g_post_mix, g_pre_ffn, g_post_ffn,
              w_in, b_f, conv_w, conv_b, dt_bias, a_log, d_skip, ssm_norm_g, w_out, w_up, w_down):
    bp = x_prompt.shape[0]
    empty_kv = jnp.zeros((bp, 0, FOX_HEADS, FOX_HEAD_DIM), x_prompt.dtype)
    empty_logf = jnp.zeros((bp, 0, FOX_HEADS), jnp.float32)
    zero_conv = jnp.zeros((bp, CONV_WIDTH - 1, CONV_DIM), x_prompt.dtype)
    zero_ssm = jnp.zeros((bp, SSM_HEADS, SSM_HEAD_DIM, SSM_STATE), jnp.float32)
    yp, ys = x_prompt, x_sample
    pk, pv, pl, pc, ps = [], [], [], [], []
    sk, sv, sl, sc, ss = [], [], [], [], []
    for i in range(DEPTH):
        lw = (w_ada[i], b_ada[i], g_pre_mix[i], g_post_mix[i], g_pre_ffn[i], g_post_ffn[i],
              w_in[i], b_f[i], conv_w[i], conv_b[i], dt_bias[i], a_log[i], d_skip[i], ssm_norm_g[i],
              w_out[i], w_up[i], w_down[i])
        yp, k1, v1, l1, c1, s1 = _layer(yp, c_prompt, empty_kv, empty_kv, empty_logf, zero_conv, zero_ssm, *lw)
        ys, k2, v2, l2, c2, s2 = _layer(ys, c_sample, cache_fox_k[i], cache_fox_v[i], cache_fox_logf[i],
                                        state_ssm_conv[i], state_ssm[i], *lw)
        pk.append(k1); pv.append(v1); pl.append(l1); pc.append(c1); ps.append(s1)
        sk.append(k2); sv.append(v2); sl.append(l2); sc.append(c2); ss.append(s2)
    prompt_fox_k = jnp.stack(pk)
    prompt_fox_v = jnp.stack(pv)
    prompt_fox_logf = jnp.stack(pl)
    prompt_ssm_conv = jnp.stack(pc)
    prompt_ssm = jnp.stack(ps)
    sample_fox_k = jnp.stack(sk)
    sample_fox_v = jnp.stack(sv)
    sample_fox_logf = jnp.stack(sl)
    sample_ssm_conv = jnp.stack(sc)
    sample_ssm = jnp.stack(ss)
    return (yp, ys, prompt_fox_k, prompt_fox_v, prompt_fox_logf, prompt_ssm_conv, prompt_ssm,
            sample_fox_k, sample_fox_v, sample_fox_logf, sample_ssm_conv, sample_ssm)
```

```python
import functools

import jax
import jax.numpy as jnp
from jax import lax
from jax.experimental import pallas as pl
from jax.experimental.pallas import tpu as pltpu

F32 = jnp.float32
BF16 = jnp.bfloat16
HIGHEST = lax.Precision.HIGHEST

D_MODEL = 1024
FOX_HEADS = 8
FOX_HEAD_DIM = 128
FOX_SCALE = FOX_HEAD_DIM ** -0.5
SSM_HEADS = 16
SSM_HEAD_DIM = 64
SSM_GROUPS = 4
SSM_STATE = 128
SSM_INNER = SSM_HEADS * SSM_HEAD_DIM
CONV_WIDTH = 4
CONV_DIM = SSM_INNER + 2 * SSM_GROUPS * SSM_STATE
D_FF = 4 * D_MODEL
N_MOD = 6
EPS = 1e-6

LANES = 128
SUBLANES = 8
DT_LANE0 = FOX_HEADS
VMEM_LIMIT = 56 * 1024 * 1024

BIG_WIDTH = 3 * D_MODEL + SSM_INNER + CONV_DIM + 2 * D_MODEL

_NT = (((1,), (1,)), ((), ()))
_TN = (((0,), (0,)), ((), ()))


def _params(*sem):
    return pltpu.CompilerParams(dimension_semantics=sem, vmem_limit_bytes=VMEM_LIMIT)


def _resident(shape):
    zeros = (0,) * len(shape)
    return pl.BlockSpec(shape, lambda *_: zeros, pipeline_mode=pl.Buffered(1))


def _rms(x, g):
    return x * lax.rsqrt(jnp.mean(x * x, axis=-1, keepdims=True) + EPS) * g


def _sigmoid(x):
    return 1.0 / (1.0 + jnp.exp(-x))


def _silu(x):
    return x * _sigmoid(x)


def _ada_kernel(c_ref, w_ref, b_ref, o_ref):
    s = _silu(c_ref[...]).astype(BF16)
    o_ref[...] = jnp.dot(s, w_ref[...].astype(BF16), preferred_element_type=F32) + b_ref[...]


def _ada(c_all, w_ada, b_ada):
    rows = c_all.shape[0]
    width = w_ada.shape[1]
    tn = 1536
    return pl.pallas_call(
        _ada_kernel,
        grid=(width // tn,),
        in_specs=[pl.BlockSpec((rows, D_MODEL), lambda j: (0, 0)),
                  pl.BlockSpec((D_MODEL, tn), lambda j: (0, j)),
                  pl.BlockSpec((1, tn), lambda j: (0, j))],
        out_specs=pl.BlockSpec((rows, tn), lambda j: (0, j)),
        out_shape=jax.ShapeDtypeStruct((rows, width), F32),
        compiler_params=_params("parallel"),
        name="ada_mod",
    )(c_all, w_ada, b_ada)


def _inproj_kernel(x_ref, g_ref, sc_ref, sh_ref, wb_ref, ws_ref, bs_ref,
                   q_ref, kf_ref, vf_ref, kb_ref, vb_ref, z_ref, xbc_ref, ga_ref, gb_ref, sm_ref):
    h = _rms(x_ref[...], g_ref[...]) * (1.0 + sc_ref[...]) + sh_ref[...]
    hb = h.astype(BF16)

    def proj(lo, width):
        return jnp.dot(hb, wb_ref[:, lo:lo + width], preferred_element_type=F32)

    q_ref[...] = (proj(0, D_MODEL) * FOX_SCALE).astype(BF16)
    k = proj(D_MODEL, D_MODEL)
    kf_ref[...] = k
    kb_ref[...] = k.astype(BF16)
    v = proj(2 * D_MODEL, D_MODEL)
    vf_ref[...] = v
    vb_ref[...] = v.astype(BF16)
    z_ref[...] = proj(3 * D_MODEL, SSM_INNER).astype(BF16)
    off = 3 * D_MODEL + SSM_INNER
    xbc_ref[...] = proj(off, CONV_DIM)
    ga_ref[...] = proj(off + CONV_DIM, D_MODEL).astype(BF16)
    gb_ref[...] = proj(off + CONV_DIM + D_MODEL, D_MODEL).astype(BF16)

    u = jnp.dot(hb, ws_ref[...], preferred_element_type=F32) + bs_ref[...]
    t = jnp.log1p(jnp.exp(-jnp.abs(u)))
    lane = lax.broadcasted_iota(jnp.int32, u.shape, 1)
    sm_ref[...] = jnp.where(lane < DT_LANE0, jnp.minimum(u, 0.0) - t,
                            jnp.where(lane < DT_LANE0 + SSM_HEADS, jnp.maximum(u, 0.0) + t, 0.0))


def _mod_spec(arr, tm):
    if arr.shape[0] == 1:
        return pl.BlockSpec((1, D_MODEL), lambda i: (0, 0))
    return pl.BlockSpec((tm, D_MODEL), lambda i: (i, 0))


def _in_proj(x2d, g, sc, sh, w_big, w_small, b_small, tm):
    rows = x2d.shape[0]
    row = lambda w: pl.BlockSpec((tm, w), lambda i: (i, 0))
    shp = lambda w, dt: jax.ShapeDtypeStruct((rows, w), dt)
    return pl.pallas_call(
        _inproj_kernel,
        grid=(rows // tm,),
        in_specs=[row(D_MODEL), _resident((1, D_MODEL)), _mod_spec(sc, tm), _mod_spec(sh, tm),
                  _resident((D_MODEL, BIG_WIDTH)), _resident((D_MODEL, LANES)), _resident((1, LANES))],
        out_specs=[row(D_MODEL), row(D_MODEL), row(D_MODEL), row(D_MODEL), row(D_MODEL),
                   row(SSM_INNER), row(CONV_DIM), row(D_MODEL), row(D_MODEL), row(LANES)],
        out_shape=[shp(D_MODEL, BF16), shp(D_MODEL, F32), shp(D_MODEL, F32), shp(D_MODEL, BF16),
                   shp(D_MODEL, BF16), shp(SSM_INNER, BF16), shp(CONV_DIM, F32), shp(D_MODEL, BF16),
                   shp(D_MODEL, BF16), shp(LANES, F32)],
        compiler_params=_params("parallel"),
        name="in_proj",
    )(x2d, g, sc, sh, w_big, w_small, b_small)


def _cumsum_kernel(sm_ref, ft_ref, carry_ref):
    @pl.when(pl.program_id(0) == 0)
    def _():
        carry_ref[...] = jnp.zeros_like(carry_ref)

    x = sm_ref[...]
    n = x.shape[0]
    r = lax.broadcasted_iota(jnp.int32, (n, n), 0)
    c = lax.broadcasted_iota(jnp.int32, (n, n), 1)
    tri = (r >= c).astype(F32)
    cs = jnp.dot(tri, x, precision=HIGHEST, preferred_element_type=F32) + carry_ref[0:1, :]
    carry_ref[0:1, :] = cs[n - 1:n, :]
    er = lax.broadcasted_iota(jnp.int32, (FOX_HEADS, LANES), 0)
    ec = lax.broadcasted_iota(jnp.int32, (FOX_HEADS, LANES), 1)
    eye = (er == ec).astype(F32)
    ft_ref[...] = lax.dot_general(eye, cs, _NT, precision=HIGHEST, preferred_element_type=F32)


def _logf_cumsum(small, tc):
    rows = small.shape[0]
    return pl.pallas_call(
        _cumsum_kernel,
        grid=(rows // tc,),
        in_specs=[pl.BlockSpec((tc, LANES), lambda i: (i, 0))],
        out_specs=pl.BlockSpec((FOX_HEADS, tc), lambda i: (0, i)),
        out_shape=jax.ShapeDtypeStruct((FOX_HEADS, rows), F32),
        scratch_shapes=[pltpu.VMEM((SUBLANES, LANES), F32)],
        compiler_params=_params("arbitrary"),
        name="logf_cumsum",
    )(small)


def _flash_kernel(fref_ref, q_ref, k_ref, v_ref, fk_ref, o_ref, *, tq):
    h = pl.program_id(0)
    i = pl.program_id(1)
    q = q_ref[...]
    fref = fref_ref[h, i]

    def block(j, carry, masked):
        m, l, acc = carry
        off = pl.multiple_of(j * tq, tq)
        kj = k_ref[pl.ds(off, tq), :]
        vj = v_ref[pl.ds(off, tq), :]
        s = lax.dot_general(q, kj, _NT, preferred_element_type=F32)
        s = s + (fref - fk_ref[j])
        if masked:
            r = lax.broadcasted_iota(jnp.int32, s.shape, 0)
            c = lax.broadcasted_iota(jnp.int32, s.shape, 1)
            s = jnp.where(c <= r, s, -jnp.inf)
        m_new = jnp.maximum(m, jnp.max(s, axis=-1, keepdims=True))
        alpha = jnp.exp(m - m_new)
        p = jnp.exp(s - m_new)
        l = alpha * l + jnp.sum(p, axis=-1, keepdims=True)
        acc = alpha * acc + jnp.dot(p.astype(BF16), vj, preferred_element_type=F32)
        return m_new, l, acc

    init = (jnp.full((tq, 1), -jnp.inf, F32), jnp.zeros((tq, 1), F32), jnp.zeros((tq, FOX_HEAD_DIM), F32))
    carry = lax.fori_loop(0, i, lambda j, cr: block(j, cr, False), init)
    _, l, acc = block(i, carry, True)
    o_ref[...] = (acc / l).astype(o_ref.dtype)


def _fox_prompt(q, k, v, ft, tq):
    L = q.shape[0]
    nq = L // tq
    fk = ft.reshape(FOX_HEADS, nq, 1, tq)
    fref = ft[:, ::tq]
    return pl.pallas_call(
        functools.partial(_flash_kernel, tq=tq),
        grid=(FOX_HEADS, nq),
        in_specs=[pl.BlockSpec(memory_space=pltpu.SMEM),
                  pl.BlockSpec((tq, FOX_HEAD_DIM), lambda h, i: (i, h)),
                  pl.BlockSpec((L, FOX_HEAD_DIM), lambda h, i: (0, h)),
                  pl.BlockSpec((L, FOX_HEAD_DIM), lambda h, i: (0, h)),
                  pl.BlockSpec((None, nq, 1, tq), lambda h, i: (h, 0, 0, 0))],
        out_specs=pl.BlockSpec((tq, FOX_HEAD_DIM), lambda h, i: (i, h)),
        out_shape=jax.ShapeDtypeStruct((L, FOX_HEADS * FOX_HEAD_DIM), BF16),
        compiler_params=_params("parallel", "arbitrary"),
        name="fox_flash",
    )(fref, q, k, v, fk)


def _cached_attn_kernel(q_ref, kn_ref, vn_ref, kc_ref, vc_ref, lpt_ref, lnt_ref, o_ref):
    n = q_ref.shape[0]
    P = kc_ref.shape[0]
    r = lax.broadcasted_iota(jnp.int32, (P, P), 0)
    c = lax.broadcasted_iota(jnp.int32, (P, P), 1)
    G = jnp.dot(lpt_ref[...], (r > c).astype(F32), precision=HIGHEST, preferred_element_type=F32)
    rn = lax.broadcasted_iota(jnp.int32, (n, n), 0)
    cn = lax.broadcasted_iota(jnp.int32, (n, n), 1)
    Hn = jnp.dot(lnt_ref[...], (rn <= cn).astype(F32), precision=HIGHEST, preferred_element_type=F32)
    causal = cn <= rn
    for h in range(FOX_HEADS):
        sl = slice(h * FOX_HEAD_DIM, (h + 1) * FOX_HEAD_DIM)
        qh = q_ref[:, sl]
        sp = lax.dot_general(qh, kc_ref[:, sl].astype(BF16), _NT, preferred_element_type=F32) + G[h:h + 1, :]
        sn = lax.dot_general(qh, kn_ref[:, sl], _NT, preferred_element_type=F32) - Hn[h:h + 1, :]
        sn = jnp.where(causal, sn, -jnp.inf)
        m = jnp.maximum(jnp.max(sp, axis=-1, keepdims=True), jnp.max(sn, axis=-1, keepdims=True))
        pp = jnp.exp(sp - m)
        pn = jnp.exp(sn - m)
        l = jnp.sum(pp, axis=-1, keepdims=True) + jnp.sum(pn, axis=-1, keepdims=True)
        o = (jnp.dot(pp.astype(BF16), vc_ref[:, sl].astype(BF16), preferred_element_type=F32)
             + jnp.dot(pn.astype(BF16), vn_ref[:, sl], preferred_element_type=F32))
        o_ref[:, sl] = (o / l).astype(o_ref.dtype)


def _fox_cached(q, kn, vn, kc, vc, lpt, lnt, n):
    B, P, W = kc.shape
    row = pl.BlockSpec((n, W), lambda b: (b, 0))
    return pl.pallas_call(
        _cached_attn_kernel,
        grid=(B,),
        in_specs=[row, row, row,
                  pl.BlockSpec((None, P, W), lambda b: (b, 0, 0)),
                  pl.BlockSpec((None, P, W), lambda b: (b, 0, 0)),
                  pl.BlockSpec((None, FOX_HEADS, P), lambda b: (b, 0, 0)),
                  pl.BlockSpec((None, FOX_HEADS, n), lambda b: (b, 0, 0))],
        out_specs=row,
        out_shape=jax.ShapeDtypeStruct((B * n, W), BF16),
        compiler_params=_params("parallel"),
        name="fox_cached",
    )(q, kn, vn, kc, vc, lpt, lnt)


def _ssd_kernel(xbc_ref, z_ref, sm_ref, convp_ref, ssmp_ref, cw_ref, cb_ref, alog_ref, dexp_ref, ng_ref,
                y_ref, convn_ref, ssmn_ref, ext_ref, st_ref, *, Q, nchunks):
    ci = pl.program_id(1)
    PAIR = 2 * SSM_HEAD_DIM
    GW = SSM_STATE

    @pl.when(ci == 0)
    def _():
        st_ref[...] = ssmp_ref[...]
        ext_ref[0:SUBLANES, :] = jnp.zeros((SUBLANES, CONV_DIM), F32)
        ext_ref[SUBLANES - (CONV_WIDTH - 1):SUBLANES, :] = convp_ref[...]

    ext_ref[SUBLANES:SUBLANES + Q, :] = xbc_ref[...]
    base = SUBLANES - (CONV_WIDTH - 1)
    conv = cb_ref[...]
    for i in range(CONV_WIDTH):
        conv = conv + ext_ref[base + i:base + i + Q, :] * cw_ref[i:i + 1, :]
    convn_ref[...] = ext_ref[Q + base:Q + SUBLANES, :]
    ext_ref[0:SUBLANES, :] = ext_ref[Q:Q + SUBLANES, :]

    xc = _silu(conv)
    xs = xc[:, :SSM_INNER]
    Bm = xc[:, SSM_INNER:SSM_INNER + SSM_GROUPS * GW]
    Cm = xc[:, SSM_INNER + SSM_GROUPS * GW:]

    sm = sm_ref[...]
    a = sm * (-jnp.exp(alog_ref[...]))
    r = lax.broadcasted_iota(jnp.int32, (Q, Q), 0)
    c = lax.broadcasted_iota(jnp.int32, (Q, Q), 1)
    causal = r >= c
    a_cs = jnp.dot(causal.astype(F32), a, precision=HIGHEST, preferred_element_type=F32)
    er = lax.broadcasted_iota(jnp.int32, (SSM_HEADS, LANES), 0)
    ec = lax.broadcasted_iota(jnp.int32, (SSM_HEADS, LANES), 1)
    pick = (ec == er + DT_LANE0).astype(F32)
    acs_t = lax.dot_general(pick, a_cs, _NT, precision=HIGHEST, preferred_element_type=F32)
    dt_t = lax.dot_general(pick, sm, _NT, precision=HIGHEST, preferred_element_type=F32)
    a_last = a_cs[Q - 1:Q, :]
    e_cs = jnp.exp(a_cs)
    w_in = jnp.exp(a_last - a_cs) * sm
    e_last = jnp.exp(a_last)

    lane_lo = lax.broadcasted_iota(jnp.int32, (Q, PAIR), 1) < SSM_HEAD_DIM
    row_lo = lax.broadcasted_iota(jnp.int32, (PAIR, GW), 0) < SSM_HEAD_DIM

    y_parts = []
    for g in range(SSM_GROUPS):
        Bg = Bm[:, g * GW:(g + 1) * GW]
        Cg = Cm[:, g * GW:(g + 1) * GW]
        cbm = lax.dot_general(Cg.astype(BF16), Bg.astype(BF16), _NT, preferred_element_type=F32)
        for pr in range(SSM_HEADS // SSM_GROUPS // 2):
            pair = g * (SSM_HEADS // SSM_GROUPS // 2) + pr
            xs_pair = xs[:, pair * PAIR:(pair + 1) * PAIR]
            xsb = xs_pair.astype(BF16)
            s_prev = st_ref[pair * PAIR:(pair + 1) * PAIR, :]
            s_prev_b = s_prev.astype(BF16)
            ys, sus = [], []
            for hh in (2 * pair, 2 * pair + 1):
                ln = DT_LANE0 + hh
                lm = jnp.exp(jnp.where(causal, a_cs[:, ln:ln + 1] - acs_t[hh:hh + 1, :], -jnp.inf))
                mat = (cbm * lm * dt_t[hh:hh + 1, :]).astype(BF16)
                y_diag = jnp.dot(mat, xsb, preferred_element_type=F32)
                cw = (Cg * e_cs[:, ln:ln + 1]).astype(BF16)
                y_off = lax.dot_general(cw, s_prev_b, _NT, preferred_element_type=F32)
                bw = (Bg * w_in[:, ln:ln + 1]).astype(BF16)
                sus.append(lax.dot_general(xsb, bw, _TN, preferred_element_type=F32))
                ys.append(y_diag + y_off)
            ln0 = DT_LANE0 + 2 * pair
            dec = jnp.where(row_lo, e_last[:, ln0:ln0 + 1], e_last[:, ln0 + 1:ln0 + 2])
            st_ref[pair * PAIR:(pair + 1) * PAIR, :] = s_prev * dec + jnp.where(row_lo, sus[0], sus[1])
            y_parts.append(jnp.where(lane_lo, ys[0], ys[1])
                           + xs_pair * dexp_ref[:, pair * PAIR:(pair + 1) * PAIR])

    y = jnp.concatenate(y_parts, axis=1) * _silu(z_ref[...].astype(F32))
    gw = SSM_INNER // SSM_GROUPS
    normed = []
    for g in range(SSM_GROUPS):
        yg = y[:, g * gw:(g + 1) * gw]
        normed.append(yg * lax.rsqrt(jnp.mean(yg * yg, axis=-1, keepdims=True) + EPS))
    y_ref[...] = (jnp.concatenate(normed, axis=1) * ng_ref[...]).astype(y_ref.dtype)

    @pl.when(ci == nchunks - 1)
    def _():
        ssmn_ref[...] = st_ref[...]


def _ssd(xbc, z, small, conv_past, ssm_past, conv_w, conv_b, alog_pad, d_exp, norm_g, batch, Q):
    rows = xbc.shape[0]
    nchunks = rows // batch // Q
    assert Q % SUBLANES == 0 and Q >= SUBLANES and nchunks * Q * batch == rows
    row = lambda w: pl.BlockSpec((Q, w), lambda b, c: (b * nchunks + c, 0))
    state_rows = SSM_HEADS * SSM_HEAD_DIM
    per_seq = lambda a, b_: pl.BlockSpec((None, a, b_), lambda b, c: (b, 0, 0))
    return pl.pallas_call(
        functools.partial(_ssd_kernel, Q=Q, nchunks=nchunks),
        grid=(batch, nchunks),
        in_specs=[row(CONV_DIM), row(SSM_INNER), row(LANES),
                  per_seq(CONV_WIDTH - 1, CONV_DIM), per_seq(state_rows, SSM_STATE),
                  _resident((CONV_WIDTH, CONV_DIM)), _resident((1, CONV_DIM)), _resident((1, LANES)),
                  _resident((1, SSM_INNER)), _resident((1, SSM_INNER))],
        out_specs=[row(SSM_INNER), per_seq(CONV_WIDTH - 1, CONV_DIM), per_seq(state_rows, SSM_STATE)],
        out_shape=[jax.ShapeDtypeStruct((rows, SSM_INNER), BF16),
                   jax.ShapeDtypeStruct((batch, CONV_WIDTH - 1, CONV_DIM), F32),
                   jax.ShapeDtypeStruct((batch, state_rows, SSM_STATE), F32)],
        scratch_shapes=[pltpu.VMEM((Q + SUBLANES, CONV_DIM), F32), pltpu.VMEM((state_rows, SSM_STATE), F32)],
        compiler_params=_params("parallel", "arbitrary"),
        name="ssd_scan",
    )(xbc, z, small, conv_past, ssm_past, conv_w, conv_b, alog_pad, d_exp, norm_g)


def _outffn_kernel(ya_ref, ys_ref, ga_ref, gb_ref, x_ref, gt1_ref, sc2_ref, sh2_ref, gt2_ref,
                   gpm_ref, gpf_ref, gqf_ref, wo_ref, wu_ref, wd_ref, o_ref):
    merged = (_sigmoid(ga_ref[...].astype(F32)) * ya_ref[...].astype(F32)
              + _sigmoid(gb_ref[...].astype(F32)) * ys_ref[...].astype(F32))
    m = jnp.dot(merged.astype(BF16), wo_ref[...], preferred_element_type=F32)
    x1 = x_ref[...] + gt1_ref[...] * _rms(m, gpm_ref[...])
    hb = (_rms(x1, gpf_ref[...]) * (1.0 + sc2_ref[...]) + sh2_ref[...]).astype(BF16)
    f = jnp.zeros(x1.shape, F32)
    for cc in range(D_FF // D_MODEL):
        sl = slice(cc * D_MODEL, (cc + 1) * D_MODEL)
        up = jnp.dot(hb, wu_ref[:, sl], preferred_element_type=F32)
        act = jnp.square(jnp.maximum(up, 0.0)).astype(BF16)
        f = f + jnp.dot(act, wd_ref[sl, :], preferred_element_type=F32)
    o_ref[...] = x1 + gt2_ref[...] * _rms(f, gqf_ref[...])


def _out_ffn(ya, ys, ga, gb, x2d, gt1, sc2, sh2, gt2, g_post_mix, g_pre_ffn, g_post_ffn, w_out, w_up, w_down, tm):
    rows = x2d.shape[0]
    row = pl.BlockSpec((tm, D_MODEL), lambda i: (i, 0))
    vec = _resident((1, D_MODEL))
    return pl.pallas_call(
        _outffn_kernel,
        grid=(rows // tm,),
        in_specs=[row, row, row, row, row,
                  _mod_spec(gt1, tm), _mod_spec(sc2, tm), _mod_spec(sh2, tm), _mod_spec(gt2, tm),
                  vec, vec, vec,
                  _resident((D_MODEL, D_MODEL)), _resident((D_MODEL, D_FF)), _resident((D_FF, D_MODEL))],
        out_specs=row,
        out_shape=jax.ShapeDtypeStruct((rows, D_MODEL), F32),
        compiler_params=_params("parallel"),
        name="out_ffn",
    )(ya, ys, ga, gb, x2d, gt1, sc2, sh2, gt2, g_post_mix, g_pre_ffn, g_post_ffn, w_out, w_up, w_down)


def _layer(x, mod, past, wts, *, tm, tq, Q):
    b, L, _ = x.shape
    x2d = x.reshape(b * L, D_MODEL)
    sh1, sc1, gt1, sh2, sc2, gt2 = mod
    q, kf, vf, kb, vb, z, xbc, ga, gb, small = _in_proj(
        x2d, wts["g_pre_mix"], sc1, sh1, wts["w_big"], wts["w_small"], wts["b_small"], tm)

    if past is None:
        assert b == 1
        ft = _logf_cumsum(small, min(512, L))
        y_att = _fox_prompt(q, kb, vb, ft, tq)
        conv_past = jnp.zeros((b, CONV_WIDTH - 1, CONV_DIM), F32)
        ssm_past = jnp.zeros((b, SSM_INNER, SSM_STATE), F32)
    else:
        k_past, v_past, logf_past, conv_past, ssm_past = past
        P = k_past.shape[1]
        lpt = jnp.swapaxes(logf_past, 1, 2)
        lnt = jnp.swapaxes(small[:, :FOX_HEADS].reshape(b, L, FOX_HEADS), 1, 2)
        y_att = _fox_cached(q, kb, vb, k_past.reshape(b, P, D_MODEL), v_past.reshape(b, P, D_MODEL), lpt, lnt, L)
        ssm_past = ssm_past.reshape(b, SSM_INNER, SSM_STATE)

    y_ssm, conv_new, ssm_new = _ssd(xbc, z, small, conv_past, ssm_past, wts["conv_w"], wts["conv_b"],
                                    wts["alog_pad"], wts["d_exp"], wts["ssm_norm_g"], b, Q)
    y = _out_ffn(y_att, y_ssm, ga, gb, x2d, gt1, sc2, sh2, gt2, wts["g_post_mix"], wts["g_pre_ffn"],
                 wts["g_post_ffn"], wts["w_out"], wts["w_up"], wts["w_down"], tm)
    return (y.reshape(b, L, D_MODEL),
            kf.reshape(b, L, FOX_HEADS, FOX_HEAD_DIM), vf.reshape(b, L, FOX_HEADS, FOX_HEAD_DIM),
            small[:, :FOX_HEADS].reshape(b, L, FOX_HEADS), conv_new,
            ssm_new.reshape(b, SSM_HEADS, SSM_HEAD_DIM, SSM_STATE))


def _prep_weights(w_ada, b_ada, g_pre_mix, g_post_mix, g_pre_ffn, g_post_ffn, w_in, b_f, conv_w, conv_b,
                  dt_bias, a_log, d_skip, ssm_norm_g, w_out, w_up, w_down):
    sizes = (D_MODEL, D_MODEL, D_MODEL, FOX_HEADS, SSM_INNER, CONV_DIM, SSM_HEADS, D_MODEL, D_MODEL)
    offs = [0]
    for s in sizes:
        offs.append(offs[-1] + s)
    piece = lambda i: w_in[:, offs[i]:offs[i + 1]]
    pad = LANES - FOX_HEADS - SSM_HEADS
    row = lambda v: v.reshape(1, -1).astype(F32)
    return {
        "w_big": jnp.concatenate([piece(0), piece(1), piece(2), piece(4), piece(5), piece(7), piece(8)],
                                 axis=1).astype(BF16),
        "w_small": jnp.pad(jnp.concatenate([piece(3), piece(6)], axis=1), ((0, 0), (0, pad))).astype(BF16),
        "b_small": jnp.pad(jnp.concatenate([b_f, dt_bias]), (0, pad)).reshape(1, LANES).astype(F32),
        "alog_pad": jnp.pad(a_log, (DT_LANE0, LANES - DT_LANE0 - SSM_HEADS)).reshape(1, LANES).astype(F32),
        "d_exp": jnp.repeat(d_skip, SSM_HEAD_DIM).reshape(1, SSM_INNER).astype(F32),
        "g_pre_mix": row(g_pre_mix), "g_post_mix": row(g_post_mix),
        "g_pre_ffn": row(g_pre_ffn), "g_post_ffn": row(g_post_ffn),
        "conv_w": conv_w.astype(F32), "conv_b": row(conv_b), "ssm_norm_g": row(ssm_norm_g),
        "w_out": w_out.astype(BF16), "w_up": w_up.astype(BF16), "w_down": w_down.astype(BF16),
    }


def _forward(x_prompt, x_sample, c_prompt, c_sample, cache_fox_k, cache_fox_v, cache_fox_logf,
             state_ssm_conv, state_ssm, w_ada, b_ada, *layer_w, tm_prompt, tq, q_prompt):
    depth = w_ada.shape[0]
    bp, Lp, _ = x_prompt.shape
    bs, Ls, _ = x_sample.shape
    yp, ys = x_prompt, x_sample
    outs_p, outs_s = [], []
    for i in range(depth):
        wts = _prep_weights(w_ada[i], b_ada[i], *[w[i] for w in layer_w])
        mod = _ada(jnp.concatenate([c_prompt, c_sample], axis=0), w_ada[i], b_ada[i].reshape(1, -1))
        mod_p = [m for m in jnp.split(mod[:bp], N_MOD, axis=-1)]
        mod_s = [jnp.repeat(m, Ls, axis=0) for m in jnp.split(mod[bp:], N_MOD, axis=-1)]
        rp = _layer(yp, mod_p, None, wts, tm=tm_prompt, tq=tq, Q=q_prompt)
        rs = _layer(ys, mod_s, (cache_fox_k[i], cache_fox_v[i], cache_fox_logf[i], state_ssm_conv[i], state_ssm[i]),
                    wts, tm=bs * Ls, tq=None, Q=Ls)
        yp, ys = rp[0], rs[0]
        outs_p.append(rp[1:])
        outs_s.append(rs[1:])
    stack = lambda outs, j: jnp.stack([o[j] for o in outs])
    return (yp, ys) + tuple(stack(outs_p, j) for j in range(5)) + tuple(stack(outs_s, j) for j in range(5))


def kernel(x_prompt, x_sample, c_prompt, c_sample, cache_fox_k, cache_fox_v, cache_fox_logf, state_ssm_conv,
           state_ssm, w_ada, b_ada, g_pre_mix, g_post_mix, g_pre_ffn, g_post_ffn, w_in, b_f, conv_w, conv_b,
           dt_bias, a_log, d_skip, ssm_norm_g, w_out, w_up, w_down):
    assert x_prompt.shape[0] == 1, "the prompt path carries one sequence"
    L = x_prompt.shape[1]
    return _forward(x_prompt, x_sample, c_prompt, c_sample, cache_fox_k, cache_fox_v, cache_fox_logf,
                    state_ssm_conv, state_ssm, w_ada, b_ada, g_pre_mix, g_post_mix, g_pre_ffn, g_post_ffn,
                    w_in, b_f, conv_w, conv_b, dt_bias, a_log, d_skip, ssm_norm_g, w_out, w_up, w_down,
                    tm_prompt=min(512, L), tq=min(512, L), q_prompt=min(256, L))
```

```python
import functools

import jax
import jax.numpy as jnp
from jax import lax
from jax.experimental import pallas as pl
from jax.experimental.pallas import tpu as pltpu

F32 = jnp.float32
BF16 = jnp.bfloat16
HIGHEST = lax.Precision.HIGHEST

D_MODEL = 1024
FOX_HEADS = 8
FOX_HEAD_DIM = 128
FOX_SCALE = FOX_HEAD_DIM ** -0.5
SSM_HEADS = 16
SSM_HEAD_DIM = 64
SSM_GROUPS = 4
SSM_STATE = 128
SSM_INNER = SSM_HEADS * SSM_HEAD_DIM
CONV_WIDTH = 4
CONV_DIM = SSM_INNER + 2 * SSM_GROUPS * SSM_STATE
D_FF = 4 * D_MODEL
N_MOD = 6
EPS = 1e-6

LANES = 128
SUBLANES = 8
DT_LANE0 = FOX_HEADS
VMEM_LIMIT = 56 * 1024 * 1024
LOG2E = 1.4426950408889634
BIAS_PARTS = 3
CUM_ROWS = 128
ROW_TILE = 16

BIG_WIDTH = 3 * D_MODEL + SSM_INNER + CONV_DIM + 2 * D_MODEL

_NT = (((1,), (1,)), ((), ()))
_TN = (((0,), (0,)), ((), ()))


def _params(*sem):
    return pltpu.CompilerParams(dimension_semantics=sem, vmem_limit_bytes=VMEM_LIMIT)


def _resident(shape):
    zeros = (0,) * len(shape)
    return pl.BlockSpec(shape, lambda *_: zeros, pipeline_mode=pl.Buffered(1))


def _rms(x, g):
    return x * lax.rsqrt(jnp.mean(x * x, axis=-1, keepdims=True) + EPS) * g


def _sigmoid(x):
    return 1.0 / (1.0 + jnp.exp(-x))


def _silu(x):
    return x * _sigmoid(x)


def _ada_kernel(c_ref, w_ref, b_ref, o_ref):
    s = _silu(c_ref[...]).astype(BF16)
    o_ref[...] = jnp.dot(s, w_ref[...].astype(BF16), preferred_element_type=F32) + b_ref[...]


def _ada(c_all, w_ada, b_ada):
    rows = c_all.shape[0]
    width = w_ada.shape[1]
    tn = 1536
    return pl.pallas_call(
        _ada_kernel,
        grid=(width // tn,),
        in_specs=[pl.BlockSpec((rows, D_MODEL), lambda j: (0, 0)),
                  pl.BlockSpec((D_MODEL, tn), lambda j: (0, j)),
                  pl.BlockSpec((1, tn), lambda j: (0, j))],
        out_specs=pl.BlockSpec((rows, tn), lambda j: (0, j)),
        out_shape=jax.ShapeDtypeStruct((rows, width), F32),
        compiler_params=_params("parallel"),
        name="ada_mod",
    )(c_all, w_ada, b_ada)


def _inproj_kernel(x_ref, g_ref, sc_ref, sh_ref, wb_ref, ws_ref, bs_ref, *rest, with_bias):
    if with_bias:
        sel_ref, q_ref, kf_ref, vf_ref, kb_ref, vb_ref, z_ref, xbc_ref, ga_ref, gb_ref, sm_ref, carry_ref = rest
    else:
        q_ref, kf_ref, vf_ref, kb_ref, vb_ref, z_ref, xbc_ref, ga_ref, gb_ref, sm_ref = rest
    h = _rms(x_ref[...], g_ref[...]) * (1.0 + sc_ref[...]) + sh_ref[...]
    hb = h.astype(BF16)

    def proj(lo, width):
        return jnp.dot(hb, wb_ref[:, lo:lo + width], preferred_element_type=F32)

    u = jnp.dot(hb, ws_ref[...], preferred_element_type=F32) + bs_ref[...]
    t = jnp.log1p(jnp.exp(-jnp.abs(u)))
    lane = lax.broadcasted_iota(jnp.int32, u.shape, 1)
    sm = jnp.where(lane < DT_LANE0, jnp.minimum(u, 0.0) - t,
                   jnp.where(lane < DT_LANE0 + SSM_HEADS, jnp.maximum(u, 0.0) + t, 0.0))
    sm_ref[...] = sm

    q_ref[...] = (proj(0, D_MODEL) * (FOX_SCALE * LOG2E)).astype(BF16)
    k = proj(D_MODEL, D_MODEL)
    kf_ref[...] = k
    if with_bias:
        @pl.when(pl.program_id(0) == 0)
        def _():
            carry_ref[...] = jnp.zeros_like(carry_ref)

        r = lax.broadcasted_iota(jnp.int32, (CUM_ROWS, CUM_ROWS), 0)
        c = lax.broadcasted_iota(jnp.int32, (CUM_ROWS, CUM_ROWS), 1)
        tri = (r >= c).astype(F32)
        run = carry_ref[0:1, :]
        parts = []
        for b in range(sm.shape[0] // CUM_ROWS):
            cs = jnp.dot(tri, sm[b * CUM_ROWS:(b + 1) * CUM_ROWS, :], precision=HIGHEST,
                         preferred_element_type=F32) + run
            run = cs[CUM_ROWS - 1:CUM_ROWS, :]
            parts.append(cs)
        carry_ref[0:1, :] = run
        nb = jnp.concatenate(parts, axis=0) * (-LOG2E)
        hi = nb.astype(BF16)
        r1 = nb - hi.astype(F32)
        mid = r1.astype(BF16)
        lo = (r1 - mid.astype(F32)).astype(BF16)
        aug = jnp.dot(jnp.concatenate([hi, mid, lo], axis=1), sel_ref[...], preferred_element_type=F32)
        for hd in range(FOX_HEADS):
            sl = slice(hd * FOX_HEAD_DIM, (hd + 1) * FOX_HEAD_DIM)
            kb_ref[:, 2 * hd * FOX_HEAD_DIM:(2 * hd + 1) * FOX_HEAD_DIM] = k[:, sl].astype(BF16)
            kb_ref[:, (2 * hd + 1) * FOX_HEAD_DIM:(2 * hd + 2) * FOX_HEAD_DIM] = aug[:, sl].astype(BF16)
    else:
        kb_ref[...] = k.astype(BF16)
    v = proj(2 * D_MODEL, D_MODEL)
    vf_ref[...] = v
    vb_ref[...] = v.astype(BF16)
    z_ref[...] = proj(3 * D_MODEL, SSM_INNER).astype(BF16)
    off = 3 * D_MODEL + SSM_INNER
    xbc_ref[...] = proj(off, CONV_DIM)
    ga_ref[...] = proj(off + CONV_DIM, D_MODEL).astype(BF16)
    gb_ref[...] = proj(off + CONV_DIM + D_MODEL, D_MODEL).astype(BF16)


def _mod_spec(arr, tm):
    if arr.shape[0] == 1:
        return pl.BlockSpec((1, D_MODEL), lambda i: (0, 0))
    return pl.BlockSpec((tm, D_MODEL), lambda i: (i, 0))


def _bias_selector():
    r = lax.broadcasted_iota(jnp.int32, (BIAS_PARTS * LANES, FOX_HEADS * FOX_HEAD_DIM), 0)
    c = lax.broadcasted_iota(jnp.int32, (BIAS_PARTS * LANES, FOX_HEADS * FOX_HEAD_DIM), 1)
    hit = (r % LANES < FOX_HEADS) & (c == (r % LANES) * FOX_HEAD_DIM + r // LANES)
    return hit.astype(BF16)


def _in_proj(x2d, g, sc, sh, w_big, w_small, b_small, tm, with_bias):
    rows = x2d.shape[0]
    assert not with_bias or tm % CUM_ROWS == 0
    row = lambda w: pl.BlockSpec((tm, w), lambda i: (i, 0))
    shp = lambda w, dt: jax.ShapeDtypeStruct((rows, w), dt)
    kb_width = 2 * D_MODEL if with_bias else D_MODEL
    in_specs = [row(D_MODEL), _resident((1, D_MODEL)), _mod_spec(sc, tm), _mod_spec(sh, tm),
                _resident((D_MODEL, BIG_WIDTH)), _resident((D_MODEL, LANES)), _resident((1, LANES))]
    args = [x2d, g, sc, sh, w_big, w_small, b_small]
    if with_bias:
        in_specs.append(_resident((BIAS_PARTS * LANES, FOX_HEADS * FOX_HEAD_DIM)))
        args.append(_bias_selector())
    return pl.pallas_call(
        functools.partial(_inproj_kernel, with_bias=with_bias),
        grid=(rows // tm,),
        in_specs=in_specs,
        out_specs=[row(D_MODEL), row(D_MODEL), row(D_MODEL), row(kb_width), row(D_MODEL),
                   row(SSM_INNER), row(CONV_DIM), row(D_MODEL), row(D_MODEL), row(LANES)],
        out_shape=[shp(D_MODEL, BF16), shp(D_MODEL, F32), shp(D_MODEL, F32), shp(kb_width, BF16),
                   shp(D_MODEL, BF16), shp(SSM_INNER, BF16), shp(CONV_DIM, F32), shp(D_MODEL, BF16),
                   shp(D_MODEL, BF16), shp(LANES, F32)],
        scratch_shapes=[pltpu.VMEM((SUBLANES, LANES), F32)] if with_bias else [],
        compiler_params=_params("arbitrary" if with_bias else "parallel"),
        name="in_proj",
    )(*args)


def _flash_kernel(q_ref, k_ref, v_ref, o_ref, s0_ref, s1_ref, p_ref, m_ref, l_ref, alpha_ref, acc_ref, *, tq):
    i = pl.program_id(1)
    nlt = tq // LANES
    lane = lax.broadcasted_iota(jnp.int32, (tq, FOX_HEAD_DIM), 1)
    ones = jnp.where(lane < BIAS_PARTS, 1.0, 0.0).astype(BF16)
    q2 = jnp.concatenate([q_ref[...], ones], axis=1)

    def scores(j, s_ref):
        off = pl.multiple_of(j * tq, tq)
        s_ref[...] = lax.dot_general(q2, k_ref[pl.ds(off, tq), :], _NT, preferred_element_type=F32)

    def update(j, s_ref, masked):
        for rt in range(tq // ROW_TILE):
            rows = slice(rt * ROW_TILE, (rt + 1) * ROW_TILE)
            cols = [s_ref[rows, c * LANES:(c + 1) * LANES] for c in range(nlt)]
            if masked:
                ri = lax.broadcasted_iota(jnp.int32, (ROW_TILE, LANES), 0) + rt * ROW_TILE
                ci = lax.broadcasted_iota(jnp.int32, (ROW_TILE, LANES), 1)
                cols = [jnp.where(ci + c * LANES <= ri, cols[c], -jnp.inf) for c in range(nlt)]
            mx = functools.reduce(jnp.maximum, cols)
            m_old = m_ref[rows, :]
            m_new = jnp.maximum(m_old, jnp.max(mx, axis=-1, keepdims=True))
            alpha = jnp.exp2(m_old - m_new)
            ps = [jnp.exp2(cc - m_new) for cc in cols]
            l_ref[rows, :] = alpha * l_ref[rows, :] + jnp.sum(functools.reduce(jnp.add, ps), axis=-1, keepdims=True)
            m_ref[rows, :] = m_new
            alpha_ref[rows, :] = alpha
            for c in range(nlt):
                p_ref[rows, c * LANES:(c + 1) * LANES] = ps[c].astype(BF16)
        off = pl.multiple_of(j * tq, tq)
        acc_ref[...] = alpha_ref[...] * acc_ref[...] + jnp.dot(p_ref[...], v_ref[pl.ds(off, tq), :],
                                                               preferred_element_type=F32)

    m_ref[...] = jnp.full(m_ref.shape, -jnp.inf, F32)
    l_ref[...] = jnp.zeros(l_ref.shape, F32)
    acc_ref[...] = jnp.zeros(acc_ref.shape, F32)
    scores(0, s0_ref)

    @pl.loop(0, i // 2)
    def _(t):
        scores(2 * t + 1, s1_ref)
        update(2 * t, s0_ref, False)
        scores(2 * t + 2, s0_ref)
        update(2 * t + 1, s1_ref, False)

    @pl.when(i % 2 == 0)
    def _():
        update(i, s0_ref, True)

    @pl.when(i % 2 == 1)
    def _():
        scores(i, s1_ref)
        update(i - 1, s0_ref, False)
        update(i, s1_ref, True)

    o_ref[...] = (acc_ref[...] / l_ref[...]).astype(o_ref.dtype)


def _fox_prompt(q, kp, v, tq):
    L = q.shape[0]
    return pl.pallas_call(
        functools.partial(_flash_kernel, tq=tq),
        grid=(FOX_HEADS, L // tq),
        in_specs=[pl.BlockSpec((tq, FOX_HEAD_DIM), lambda h, i: (i, h)),
                  pl.BlockSpec((L, 2 * FOX_HEAD_DIM), lambda h, i: (0, h)),
                  pl.BlockSpec((L, FOX_HEAD_DIM), lambda h, i: (0, h))],
        out_specs=pl.BlockSpec((tq, FOX_HEAD_DIM), lambda h, i: (i, h)),
        out_shape=jax.ShapeDtypeStruct((L, FOX_HEADS * FOX_HEAD_DIM), BF16),
        scratch_shapes=[pltpu.VMEM((tq, tq), F32), pltpu.VMEM((tq, tq), F32), pltpu.VMEM((tq, tq), BF16)]
        + [pltpu.VMEM((tq, FOX_HEAD_DIM), F32)] * 4,
        compiler_params=_params("parallel", "arbitrary"),
        name="fox_flash",
    )(q, kp, v)


def _cached_attn_kernel(q_ref, kn_ref, vn_ref, kc_ref, vc_ref, lpt_ref, lnt_ref, o_ref):
    n = q_ref.shape[0]
    P = kc_ref.shape[0]
    r = lax.broadcasted_iota(jnp.int32, (P, P), 0)
    c = lax.broadcasted_iota(jnp.int32, (P, P), 1)
    G = jnp.dot(lpt_ref[...], (r > c).astype(F32), precision=HIGHEST, preferred_element_type=F32) * LOG2E
    rn = lax.broadcasted_iota(jnp.int32, (n, n), 0)
    cn = lax.broadcasted_iota(jnp.int32, (n, n), 1)
    Hn = jnp.dot(lnt_ref[...], (rn <= cn).astype(F32), precision=HIGHEST, preferred_element_type=F32) * LOG2E
    causal = cn <= rn
    for h in range(FOX_HEADS):
        sl = slice(h * FOX_HEAD_DIM, (h + 1) * FOX_HEAD_DIM)
        qh = q_ref[:, sl]
        sp = lax.dot_general(qh, kc_ref[:, sl].astype(BF16), _NT, preferred_element_type=F32) + G[h:h + 1, :]
        sn = lax.dot_general(qh, kn_ref[:, sl], _NT, preferred_element_type=F32) - Hn[h:h + 1, :]
        sn = jnp.where(causal, sn, -jnp.inf)
        m = jnp.maximum(jnp.max(sp, axis=-1, keepdims=True), jnp.max(sn, axis=-1, keepdims=True))
        pp = jnp.exp2(sp - m)
        pn = jnp.exp2(sn - m)
        l = jnp.sum(pp, axis=-1, keepdims=True) + jnp.sum(pn, axis=-1, keepdims=True)
        o = (jnp.dot(pp.astype(BF16), vc_ref[:, sl].astype(BF16), preferred_element_type=F32)
             + jnp.dot(pn.astype(BF16), vn_ref[:, sl], preferred_element_type=F32))
        o_ref[:, sl] = (o / l).astype(o_ref.dtype)


def _fox_cached(q, kn, vn, kc, vc, lpt, lnt, n):
    B, P, W = kc.shape
    row = pl.BlockSpec((n, W), lambda b: (b, 0))
    return pl.pallas_call(
        _cached_attn_kernel,
        grid=(B,),
        in_specs=[row, row, row,
                  pl.BlockSpec((None, P, W), lambda b: (b, 0, 0)),
                  pl.BlockSpec((None, P, W), lambda b: (b, 0, 0)),
                  pl.BlockSpec((None, FOX_HEADS, P), lambda b: (b, 0, 0)),
                  pl.BlockSpec((None, FOX_HEADS, n), lambda b: (b, 0, 0))],
        out_specs=row,
        out_shape=jax.ShapeDtypeStruct((B * n, W), BF16),
        compiler_params=_params("parallel"),
        name="fox_cached",
    )(q, kn, vn, kc, vc, lpt, lnt)


def _ssd_kernel(xbc_ref, z_ref, sm_ref, convp_ref, ssmp_ref, cw_ref, cb_ref, alog_ref, dexp_ref, ng_ref,
                y_ref, convn_ref, ssmn_ref, ext_ref, st_ref, *, Q, nchunks):
    ci = pl.program_id(1)
    PAIR = 2 * SSM_HEAD_DIM
    GW = SSM_STATE

    @pl.when(ci == 0)
    def _():
        st_ref[...] = ssmp_ref[...]
        ext_ref[0:SUBLANES, :] = jnp.zeros((SUBLANES, CONV_DIM), F32)
        ext_ref[SUBLANES - (CONV_WIDTH - 1):SUBLANES, :] = convp_ref[...]

    ext_ref[SUBLANES:SUBLANES + Q, :] = xbc_ref[...]
    base = SUBLANES - (CONV_WIDTH - 1)
    conv = cb_ref[...]
    for i in range(CONV_WIDTH):
        conv = conv + ext_ref[base + i:base + i + Q, :] * cw_ref[i:i + 1, :]
    convn_ref[...] = ext_ref[Q + base:Q + SUBLANES, :]
    ext_ref[0:SUBLANES, :] = ext_ref[Q:Q + SUBLANES, :]

    xc = _silu(conv)
    xs = xc[:, :SSM_INNER]
    Bm = xc[:, SSM_INNER:SSM_INNER + SSM_GROUPS * GW]
    Cm = xc[:, SSM_INNER + SSM_GROUPS * GW:]

    sm = sm_ref[...]
    a = sm * (-jnp.exp(alog_ref[...]))
    r = lax.broadcasted_iota(jnp.int32, (Q, Q), 0)
    c = lax.broadcasted_iota(jnp.int32, (Q, Q), 1)
    causal = r >= c
    a_cs = jnp.dot(causal.astype(F32), a, precision=HIGHEST, preferred_element_type=F32)
    er = lax.broadcasted_iota(jnp.int32, (SSM_HEADS, LANES), 0)
    ec = lax.broadcasted_iota(jnp.int32, (SSM_HEADS, LANES), 1)
    pick = (ec == er + DT_LANE0).astype(F32)
    acs_t = lax.dot_general(pick, a_cs, _NT, precision=HIGHEST, preferred_element_type=F32)
    dt_t = lax.dot_general(pick, sm, _NT, precision=HIGHEST, preferred_element_type=F32)
    a_last = a_cs[Q - 1:Q, :]
    e_cs = jnp.exp(a_cs)
    w_in = jnp.exp(a_last - a_cs) * sm
    e_last = jnp.exp(a_last)

    lane_lo = lax.broadcasted_iota(jnp.int32, (Q, PAIR), 1) < SSM_HEAD_DIM
    row_lo = lax.broadcasted_iota(jnp.int32, (PAIR, GW), 0) < SSM_HEAD_DIM

    y_parts = []
    for g in range(SSM_GROUPS):
        Bg = Bm[:, g * GW:(g + 1) * GW]
        Cg = Cm[:, g * GW:(g + 1) * GW]
        cbm = lax.dot_general(Cg.astype(BF16), Bg.astype(BF16), _NT, preferred_element_type=F32)
        for pr in range(SSM_HEADS // SSM_GROUPS // 2):
            pair = g * (SSM_HEADS // SSM_GROUPS // 2) + pr
            xs_pair = xs[:, pair * PAIR:(pair + 1) * PAIR]
            xsb = xs_pair.astype(BF16)
            s_prev = st_ref[pair * PAIR:(pair + 1) * PAIR, :]
            s_prev_b = s_prev.astype(BF16)
            ys, sus = [], []
            for hh in (2 * pair, 2 * pair + 1):
                ln = DT_LANE0 + hh
                lm = jnp.exp(jnp.where(causal, a_cs[:, ln:ln + 1] - acs_t[hh:hh + 1, :], -jnp.inf))
                mat = (cbm * lm * dt_t[hh:hh + 1, :]).astype(BF16)
                y_diag = jnp.dot(mat, xsb, preferred_element_type=F32)
                cw = (Cg * e_cs[:, ln:ln + 1]).astype(BF16)
                y_off = lax.dot_general(cw, s_prev_b, _NT, preferred_element_type=F32)
                bw = (Bg * w_in[:, ln:ln + 1]).astype(BF16)
                sus.append(lax.dot_general(xsb, bw, _TN, preferred_element_type=F32))
                ys.append(y_diag + y_off)
            ln0 = DT_LANE0 + 2 * pair
            dec = jnp.where(row_lo, e_last[:, ln0:ln0 + 1], e_last[:, ln0 + 1:ln0 + 2])
            st_ref[pair * PAIR:(pair + 1) * PAIR, :] = s_prev * dec + jnp.where(row_lo, sus[0], sus[1])
            y_parts.append(jnp.where(lane_lo, ys[0], ys[1])
                           + xs_pair * dexp_ref[:, pair * PAIR:(pair + 1) * PAIR])

    y = jnp.concatenate(y_parts, axis=1) * _silu(z_ref[...].astype(F32))
    gw = SSM_INNER // SSM_GROUPS
    normed = []
    for g in range(SSM_GROUPS):
        yg = y[:, g * gw:(g + 1) * gw]
        normed.append(yg * lax.rsqrt(jnp.mean(yg * yg, axis=-1, keepdims=True) + EPS))
    y_ref[...] = (jnp.concatenate(normed, axis=1) * ng_ref[...]).astype(y_ref.dtype)

    @pl.when(ci == nchunks - 1)
    def _():
        ssmn_ref[...] = st_ref[...]


def _ssd(xbc, z, small, conv_past, ssm_past, conv_w, conv_b, alog_pad, d_exp, norm_g, batch, Q):
    rows = xbc.shape[0]
    nchunks = rows // batch // Q
    assert Q % SUBLANES == 0 and Q >= SUBLANES and nchunks * Q * batch == rows
    row = lambda w: pl.BlockSpec((Q, w), lambda b, c: (b * nchunks + c, 0))
    state_rows = SSM_HEADS * SSM_HEAD_DIM
    per_seq = lambda a, b_: pl.BlockSpec((None, a, b_), lambda b, c: (b, 0, 0))
    return pl.pallas_call(
        functools.partial(_ssd_kernel, Q=Q, nchunks=nchunks),
        grid=(batch, nchunks),
        in_specs=[row(CONV_DIM), row(SSM_INNER), row(LANES),
                  per_seq(CONV_WIDTH - 1, CONV_DIM), per_seq(state_rows, SSM_STATE),
                  _resident((CONV_WIDTH, CONV_DIM)), _resident((1, CONV_DIM)), _resident((1, LANES)),
                  _resident((1, SSM_INNER)), _resident((1, SSM_INNER))],
        out_specs=[row(SSM_INNER), per_seq(CONV_WIDTH - 1, CONV_DIM), per_seq(state_rows, SSM_STATE)],
        out_shape=[jax.ShapeDtypeStruct((rows, SSM_INNER), BF16),
                   jax.ShapeDtypeStruct((batch, CONV_WIDTH - 1, CONV_DIM), F32),
                   jax.ShapeDtypeStruct((batch, state_rows, SSM_STATE), F32)],
        scratch_shapes=[pltpu.VMEM((Q + SUBLANES, CONV_DIM), F32), pltpu.VMEM((state_rows, SSM_STATE), F32)],
        compiler_params=_params("parallel", "arbitrary"),
        name="ssd_scan",
    )(xbc, z, small, conv_past, ssm_past, conv_w, conv_b, alog_pad, d_exp, norm_g)


def _outffn_kernel(ya_ref, ys_ref, ga_ref, gb_ref, x_ref, gt1_ref, sc2_ref, sh2_ref, gt2_ref,
                   gpm_ref, gpf_ref, gqf_ref, wo_ref, wu_ref, wd_ref, o_ref):
    merged = (_sigmoid(ga_ref[...].astype(F32)) * ya_ref[...].astype(F32)
              + _sigmoid(gb_ref[...].astype(F32)) * ys_ref[...].astype(F32))
    m = jnp.dot(merged.astype(BF16), wo_ref[...], preferred_element_type=F32)
    x1 = x_ref[...] + gt1_ref[...] * _rms(m, gpm_ref[...])
    hb = (_rms(x1, gpf_ref[...]) * (1.0 + sc2_ref[...]) + sh2_ref[...]).astype(BF16)
    f = jnp.zeros(x1.shape, F32)
    for cc in range(D_FF // D_MODEL):
        sl = slice(cc * D_MODEL, (cc + 1) * D_MODEL)
        up = jnp.dot(hb, wu_ref[:, sl], preferred_element_type=F32)
        act = jnp.square(jnp.maximum(up, 0.0)).astype(BF16)
        f = f + jnp.dot(act, wd_ref[sl, :], preferred_element_type=F32)
    o_ref[...] = x1 + gt2_ref[...] * _rms(f, gqf_ref[...])


def _out_ffn(ya, ys, ga, gb, x2d, gt1, sc2, sh2, gt2, g_post_mix, g_pre_ffn, g_post_ffn, w_out, w_up, w_down, tm):
    rows = x2d.shape[0]
    row = pl.BlockSpec((tm, D_MODEL), lambda i: (i, 0))
    vec = _resident((1, D_MODEL))
    return pl.pallas_call(
        _outffn_kernel,
        grid=(rows // tm,),
        in_specs=[row, row, row, row, row,
                  _mod_spec(gt1, tm), _mod_spec(sc2, tm), _mod_spec(sh2, tm), _mod_spec(gt2, tm),
                  vec, vec, vec,
                  _resident((D_MODEL, D_MODEL)), _resident((D_MODEL, D_FF)), _resident((D_FF, D_MODEL))],
        out_specs=row,
        out_shape=jax.ShapeDtypeStruct((rows, D_MODEL), F32),
        compiler_params=_params("parallel"),
        name="out_ffn",
    )(ya, ys, ga, gb, x2d, gt1, sc2, sh2, gt2, g_post_mix, g_pre_ffn, g_post_ffn, w_out, w_up, w_down)


def _layer(x, mod, past, wts, *, tm, tq, Q):
    b, L, _ = x.shape
    x2d = x.reshape(b * L, D_MODEL)
    sh1, sc1, gt1, sh2, sc2, gt2 = mod
    q, kf, vf, kb, vb, z, xbc, ga, gb, small = _in_proj(
        x2d, wts["g_pre_mix"], sc1, sh1, wts["w_big"], wts["w_small"], wts["b_small"], tm, past is None)

    if past is None:
        assert b == 1
        y_att = _fox_prompt(q, kb, vb, tq)
        conv_past = jnp.zeros((b, CONV_WIDTH - 1, CONV_DIM), F32)
        ssm_past = jnp.zeros((b, SSM_INNER, SSM_STATE), F32)
    else:
        k_past, v_past, logf_past, conv_past, ssm_past = past
        P = k_past.shape[1]
        lpt = jnp.swapaxes(logf_past, 1, 2)
        lnt = jnp.swapaxes(small[:, :FOX_HEADS].reshape(b, L, FOX_HEADS), 1, 2)
        y_att = _fox_cached(q, kb, vb, k_past.reshape(b, P, D_MODEL), v_past.reshape(b, P, D_MODEL), lpt, lnt, L)
        ssm_past = ssm_past.reshape(b, SSM_INNER, SSM_STATE)

    y_ssm, conv_new, ssm_new = _ssd(xbc, z, small, conv_past, ssm_past, wts["conv_w"], wts["conv_b"],
                                    wts["alog_pad"], wts["d_exp"], wts["ssm_norm_g"], b, Q)
    y = _out_ffn(y_att, y_ssm, ga, gb, x2d, gt1, sc2, sh2, gt2, wts["g_post_mix"], wts["g_pre_ffn"],
                 wts["g_post_ffn"], wts["w_out"], wts["w_up"], wts["w_down"], tm)
    return (y.reshape(b, L, D_MODEL),
            kf.reshape(b, L, FOX_HEADS, FOX_HEAD_DIM), vf.reshape(b, L, FOX_HEADS, FOX_HEAD_DIM),
            small[:, :FOX_HEADS].reshape(b, L, FOX_HEADS), conv_new,
            ssm_new.reshape(b, SSM_HEADS, SSM_HEAD_DIM, SSM_STATE))


def _prep_weights(w_ada, b_ada, g_pre_mix, g_post_mix, g_pre_ffn, g_post_ffn, w_in, b_f, conv_w, conv_b,
                  dt_bias, a_log, d_skip, ssm_norm_g, w_out, w_up, w_down):
    sizes = (D_MODEL, D_MODEL, D_MODEL, FOX_HEADS, SSM_INNER, CONV_DIM, SSM_HEADS, D_MODEL, D_MODEL)
    offs = [0]
    for s in sizes:
        offs.append(offs[-1] + s)
    piece = lambda i: w_in[:, offs[i]:offs[i + 1]]
    pad = LANES - FOX_HEADS - SSM_HEADS
    row = lambda v: v.reshape(1, -1).astype(F32)
    return {
        "w_big": jnp.concatenate([piece(0), piece(1), piece(2), piece(4), piece(5), piece(7), piece(8)],
                                 axis=1).astype(BF16),
        "w_small": jnp.pad(jnp.concatenate([piece(3), piece(6)], axis=1), ((0, 0), (0, pad))).astype(BF16),
        "b_small": jnp.pad(jnp.concatenate([b_f, dt_bias]), (0, pad)).reshape(1, LANES).astype(F32),
        "alog_pad": jnp.pad(a_log, (DT_LANE0, LANES - DT_LANE0 - SSM_HEADS)).reshape(1, LANES).astype(F32),
        "d_exp": jnp.repeat(d_skip, SSM_HEAD_DIM).reshape(1, SSM_INNER).astype(F32),
        "g_pre_mix": row(g_pre_mix), "g_post_mix": row(g_post_mix),
        "g_pre_ffn": row(g_pre_ffn), "g_post_ffn": row(g_post_ffn),
        "conv_w": conv_w.astype(F32), "conv_b": row(conv_b), "ssm_norm_g": row(ssm_norm_g),
        "w_out": w_out.astype(BF16), "w_up": w_up.astype(BF16), "w_down": w_down.astype(BF16),
    }


def _forward(x_prompt, x_sample, c_prompt, c_sample, cache_fox_k, cache_fox_v, cache_fox_logf,
             state_ssm_conv, state_ssm, w_ada, b_ada, *layer_w, tm_prompt, tq, q_prompt):
    depth = w_ada.shape[0]
    bp, Lp, _ = x_prompt.shape
    bs, Ls, _ = x_sample.shape
    yp, ys = x_prompt, x_sample
    outs_p, outs_s = [], []
    for i in range(depth):
        wts = _prep_weights(w_ada[i], b_ada[i], *[w[i] for w in layer_w])
        mod = _ada(jnp.concatenate([c_prompt, c_sample], axis=0), w_ada[i], b_ada[i].reshape(1, -1))
        mod_p = [m for m in jnp.split(mod[:bp], N_MOD, axis=-1)]
        mod_s = [jnp.repeat(m, Ls, axis=0) for m in jnp.split(mod[bp:], N_MOD, axis=-1)]
        rp = _layer(yp, mod_p, None, wts, tm=tm_prompt, tq=tq, Q=q_prompt)
        rs = _layer(ys, mod_s, (cache_fox_k[i], cache_fox_v[i], cache_fox_logf[i], state_ssm_conv[i], state_ssm[i]),
                    wts, tm=bs * Ls, tq=None, Q=Ls)
        yp, ys = rp[0], rs[0]
        outs_p.append(rp[1:])
        outs_s.append(rs[1:])
    stack = lambda outs, j: jnp.stack([o[j] for o in outs])
    return (yp, ys) + tuple(stack(outs_p, j) for j in range(5)) + tuple(stack(outs_s, j) for j in range(5))


def kernel(x_prompt, x_sample, c_prompt, c_sample, cache_fox_k, cache_fox_v, cache_fox_logf, state_ssm_conv,
           state_ssm, w_ada, b_ada, g_pre_mix, g_post_mix, g_pre_ffn, g_post_ffn, w_in, b_f, conv_w, conv_b,
           dt_bias, a_log, d_skip, ssm_norm_g, w_out, w_up, w_down):
    assert x_prompt.shape[0] == 1, "the prompt path carries one sequence"
    L = x_prompt.shape[1]
    return _forward(x_prompt, x_sample, c_prompt, c_sample, cache_fox_k, cache_fox_v, cache_fox_logf,
                    state_ssm_conv, state_ssm, w_ada, b_ada, g_pre_mix, g_post_mix, g_pre_ffn, g_post_ffn,
                    w_in, b_f, conv_w, conv_b, dt_bias, a_log, d_skip, ssm_norm_g, w_out, w_up, w_down,
                    tm_prompt=min(512, L), tq=min(512, L), q_prompt=min(256, L))
```

```python
import functools

import jax
import jax.numpy as jnp
from jax import lax
from jax.experimental import pallas as pl
from jax.experimental.pallas import tpu as pltpu

F32 = jnp.float32
BF16 = jnp.bfloat16
HIGHEST = lax.Precision.HIGHEST

D_MODEL = 1024
FOX_HEADS = 8
FOX_HEAD_DIM = 128
FOX_SCALE = FOX_HEAD_DIM ** -0.5
SSM_HEADS = 16
SSM_HEAD_DIM = 64
SSM_GROUPS = 4
SSM_STATE = 128
SSM_INNER = SSM_HEADS * SSM_HEAD_DIM
CONV_WIDTH = 4
CONV_DIM = SSM_INNER + 2 * SSM_GROUPS * SSM_STATE
D_FF = 4 * D_MODEL
N_MOD = 6
EPS = 1e-6

LANES = 128
SUBLANES = 8
DT_LANE0 = FOX_HEADS
VMEM_LIMIT = 56 * 1024 * 1024
LOG2E = 1.4426950408889634
BIAS_PARTS = 3
CUM_ROWS = 128
ROW_TILE = 16
UNROLL = 4

BIG_WIDTH = 3 * D_MODEL + SSM_INNER + CONV_DIM + 2 * D_MODEL

_NT = (((1,), (1,)), ((), ()))
_TN = (((0,), (0,)), ((), ()))


def _params(*sem):
    return pltpu.CompilerParams(dimension_semantics=sem, vmem_limit_bytes=VMEM_LIMIT)


def _resident(shape):
    zeros = (0,) * len(shape)
    return pl.BlockSpec(shape, lambda *_: zeros, pipeline_mode=pl.Buffered(1))


def _rms(x, g):
    return x * lax.rsqrt(jnp.mean(x * x, axis=-1, keepdims=True) + EPS) * g


def _sigmoid(x):
    return 1.0 / (1.0 + jnp.exp(-x))


def _silu(x):
    return x * _sigmoid(x)


def _ada_kernel(c_ref, w_ref, b_ref, o_ref):
    s = _silu(c_ref[...]).astype(BF16)
    o_ref[...] = jnp.dot(s, w_ref[...].astype(BF16), preferred_element_type=F32) + b_ref[...]


def _ada(c_all, w_ada, b_ada):
    rows = c_all.shape[0]
    width = w_ada.shape[1]
    tn = 1536
    return pl.pallas_call(
        _ada_kernel,
        grid=(width // tn,),
        in_specs=[pl.BlockSpec((rows, D_MODEL), lambda j: (0, 0)),
                  pl.BlockSpec((D_MODEL, tn), lambda j: (0, j)),
                  pl.BlockSpec((1, tn), lambda j: (0, j))],
        out_specs=pl.BlockSpec((rows, tn), lambda j: (0, j)),
        out_shape=jax.ShapeDtypeStruct((rows, width), F32),
        compiler_params=_params("parallel"),
        name="ada_mod",
    )(c_all, w_ada, b_ada)


def _inproj_kernel(x_ref, g_ref, sc_ref, sh_ref, wb_ref, ws_ref, bs_ref, *rest, with_bias):
    if with_bias:
        sel_ref, q_ref, kf_ref, vf_ref, kb_ref, vb_ref, z_ref, xbc_ref, ga_ref, gb_ref, sm_ref, carry_ref = rest
    else:
        q_ref, kf_ref, vf_ref, kb_ref, vb_ref, z_ref, xbc_ref, ga_ref, gb_ref, sm_ref = rest
    h = _rms(x_ref[...], g_ref[...]) * (1.0 + sc_ref[...]) + sh_ref[...]
    hb = h.astype(BF16)

    def proj(lo, width):
        return jnp.dot(hb, wb_ref[:, lo:lo + width], preferred_element_type=F32)

    u = jnp.dot(hb, ws_ref[...], preferred_element_type=F32) + bs_ref[...]
    t = jnp.log1p(jnp.exp(-jnp.abs(u)))
    lane = lax.broadcasted_iota(jnp.int32, u.shape, 1)
    sm = jnp.where(lane < DT_LANE0, jnp.minimum(u, 0.0) - t,
                   jnp.where(lane < DT_LANE0 + SSM_HEADS, jnp.maximum(u, 0.0) + t, 0.0))
    sm_ref[...] = sm

    q_ref[...] = (proj(0, D_MODEL) * (FOX_SCALE * LOG2E)).astype(BF16)
    k = proj(D_MODEL, D_MODEL)
    for hd in range(FOX_HEADS):
        kf_ref[:, hd, :] = k[:, hd * FOX_HEAD_DIM:(hd + 1) * FOX_HEAD_DIM]
    if with_bias:
        @pl.when(pl.program_id(0) == 0)
        def _():
            carry_ref[...] = jnp.zeros_like(carry_ref)

        r = lax.broadcasted_iota(jnp.int32, (CUM_ROWS, CUM_ROWS), 0)
        c = lax.broadcasted_iota(jnp.int32, (CUM_ROWS, CUM_ROWS), 1)
        tri = (r >= c).astype(F32)
        run = carry_ref[0:1, :]
        parts = []
        for b in range(sm.shape[0] // CUM_ROWS):
            cs = jnp.dot(tri, sm[b * CUM_ROWS:(b + 1) * CUM_ROWS, :], precision=HIGHEST,
                         preferred_element_type=F32) + run
            run = cs[CUM_ROWS - 1:CUM_ROWS, :]
            parts.append(cs)
        carry_ref[0:1, :] = run
        nb = jnp.concatenate(parts, axis=0) * (-LOG2E)
        hi = nb.astype(BF16)
        r1 = nb - hi.astype(F32)
        mid = r1.astype(BF16)
        lo = (r1 - mid.astype(F32)).astype(BF16)
        aug = jnp.dot(jnp.concatenate([hi, mid, lo], axis=1), sel_ref[...], preferred_element_type=F32)
        for hd in range(FOX_HEADS):
            sl = slice(hd * FOX_HEAD_DIM, (hd + 1) * FOX_HEAD_DIM)
            kb_ref[:, 2 * hd * FOX_HEAD_DIM:(2 * hd + 1) * FOX_HEAD_DIM] = k[:, sl].astype(BF16)
            kb_ref[:, (2 * hd + 1) * FOX_HEAD_DIM:(2 * hd + 2) * FOX_HEAD_DIM] = aug[:, sl].astype(BF16)
    else:
        kb_ref[...] = k.astype(BF16)
    v = proj(2 * D_MODEL, D_MODEL)
    for hd in range(FOX_HEADS):
        vf_ref[:, hd, :] = v[:, hd * FOX_HEAD_DIM:(hd + 1) * FOX_HEAD_DIM]
    vb_ref[...] = v.astype(BF16)
    z_ref[...] = proj(3 * D_MODEL, SSM_INNER).astype(BF16)
    off = 3 * D_MODEL + SSM_INNER
    xbc_ref[...] = proj(off, CONV_DIM)
    ga_ref[...] = proj(off + CONV_DIM, D_MODEL).astype(BF16)
    gb_ref[...] = proj(off + CONV_DIM + D_MODEL, D_MODEL).astype(BF16)


def _mod_spec(arr, tm):
    if arr.shape[0] == 1:
        return pl.BlockSpec((1, D_MODEL), lambda i: (0, 0))
    return pl.BlockSpec((tm, D_MODEL), lambda i: (i, 0))


def _bias_selector():
    r = lax.broadcasted_iota(jnp.int32, (BIAS_PARTS * LANES, FOX_HEADS * FOX_HEAD_DIM), 0)
    c = lax.broadcasted_iota(jnp.int32, (BIAS_PARTS * LANES, FOX_HEADS * FOX_HEAD_DIM), 1)
    hit = (r % LANES < FOX_HEADS) & (c == (r % LANES) * FOX_HEAD_DIM + r // LANES)
    return hit.astype(BF16)


def _in_proj(x2d, g, sc, sh, w_big, w_small, b_small, tm, with_bias):
    rows = x2d.shape[0]
    assert not with_bias or tm % CUM_ROWS == 0
    row = lambda w: pl.BlockSpec((tm, w), lambda i: (i, 0))
    shp = lambda w, dt: jax.ShapeDtypeStruct((rows, w), dt)
    kb_width = 2 * D_MODEL if with_bias else D_MODEL
    heads = pl.BlockSpec((tm, FOX_HEADS, FOX_HEAD_DIM), lambda i: (i, 0, 0))
    heads_shape = jax.ShapeDtypeStruct((rows, FOX_HEADS, FOX_HEAD_DIM), F32)
    in_specs = [row(D_MODEL), _resident((1, D_MODEL)), _mod_spec(sc, tm), _mod_spec(sh, tm),
                _resident((D_MODEL, BIG_WIDTH)), _resident((D_MODEL, LANES)), _resident((1, LANES))]
    args = [x2d, g, sc, sh, w_big, w_small, b_small]
    if with_bias:
        in_specs.append(_resident((BIAS_PARTS * LANES, FOX_HEADS * FOX_HEAD_DIM)))
        args.append(_bias_selector())
    return pl.pallas_call(
        functools.partial(_inproj_kernel, with_bias=with_bias),
        grid=(rows // tm,),
        in_specs=in_specs,
        out_specs=[row(D_MODEL), heads, heads, row(kb_width), row(D_MODEL),
                   row(SSM_INNER), row(CONV_DIM), row(D_MODEL), row(D_MODEL), row(LANES)],
        out_shape=[shp(D_MODEL, BF16), heads_shape, heads_shape, shp(kb_width, BF16),
                   shp(D_MODEL, BF16), shp(SSM_INNER, BF16), shp(CONV_DIM, F32), shp(D_MODEL, BF16),
                   shp(D_MODEL, BF16), shp(LANES, F32)],
        scratch_shapes=[pltpu.VMEM((SUBLANES, LANES), F32)] if with_bias else [],
        compiler_params=_params("arbitrary" if with_bias else "parallel"),
        name="in_proj",
    )(*args)


def _flash_kernel(q_ref, k_ref, v_ref, o_ref, s0_ref, s1_ref, p0_ref, p1_ref, m_ref, l_ref, alpha0_ref,
                  alpha1_ref, acc_ref, *, tq, tk):
    i = pl.program_id(1)
    nlt = tk // LANES
    lane = lax.broadcasted_iota(jnp.int32, (tq, FOX_HEAD_DIM), 1)
    ones = jnp.where(lane < BIAS_PARTS, 1.0, 0.0).astype(BF16)
    q2 = jnp.concatenate([q_ref[...], ones], axis=1)

    def scores(j, s_ref):
        off = pl.multiple_of(j * tk, tk)
        s_ref[...] = lax.dot_general(q2, k_ref[pl.ds(off, tk), :], _NT, preferred_element_type=F32)

    def softmax(s_ref, p_ref, alpha_ref, mask_shift):
        for rt in range(tq // ROW_TILE):
            rows = slice(rt * ROW_TILE, (rt + 1) * ROW_TILE)
            cols = [s_ref[rows, c * LANES:(c + 1) * LANES] for c in range(nlt)]
            if mask_shift is not None:
                ri = lax.broadcasted_iota(jnp.int32, (ROW_TILE, LANES), 0) + rt * ROW_TILE
                ci = lax.broadcasted_iota(jnp.int32, (ROW_TILE, LANES), 1) + mask_shift
                cols = [jnp.where(ci + c * LANES <= ri, cols[c], -jnp.inf) for c in range(nlt)]
            mx = functools.reduce(jnp.maximum, cols)
            m_old = m_ref[rows, :]
            m_new = jnp.maximum(m_old, jnp.max(mx, axis=-1, keepdims=True))
            alpha = jnp.exp2(m_old - m_new)
            ps = [jnp.exp2(cc - m_new) for cc in cols]
            l_ref[rows, :] = alpha * l_ref[rows, :] + jnp.sum(functools.reduce(jnp.add, ps), axis=-1, keepdims=True)
            m_ref[rows, :] = m_new
            alpha_ref[rows, :] = alpha
            for c in range(nlt):
                p_ref[rows, c * LANES:(c + 1) * LANES] = ps[c].astype(BF16)

    def weighted_values(j, p_ref, alpha_ref):
        off = pl.multiple_of(j * tk, tk)
        acc_ref[...] = alpha_ref[...] * acc_ref[...] + jnp.dot(p_ref[...], v_ref[pl.ds(off, tk), :],
                                                               preferred_element_type=F32)

    m_ref[...] = jnp.full(m_ref.shape, -jnp.inf, F32)
    l_ref[...] = jnp.zeros(l_ref.shape, F32)
    acc_ref[...] = jnp.zeros(acc_ref.shape, F32)
    bufs = ((s0_ref, p0_ref, alpha0_ref), (s1_ref, p1_ref, alpha1_ref))

    def run(first, n, diagonal_last, next_scores):
        for k in range(n):
            s_ref, p_ref, alpha_ref = bufs[k % 2]
            s_nxt, p_prv, alpha_prv = bufs[(k + 1) % 2]
            weighted_values(jnp.maximum(first + k - 1, 0), p_prv, alpha_prv)
            if k + 1 < n or next_scores:
                scores(first + k + 1, s_nxt)
            softmax(s_ref, p_ref, alpha_ref, 0 if (diagonal_last and k == n - 1) else None)

    p1_ref[...] = jnp.zeros(p1_ref.shape, BF16)
    alpha1_ref[...] = jnp.ones(alpha1_ref.shape, F32)
    scores(0, s0_ref)

    @pl.loop(0, i // UNROLL)
    def _(t):
        run(UNROLL * t, UNROLL, False, True)

    for rem in range(UNROLL):
        @pl.when(i % UNROLL == rem)
        def _():
            run(i - rem, rem + 1, True, False)
            weighted_values(i, *bufs[rem % 2][1:])

    o_ref[...] = (acc_ref[...] / l_ref[...]).astype(o_ref.dtype)


def _fox_prompt(q, kp, v, tq):
    L = q.shape[0]
    tk = tq
    return pl.pallas_call(
        functools.partial(_flash_kernel, tq=tq, tk=tk),
        grid=(FOX_HEADS, L // tq),
        in_specs=[pl.BlockSpec((tq, FOX_HEAD_DIM), lambda h, i: (i, h)),
                  pl.BlockSpec((L, 2 * FOX_HEAD_DIM), lambda h, i: (0, h)),
                  pl.BlockSpec((L, FOX_HEAD_DIM), lambda h, i: (0, h))],
        out_specs=pl.BlockSpec((tq, FOX_HEAD_DIM), lambda h, i: (i, h)),
        out_shape=jax.ShapeDtypeStruct((L, FOX_HEADS * FOX_HEAD_DIM), BF16),
        scratch_shapes=[pltpu.VMEM((tq, tk), F32)] * 2 + [pltpu.VMEM((tq, tk), BF16)] * 2
        + [pltpu.VMEM((tq, FOX_HEAD_DIM), F32)] * 5,
        compiler_params=_params("parallel", "arbitrary"),
        name="fox_flash",
    )(q, kp, v)


def _cached_attn_kernel(q_ref, kn_ref, vn_ref, kc_ref, vc_ref, lpt_ref, lnt_ref, o_ref):
    n = q_ref.shape[0]
    P = kc_ref.shape[0]
    r = lax.broadcasted_iota(jnp.int32, (P, P), 0)
    c = lax.broadcasted_iota(jnp.int32, (P, P), 1)
    G = jnp.dot(lpt_ref[...], (r > c).astype(F32), precision=HIGHEST, preferred_element_type=F32) * LOG2E
    rn = lax.broadcasted_iota(jnp.int32, (n, n), 0)
    cn = lax.broadcasted_iota(jnp.int32, (n, n), 1)
    Hn = jnp.dot(lnt_ref[...], (rn <= cn).astype(F32), precision=HIGHEST, preferred_element_type=F32) * LOG2E
    causal = cn <= rn
    for h in range(FOX_HEADS):
        sl = slice(h * FOX_HEAD_DIM, (h + 1) * FOX_HEAD_DIM)
        qh = q_ref[:, sl]
        sp = lax.dot_general(qh, kc_ref[:, sl].astype(BF16), _NT, preferred_element_type=F32) + G[h:h + 1, :]
        sn = lax.dot_general(qh, kn_ref[:, sl], _NT, preferred_element_type=F32) - Hn[h:h + 1, :]
        sn = jnp.where(causal, sn, -jnp.inf)
        m = jnp.maximum(jnp.max(sp, axis=-1, keepdims=True), jnp.max(sn, axis=-1, keepdims=True))
        pp = jnp.exp2(sp - m)
        pn = jnp.exp2(sn - m)
        l = jnp.sum(pp, axis=-1, keepdims=True) + jnp.sum(pn, axis=-1, keepdims=True)
        o = (jnp.dot(pp.astype(BF16), vc_ref[:, sl].astype(BF16), preferred_element_type=F32)
             + jnp.dot(pn.astype(BF16), vn_ref[:, sl], preferred_element_type=F32))
        o_ref[:, sl] = (o / l).astype(o_ref.dtype)


def _fox_cached(q, kn, vn, kc, vc, lpt, lnt, n):
    B, P, W = kc.shape
    row = pl.BlockSpec((n, W), lambda b: (b, 0))
    return pl.pallas_call(
        _cached_attn_kernel,
        grid=(B,),
        in_specs=[row, row, row,
                  pl.BlockSpec((None, P, W), lambda b: (b, 0, 0)),
                  pl.BlockSpec((None, P, W), lambda b: (b, 0, 0)),
                  pl.BlockSpec((None, FOX_HEADS, P), lambda b: (b, 0, 0)),
                  pl.BlockSpec((None, FOX_HEADS, n), lambda b: (b, 0, 0))],
        out_specs=row,
        out_shape=jax.ShapeDtypeStruct((B * n, W), BF16),
        compiler_params=_params("parallel"),
        name="fox_cached",
    )(q, kn, vn, kc, vc, lpt, lnt)


def _ssd_kernel(xbc_ref, z_ref, sm_ref, convp_ref, ssmp_ref, cw_ref, cb_ref, alog_ref, dexp_ref, ng_ref,
                y_ref, convn_ref, ssmn_ref, ext_ref, st_ref, *, Q, nchunks):
    ci = pl.program_id(1)
    PAIR = 2 * SSM_HEAD_DIM
    GW = SSM_STATE

    @pl.when(ci == 0)
    def _():
        st_ref[...] = ssmp_ref[...]
        ext_ref[0:SUBLANES, :] = jnp.zeros((SUBLANES, CONV_DIM), F32)
        ext_ref[SUBLANES - (CONV_WIDTH - 1):SUBLANES, :] = convp_ref[...]

    ext_ref[SUBLANES:SUBLANES + Q, :] = xbc_ref[...]
    base = SUBLANES - (CONV_WIDTH - 1)
    conv = cb_ref[...]
    for i in range(CONV_WIDTH):
        conv = conv + ext_ref[base + i:base + i + Q, :] * cw_ref[i:i + 1, :]
    convn_ref[...] = ext_ref[Q + base:Q + SUBLANES, :]
    ext_ref[0:SUBLANES, :] = ext_ref[Q:Q + SUBLANES, :]

    xc = _silu(conv)
    xs = xc[:, :SSM_INNER]
    Bm = xc[:, SSM_INNER:SSM_INNER + SSM_GROUPS * GW]
    Cm = xc[:, SSM_INNER + SSM_GROUPS * GW:]

    sm = sm_ref[...]
    a = sm * (-jnp.exp(alog_ref[...]))
    r = lax.broadcasted_iota(jnp.int32, (Q, Q), 0)
    c = lax.broadcasted_iota(jnp.int32, (Q, Q), 1)
    causal = r >= c
    a_cs = jnp.dot(causal.astype(F32), a, precision=HIGHEST, preferred_element_type=F32)
    er = lax.broadcasted_iota(jnp.int32, (SSM_HEADS, LANES), 0)
    ec = lax.broadcasted_iota(jnp.int32, (SSM_HEADS, LANES), 1)
    pick = (ec == er + DT_LANE0).astype(F32)
    acs_t = lax.dot_general(pick, a_cs, _NT, precision=HIGHEST, preferred_element_type=F32)
    dt_t = lax.dot_general(pick, sm, _NT, precision=HIGHEST, preferred_element_type=F32)
    a_last = a_cs[Q - 1:Q, :]
    e_cs = jnp.exp(a_cs)
    w_in = jnp.exp(a_last - a_cs) * sm
    e_last = jnp.exp(a_last)

    lane_lo = lax.broadcasted_iota(jnp.int32, (Q, PAIR), 1) < SSM_HEAD_DIM
    row_lo = lax.broadcasted_iota(jnp.int32, (PAIR, GW), 0) < SSM_HEAD_DIM

    y_parts = []
    for g in range(SSM_GROUPS):
        Bg = Bm[:, g * GW:(g + 1) * GW]
        Cg = Cm[:, g * GW:(g + 1) * GW]
        cbm = lax.dot_general(Cg.astype(BF16), Bg.astype(BF16), _NT, preferred_element_type=F32)
        for pr in range(SSM_HEADS // SSM_GROUPS // 2):
            pair = g * (SSM_HEADS // SSM_GROUPS // 2) + pr
            xs_pair = xs[:, pair * PAIR:(pair + 1) * PAIR]
            xsb = xs_pair.astype(BF16)
            s_prev = st_ref[pair * PAIR:(pair + 1) * PAIR, :]
            s_prev_b = s_prev.astype(BF16)
            ys, sus = [], []
            for hh in (2 * pair, 2 * pair + 1):
                ln = DT_LANE0 + hh
                lm = jnp.exp(jnp.where(causal, a_cs[:, ln:ln + 1] - acs_t[hh:hh + 1, :], -jnp.inf))
                mat = (cbm * lm * dt_t[hh:hh + 1, :]).astype(BF16)
                y_diag = jnp.dot(mat, xsb, preferred_element_type=F32)
                cw = (Cg * e_cs[:, ln:ln + 1]).astype(BF16)
                y_off = lax.dot_general(cw, s_prev_b, _NT, preferred_element_type=F32)
                bw = (Bg * w_in[:, ln:ln + 1]).astype(BF16)
                sus.append(lax.dot_general(xsb, bw, _TN, preferred_element_type=F32))
                ys.append(y_diag + y_off)
            ln0 = DT_LANE0 + 2 * pair
            dec = jnp.where(row_lo, e_last[:, ln0:ln0 + 1], e_last[:, ln0 + 1:ln0 + 2])
            st_ref[pair * PAIR:(pair + 1) * PAIR, :] = s_prev * dec + jnp.where(row_lo, sus[0], sus[1])
            y_parts.append(jnp.where(lane_lo, ys[0], ys[1])
                           + xs_pair * dexp_ref[:, pair * PAIR:(pair + 1) * PAIR])

    y = jnp.concatenate(y_parts, axis=1) * _silu(z_ref[...].astype(F32))
    gw = SSM_INNER // SSM_GROUPS
    normed = []
    for g in range(SSM_GROUPS):
        yg = y[:, g * gw:(g + 1) * gw]
        normed.append(yg * lax.rsqrt(jnp.mean(yg * yg, axis=-1, keepdims=True) + EPS))
    y_ref[...] = (jnp.concatenate(normed, axis=1) * ng_ref[...]).astype(y_ref.dtype)

    @pl.when(ci == nchunks - 1)
    def _():
        ssmn_ref[...] = st_ref[...]


def _ssd(xbc, z, small, conv_past, ssm_past, conv_w, conv_b, alog_pad, d_exp, norm_g, batch, Q):
    rows = xbc.shape[0]
    nchunks = rows // batch // Q
    assert Q % SUBLANES == 0 and Q >= SUBLANES and nchunks * Q * batch == rows
    row = lambda w: pl.BlockSpec((Q, w), lambda b, c: (b * nchunks + c, 0))
    state_rows = SSM_HEADS * SSM_HEAD_DIM
    per_seq = lambda a, b_: pl.BlockSpec((None, a, b_), lambda b, c: (b, 0, 0))
    return pl.pallas_call(
        functools.partial(_ssd_kernel, Q=Q, nchunks=nchunks),
        grid=(batch, nchunks),
        in_specs=[row(CONV_DIM), row(SSM_INNER), row(LANES),
                  per_seq(CONV_WIDTH - 1, CONV_DIM), per_seq(state_rows, SSM_STATE),
                  _resident((CONV_WIDTH, CONV_DIM)), _resident((1, CONV_DIM)), _resident((1, LANES)),
                  _resident((1, SSM_INNER)), _resident((1, SSM_INNER))],
        out_specs=[row(SSM_INNER), per_seq(CONV_WIDTH - 1, CONV_DIM), per_seq(state_rows, SSM_STATE)],
        out_shape=[jax.ShapeDtypeStruct((rows, SSM_INNER), BF16),
                   jax.ShapeDtypeStruct((batch, CONV_WIDTH - 1, CONV_DIM), F32),
                   jax.ShapeDtypeStruct((batch, state_rows, SSM_STATE), F32)],
        scratch_shapes=[pltpu.VMEM((Q + SUBLANES, CONV_DIM), F32), pltpu.VMEM((state_rows, SSM_STATE), F32)],
        compiler_params=_params("parallel", "arbitrary"),
        name="ssd_scan",
    )(xbc, z, small, conv_past, ssm_past, conv_w, conv_b, alog_pad, d_exp, norm_g)


def _outffn_kernel(ya_ref, ys_ref, ga_ref, gb_ref, x_ref, gt1_ref, sc2_ref, sh2_ref, gt2_ref,
                   gpm_ref, gpf_ref, gqf_ref, wo_ref, wu_ref, wd_ref, o_ref):
    merged = (_sigmoid(ga_ref[...].astype(F32)) * ya_ref[...].astype(F32)
              + _sigmoid(gb_ref[...].astype(F32)) * ys_ref[...].astype(F32))
    m = jnp.dot(merged.astype(BF16), wo_ref[...], preferred_element_type=F32)
    x1 = x_ref[...] + gt1_ref[...] * _rms(m, gpm_ref[...])
    hb = (_rms(x1, gpf_ref[...]) * (1.0 + sc2_ref[...]) + sh2_ref[...]).astype(BF16)
    f = jnp.zeros(x1.shape, F32)
    for cc in range(D_FF // D_MODEL):
        sl = slice(cc * D_MODEL, (cc + 1) * D_MODEL)
        up = jnp.dot(hb, wu_ref[:, sl], preferred_element_type=F32)
        act = jnp.square(jnp.maximum(up, 0.0)).astype(BF16)
        f = f + jnp.dot(act, wd_ref[sl, :], preferred_element_type=F32)
    o_ref[...] = x1 + gt2_ref[...] * _rms(f, gqf_ref[...])


def _out_ffn(ya, ys, ga, gb, x2d, gt1, sc2, sh2, gt2, g_post_mix, g_pre_ffn, g_post_ffn, w_out, w_up, w_down, tm):
    rows = x2d.shape[0]
    row = pl.BlockSpec((tm, D_MODEL), lambda i: (i, 0))
    vec = _resident((1, D_MODEL))
    return pl.pallas_call(
        _outffn_kernel,
        grid=(rows // tm,),
        in_specs=[row, row, row, row, row,
                  _mod_spec(gt1, tm), _mod_spec(sc2, tm), _mod_spec(sh2, tm), _mod_spec(gt2, tm),
                  vec, vec, vec,
                  _resident((D_MODEL, D_MODEL)), _resident((D_MODEL, D_FF)), _resident((D_FF, D_MODEL))],
        out_specs=row,
        out_shape=jax.ShapeDtypeStruct((rows, D_MODEL), F32),
        compiler_params=_params("parallel"),
        name="out_ffn",
    )(ya, ys, ga, gb, x2d, gt1, sc2, sh2, gt2, g_post_mix, g_pre_ffn, g_post_ffn, w_out, w_up, w_down)


def _layer(x, mod, past, wts, *, tm, tq, Q):
    b, L, _ = x.shape
    x2d = x.reshape(b * L, D_MODEL)
    sh1, sc1, gt1, sh2, sc2, gt2 = mod
    q, kf, vf, kb, vb, z, xbc, ga, gb, small = _in_proj(
        x2d, wts["g_pre_mix"], sc1, sh1, wts["w_big"], wts["w_small"], wts["b_small"], tm, past is None)

    if past is None:
        assert b == 1
        y_att = _fox_prompt(q, kb, vb, tq)
        conv_past = jnp.zeros((b, CONV_WIDTH - 1, CONV_DIM), F32)
        ssm_past = jnp.zeros((b, SSM_INNER, SSM_STATE), F32)
    else:
        k_past, v_past, logf_past, conv_past, ssm_past = past
        P = k_past.shape[1]
        lpt = jnp.swapaxes(logf_past, 1, 2)
        lnt = jnp.swapaxes(small[:, :FOX_HEADS].reshape(b, L, FOX_HEADS), 1, 2)
        y_att = _fox_cached(q, kb, vb, k_past.reshape(b, P, D_MODEL), v_past.reshape(b, P, D_MODEL), lpt, lnt, L)
        ssm_past = ssm_past.reshape(b, SSM_INNER, SSM_STATE)

    y_ssm, conv_new, ssm_new = _ssd(xbc, z, small, conv_past, ssm_past, wts["conv_w"], wts["conv_b"],
                                    wts["alog_pad"], wts["d_exp"], wts["ssm_norm_g"], b, Q)
    y = _out_ffn(y_att, y_ssm, ga, gb, x2d, gt1, sc2, sh2, gt2, wts["g_post_mix"], wts["g_pre_ffn"],
                 wts["g_post_ffn"], wts["w_out"], wts["w_up"], wts["w_down"], tm)
    return (y.reshape(b, L, D_MODEL),
            kf.reshape(b, L, FOX_HEADS, FOX_HEAD_DIM), vf.reshape(b, L, FOX_HEADS, FOX_HEAD_DIM),
            small[:, :FOX_HEADS].reshape(b, L, FOX_HEADS), conv_new,
            ssm_new.reshape(b, SSM_HEADS, SSM_HEAD_DIM, SSM_STATE))


def _prep_weights(w_ada, b_ada, g_pre_mix, g_post_mix, g_pre_ffn, g_post_ffn, w_in, b_f, conv_w, conv_b,
                  dt_bias, a_log, d_skip, ssm_norm_g, w_out, w_up, w_down):
    sizes = (D_MODEL, D_MODEL, D_MODEL, FOX_HEADS, SSM_INNER, CONV_DIM, SSM_HEADS, D_MODEL, D_MODEL)
    offs = [0]
    for s in sizes:
        offs.append(offs[-1] + s)
    piece = lambda i: w_in[:, offs[i]:offs[i + 1]]
    pad = LANES - FOX_HEADS - SSM_HEADS
    row = lambda v: v.reshape(1, -1).astype(F32)
    return {
        "w_big": jnp.concatenate([piece(0), piece(1), piece(2), piece(4), piece(5), piece(7), piece(8)],
                                 axis=1).astype(BF16),
        "w_small": jnp.pad(jnp.concatenate([piece(3), piece(6)], axis=1), ((0, 0), (0, pad))).astype(BF16),
        "b_small": jnp.pad(jnp.concatenate([b_f, dt_bias]), (0, pad)).reshape(1, LANES).astype(F32),
        "alog_pad": jnp.pad(a_log, (DT_LANE0, LANES - DT_LANE0 - SSM_HEADS)).reshape(1, LANES).astype(F32),
        "d_exp": jnp.repeat(d_skip, SSM_HEAD_DIM).reshape(1, SSM_INNER).astype(F32),
        "g_pre_mix": row(g_pre_mix), "g_post_mix": row(g_post_mix),
        "g_pre_ffn": row(g_pre_ffn), "g_post_ffn": row(g_post_ffn),
        "conv_w": conv_w.astype(F32), "conv_b": row(conv_b), "ssm_norm_g": row(ssm_norm_g),
        "w_out": w_out.astype(BF16), "w_up": w_up.astype(BF16), "w_down": w_down.astype(BF16),
    }


def _forward(x_prompt, x_sample, c_prompt, c_sample, cache_fox_k, cache_fox_v, cache_fox_logf,
             state_ssm_conv, state_ssm, w_ada, b_ada, *layer_w, tm_prompt, tq, q_prompt):
    depth = w_ada.shape[0]
    bp, Lp, _ = x_prompt.shape
    bs, Ls, _ = x_sample.shape
    yp, ys = x_prompt, x_sample
    outs_p, outs_s = [], []
    for i in range(depth):
        wts = _prep_weights(w_ada[i], b_ada[i], *[w[i] for w in layer_w])
        mod = _ada(jnp.concatenate([c_prompt, c_sample], axis=0), w_ada[i], b_ada[i].reshape(1, -1))
        mod_p = [m for m in jnp.split(mod[:bp], N_MOD, axis=-1)]
        mod_s = [jnp.repeat(m, Ls, axis=0) for m in jnp.split(mod[bp:], N_MOD, axis=-1)]
        rp = _layer(yp, mod_p, None, wts, tm=tm_prompt, tq=tq, Q=q_prompt)
        rs = _layer(ys, mod_s, (cache_fox_k[i], cache_fox_v[i], cache_fox_logf[i], state_ssm_conv[i], state_ssm[i]),
                    wts, tm=bs * Ls, tq=None, Q=Ls)
        yp, ys = rp[0], rs[0]
        outs_p.append(rp[1:])
        outs_s.append(rs[1:])
    stack = lambda outs, j: jnp.stack([o[j] for o in outs])
    return (yp, ys) + tuple(stack(outs_p, j) for j in range(5)) + tuple(stack(outs_s, j) for j in range(5))


def kernel(x_prompt, x_sample, c_prompt, c_sample, cache_fox_k, cache_fox_v, cache_fox_logf, state_ssm_conv,
           state_ssm, w_ada, b_ada, g_pre_mix, g_post_mix, g_pre_ffn, g_post_ffn, w_in, b_f, conv_w, conv_b,
           dt_bias, a_log, d_skip, ssm_norm_g, w_out, w_up, w_down):
    assert x_prompt.shape[0] == 1, "the prompt path carries one sequence"
    L = x_prompt.shape[1]
    return _forward(x_prompt, x_sample, c_prompt, c_sample, cache_fox_k, cache_fox_v, cache_fox_logf,
                    state_ssm_conv, state_ssm, w_ada, b_ada, g_pre_mix, g_post_mix, g_pre_ffn, g_post_ffn,
                    w_in, b_f, conv_w, conv_b, dt_bias, a_log, d_skip, ssm_norm_g, w_out, w_up, w_down,
                    tm_prompt=min(512, L), tq=min(512, L), q_prompt=min(256, L))
```

```python
import functools

import jax
import jax.numpy as jnp
from jax import lax
from jax.experimental import pallas as pl
from jax.experimental.pallas import tpu as pltpu

F32 = jnp.float32
BF16 = jnp.bfloat16
HIGHEST = lax.Precision.HIGHEST

D_MODEL = 1024
FOX_HEADS = 8
FOX_HEAD_DIM = 128
FOX_SCALE = FOX_HEAD_DIM ** -0.5
SSM_HEADS = 16
SSM_HEAD_DIM = 64
SSM_GROUPS = 4
SSM_STATE = 128
SSM_INNER = SSM_HEADS * SSM_HEAD_DIM
CONV_WIDTH = 4
CONV_DIM = SSM_INNER + 2 * SSM_GROUPS * SSM_STATE
D_FF = 4 * D_MODEL
N_MOD = 6
EPS = 1e-6

LANES = 128
SUBLANES = 8
DT_LANE0 = FOX_HEADS
VMEM_LIMIT = 56 * 1024 * 1024
LOG2E = 1.4426950408889634
BIAS_PARTS = 3
PIECE_LANES = 32
CUM_ROWS = 128
ROW_TILE = 16
UNROLL = 4

BIG_WIDTH = 3 * D_MODEL + SSM_INNER + CONV_DIM + 2 * D_MODEL

_NT = (((1,), (1,)), ((), ()))
_TN = (((0,), (0,)), ((), ()))


def _params(*sem):
    return pltpu.CompilerParams(dimension_semantics=sem, vmem_limit_bytes=VMEM_LIMIT)


def _resident(shape):
    zeros = (0,) * len(shape)
    return pl.BlockSpec(shape, lambda *_: zeros, pipeline_mode=pl.Buffered(1))


def _rms(x, g):
    return x * lax.rsqrt(jnp.mean(x * x, axis=-1, keepdims=True) + EPS) * g


def _sigmoid(x):
    return 0.5 * jnp.tanh(0.5 * x) + 0.5


def _silu(x):
    h = 0.5 * x
    return h * jnp.tanh(h) + h


def _ada_kernel(c_ref, w_ref, b_ref, o_ref):
    s = _silu(c_ref[...]).astype(BF16)
    o_ref[...] = jnp.dot(s, w_ref[...].astype(BF16), preferred_element_type=F32) + b_ref[...]


def _ada(c_all, w_ada, b_ada):
    rows = c_all.shape[0]
    width = w_ada.shape[1]
    tn = 1536
    return pl.pallas_call(
        _ada_kernel,
        grid=(width // tn,),
        in_specs=[pl.BlockSpec((rows, D_MODEL), lambda j: (0, 0)),
                  pl.BlockSpec((D_MODEL, tn), lambda j: (0, j)),
                  pl.BlockSpec((1, tn), lambda j: (0, j))],
        out_specs=pl.BlockSpec((rows, tn), lambda j: (0, j)),
        out_shape=jax.ShapeDtypeStruct((rows, width), F32),
        compiler_params=_params("parallel"),
        name="ada_mod",
    )(c_all, w_ada, b_ada)


def _inproj_kernel(x_ref, g_ref, sc_ref, sh_ref, wb_ref, ws_ref, bs_ref, *rest, with_bias):
    if with_bias:
        sel_ref, q_ref, kf_ref, vf_ref, kb_ref, vb_ref, z_ref, xbc_ref, ga_ref, gb_ref, sm_ref, carry_ref = rest
    else:
        q_ref, kf_ref, vf_ref, kb_ref, vb_ref, z_ref, xbc_ref, ga_ref, gb_ref, sm_ref = rest
    h = _rms(x_ref[...], g_ref[...]) * (1.0 + sc_ref[...]) + sh_ref[...]
    hb = h.astype(BF16)

    def proj(lo, width):
        return jnp.dot(hb, wb_ref[:, lo:lo + width], preferred_element_type=F32)

    u = jnp.dot(hb, ws_ref[...], preferred_element_type=F32) + bs_ref[...]
    t = jnp.log1p(jnp.exp(-jnp.abs(u)))
    lane = lax.broadcasted_iota(jnp.int32, u.shape, 1)
    sm = jnp.where(lane < DT_LANE0, jnp.minimum(u, 0.0) - t,
                   jnp.where(lane < DT_LANE0 + SSM_HEADS, jnp.maximum(u, 0.0) + t, 0.0))
    sm_ref[...] = sm

    q_ref[...] = (proj(0, D_MODEL) * (FOX_SCALE * LOG2E)).astype(BF16)
    k = proj(D_MODEL, D_MODEL)
    kf_ref[...] = k
    if with_bias:
        @pl.when(pl.program_id(0) == 0)
        def _():
            carry_ref[...] = jnp.zeros_like(carry_ref)

        r = lax.broadcasted_iota(jnp.int32, (CUM_ROWS, CUM_ROWS), 0)
        c = lax.broadcasted_iota(jnp.int32, (CUM_ROWS, CUM_ROWS), 1)
        tri = (r >= c).astype(F32)
        run = carry_ref[0:1, :]
        parts = []
        for b in range(sm.shape[0] // CUM_ROWS):
            cs = jnp.dot(tri, sm[b * CUM_ROWS:(b + 1) * CUM_ROWS, :], precision=HIGHEST,
                         preferred_element_type=F32) + run
            run = cs[CUM_ROWS - 1:CUM_ROWS, :]
            parts.append(cs)
        carry_ref[0:1, :] = run
        nb = jnp.concatenate(parts, axis=0) * (-LOG2E)
        hi = nb.astype(BF16)
        r1 = nb - hi.astype(F32)
        mid = r1.astype(BF16)
        lo = (r1 - mid.astype(F32)).astype(BF16)
        aug = jnp.dot(jnp.concatenate([hi, mid, lo], axis=1), sel_ref[...], preferred_element_type=F32)
        for hd in range(FOX_HEADS):
            sl = slice(hd * FOX_HEAD_DIM, (hd + 1) * FOX_HEAD_DIM)
            kb_ref[:, 2 * hd * FOX_HEAD_DIM:(2 * hd + 1) * FOX_HEAD_DIM] = k[:, sl].astype(BF16)
            kb_ref[:, (2 * hd + 1) * FOX_HEAD_DIM:(2 * hd + 2) * FOX_HEAD_DIM] = aug[:, sl].astype(BF16)
    else:
        kb_ref[...] = k.astype(BF16)
    v = proj(2 * D_MODEL, D_MODEL)
    vf_ref[...] = v
    vb_ref[...] = v.astype(BF16)
    z_ref[...] = proj(3 * D_MODEL, SSM_INNER).astype(BF16)
    off = 3 * D_MODEL + SSM_INNER
    xbc_ref[...] = proj(off, CONV_DIM)
    ga_ref[...] = proj(off + CONV_DIM, D_MODEL).astype(BF16)
    gb_ref[...] = proj(off + CONV_DIM + D_MODEL, D_MODEL).astype(BF16)


def _mod_spec(arr, tm):
    if arr.shape[0] == 1:
        return pl.BlockSpec((1, D_MODEL), lambda i: (0, 0))
    return pl.BlockSpec((tm, D_MODEL), lambda i: (i, 0))


def _bias_selector():
    r = lax.broadcasted_iota(jnp.int32, (BIAS_PARTS * LANES, FOX_HEADS * FOX_HEAD_DIM), 0)
    c = lax.broadcasted_iota(jnp.int32, (BIAS_PARTS * LANES, FOX_HEADS * FOX_HEAD_DIM), 1)
    hit = (r % LANES < FOX_HEADS) & (c == (r % LANES) * FOX_HEAD_DIM + r // LANES)
    return hit.astype(BF16)


def _in_proj(x2d, g, sc, sh, w_big, w_small, b_small, tm, with_bias):
    rows = x2d.shape[0]
    assert not with_bias or tm % CUM_ROWS == 0
    row = lambda w: pl.BlockSpec((tm, w), lambda i: (i, 0))
    shp = lambda w, dt: jax.ShapeDtypeStruct((rows, w), dt)
    kb_width = 2 * D_MODEL if with_bias else D_MODEL
    in_specs = [row(D_MODEL), _resident((1, D_MODEL)), _mod_spec(sc, tm), _mod_spec(sh, tm),
                _resident((D_MODEL, BIG_WIDTH)), _resident((D_MODEL, LANES)), _resident((1, LANES))]
    args = [x2d, g, sc, sh, w_big, w_small, b_small]
    if with_bias:
        in_specs.append(_resident((BIAS_PARTS * LANES, FOX_HEADS * FOX_HEAD_DIM)))
        args.append(_bias_selector())
    return pl.pallas_call(
        functools.partial(_inproj_kernel, with_bias=with_bias),
        grid=(rows // tm,),
        in_specs=in_specs,
        out_specs=[row(D_MODEL), row(D_MODEL), row(D_MODEL), row(kb_width), row(D_MODEL),
                   row(SSM_INNER), row(CONV_DIM), row(D_MODEL), row(D_MODEL), row(LANES)],
        out_shape=[shp(D_MODEL, BF16), shp(D_MODEL, F32), shp(D_MODEL, F32), shp(kb_width, BF16),
                   shp(D_MODEL, BF16), shp(SSM_INNER, BF16), shp(CONV_DIM, F32), shp(D_MODEL, BF16),
                   shp(D_MODEL, BF16), shp(LANES, F32)],
        scratch_shapes=[pltpu.VMEM((SUBLANES, LANES), F32)] if with_bias else [],
        compiler_params=_params("arbitrary" if with_bias else "parallel"),
        name="in_proj",
    )(*args)


def _flash_kernel(q_ref, k_ref, v_ref, o_ref, s0_ref, s1_ref, p0_ref, p1_ref, m_ref, l_ref, alpha0_ref,
                  alpha1_ref, acc_ref, *, tq, tk):
    i = pl.program_id(1)
    nlt = tk // LANES
    lane = lax.broadcasted_iota(jnp.int32, (tq, FOX_HEAD_DIM), 1)
    ones = jnp.where(lane < BIAS_PARTS, 1.0, 0.0).astype(BF16)
    q2 = jnp.concatenate([q_ref[...], ones], axis=1)

    def scores(j, s_ref):
        off = pl.multiple_of(j * tk, tk)
        s_ref[...] = lax.dot_general(q2, k_ref[pl.ds(off, tk), :], _NT, preferred_element_type=F32)

    def softmax(s_ref, p_ref, alpha_ref, mask_shift):
        for rt in range(tq // ROW_TILE):
            rows = slice(rt * ROW_TILE, (rt + 1) * ROW_TILE)
            cols = [s_ref[rows, c * LANES:(c + 1) * LANES] for c in range(nlt)]
            if mask_shift is not None:
                ri = lax.broadcasted_iota(jnp.int32, (ROW_TILE, LANES), 0) + rt * ROW_TILE
                ci = lax.broadcasted_iota(jnp.int32, (ROW_TILE, LANES), 1) + mask_shift
                cols = [jnp.where(ci + c * LANES <= ri, cols[c], -jnp.inf) for c in range(nlt)]
            mx = functools.reduce(jnp.maximum, cols)
            m_old = m_ref[rows, :]
            m_new = jnp.maximum(m_old, jnp.max(mx, axis=-1, keepdims=True))
            alpha = jnp.exp2(m_old - m_new)
            ps = [jnp.exp2(cc - m_new) for cc in cols]
            l_ref[rows, :] = alpha * l_ref[rows, :] + jnp.sum(functools.reduce(jnp.add, ps), axis=-1, keepdims=True)
            m_ref[rows, :] = m_new
            alpha_ref[rows, :] = alpha
            for c in range(nlt):
                p_ref[rows, c * LANES:(c + 1) * LANES] = ps[c].astype(BF16)

    def weighted_values(j, p_ref, alpha_ref):
        off = pl.multiple_of(j * tk, tk)
        acc_ref[...] = alpha_ref[...] * acc_ref[...] + jnp.dot(p_ref[...], v_ref[pl.ds(off, tk), :],
                                                               preferred_element_type=F32)

    m_ref[...] = jnp.full(m_ref.shape, -jnp.inf, F32)
    l_ref[...] = jnp.zeros(l_ref.shape, F32)
    acc_ref[...] = jnp.zeros(acc_ref.shape, F32)
    bufs = ((s0_ref, p0_ref, alpha0_ref), (s1_ref, p1_ref, alpha1_ref))

    def run(first, n, diagonal_last, next_scores):
        for k in range(n):
            s_ref, p_ref, alpha_ref = bufs[k % 2]
            s_nxt, p_prv, alpha_prv = bufs[(k + 1) % 2]
            weighted_values(jnp.maximum(first + k - 1, 0), p_prv, alpha_prv)
            if k + 1 < n or next_scores:
                scores(first + k + 1, s_nxt)
            softmax(s_ref, p_ref, alpha_ref, 0 if (diagonal_last and k == n - 1) else None)

    p1_ref[...] = jnp.zeros(p1_ref.shape, BF16)
    alpha1_ref[...] = jnp.ones(alpha1_ref.shape, F32)
    scores(0, s0_ref)

    @pl.loop(0, i // UNROLL)
    def _(t):
        run(UNROLL * t, UNROLL, False, True)

    for rem in range(UNROLL):
        @pl.when(i % UNROLL == rem)
        def _():
            run(i - rem, rem + 1, True, False)
            weighted_values(i, *bufs[rem % 2][1:])

    o_ref[...] = (acc_ref[...] / l_ref[...]).astype(o_ref.dtype)


def _fox_prompt(q, kp, v, tq):
    L = q.shape[0]
    tk = tq
    return pl.pallas_call(
        functools.partial(_flash_kernel, tq=tq, tk=tk),
        grid=(FOX_HEADS, L // tq),
        in_specs=[pl.BlockSpec((tq, FOX_HEAD_DIM), lambda h, i: (i, h)),
                  pl.BlockSpec((L, 2 * FOX_HEAD_DIM), lambda h, i: (0, h)),
                  pl.BlockSpec((L, FOX_HEAD_DIM), lambda h, i: (0, h))],
        out_specs=pl.BlockSpec((tq, FOX_HEAD_DIM), lambda h, i: (i, h)),
        out_shape=jax.ShapeDtypeStruct((L, FOX_HEADS * FOX_HEAD_DIM), BF16),
        scratch_shapes=[pltpu.VMEM((tq, tk), F32)] * 2 + [pltpu.VMEM((tq, tk), BF16)] * 2
        + [pltpu.VMEM((tq, FOX_HEAD_DIM), F32)] * 5,
        compiler_params=_params("parallel", "arbitrary"),
        name="fox_flash",
    )(q, kp, v)


def _cached_attn_kernel(q_ref, kn_ref, vn_ref, kc_ref, vc_ref, lpt_ref, lnt_ref, o_ref):
    n = q_ref.shape[0]
    P = kc_ref.shape[0]
    r = lax.broadcasted_iota(jnp.int32, (P, P), 0)
    c = lax.broadcasted_iota(jnp.int32, (P, P), 1)
    G = jnp.dot(lpt_ref[...], (r > c).astype(F32), precision=HIGHEST, preferred_element_type=F32) * LOG2E
    rn = lax.broadcasted_iota(jnp.int32, (n, n), 0)
    cn = lax.broadcasted_iota(jnp.int32, (n, n), 1)
    Hn = jnp.dot(lnt_ref[...], (rn <= cn).astype(F32), precision=HIGHEST, preferred_element_type=F32) * LOG2E
    causal = cn <= rn
    for h in range(FOX_HEADS):
        sl = slice(h * FOX_HEAD_DIM, (h + 1) * FOX_HEAD_DIM)
        qh = q_ref[:, sl]
        sp = lax.dot_general(qh, kc_ref[:, h, :].astype(BF16), _NT, preferred_element_type=F32) + G[h:h + 1, :]
        sn = lax.dot_general(qh, kn_ref[:, sl], _NT, preferred_element_type=F32) - Hn[h:h + 1, :]
        sn = jnp.where(causal, sn, -jnp.inf)
        m = jnp.maximum(jnp.max(sp, axis=-1, keepdims=True), jnp.max(sn, axis=-1, keepdims=True))
        pp = jnp.exp2(sp - m)
        pn = jnp.exp2(sn - m)
        l = jnp.sum(pp, axis=-1, keepdims=True) + jnp.sum(pn, axis=-1, keepdims=True)
        o = (jnp.dot(pp.astype(BF16), vc_ref[:, h, :].astype(BF16), preferred_element_type=F32)
             + jnp.dot(pn.astype(BF16), vn_ref[:, sl], preferred_element_type=F32))
        o_ref[:, sl] = (o / l).astype(o_ref.dtype)


def _fox_cached(q, kn, vn, kc, vc, lpt, lnt, n):
    B, P = kc.shape[:2]
    W = FOX_HEADS * FOX_HEAD_DIM
    row = pl.BlockSpec((n, W), lambda b: (b, 0))
    cache = pl.BlockSpec((None, P, FOX_HEADS, FOX_HEAD_DIM), lambda b: (b, 0, 0, 0))
    return pl.pallas_call(
        _cached_attn_kernel,
        grid=(B,),
        in_specs=[row, row, row, cache, cache,
                  pl.BlockSpec((None, FOX_HEADS, P), lambda b: (b, 0, 0)),
                  pl.BlockSpec((None, FOX_HEADS, n), lambda b: (b, 0, 0))],
        out_specs=row,
        out_shape=jax.ShapeDtypeStruct((B * n, W), BF16),
        compiler_params=_params("parallel"),
        name="fox_cached",
    )(q, kn, vn, kc, vc, lpt, lnt)


def _ssd_kernel(xbc_ref, z_ref, sm_ref, convp_ref, ssmp_ref, cw_ref, cb_ref, alog_ref, dexp_ref, ng_ref,
                expand_ref, y_ref, convn_ref, ssmn_ref, tail_ref, st_ref, *, Q, nchunks):
    ci = pl.program_id(1)
    PAIR = 2 * SSM_HEAD_DIM
    GW = SSM_STATE

    @pl.when(ci == 0)
    def _():
        st_ref[...] = ssmp_ref[...]
        tail_ref[...] = jnp.zeros((SUBLANES, CONV_DIM), F32)
        tail_ref[SUBLANES - (CONV_WIDTH - 1):SUBLANES, :] = convp_ref[...]

    x3 = xbc_ref[...].reshape(Q // SUBLANES, SUBLANES, CONV_DIM)
    tail = tail_ref[...].reshape(1, SUBLANES, CONV_DIM)
    sub = lax.broadcasted_iota(jnp.int32, x3.shape, 1)
    conv = cb_ref[...]
    for d in range(CONV_WIDTH - 1, 0, -1):
        rot = pltpu.roll(x3, d, 1)
        prev = jnp.concatenate([pltpu.roll(tail, d, 1), rot[:-1]], axis=0)
        conv = conv + jnp.where(sub < d, prev, rot) * cw_ref[CONV_WIDTH - 1 - d:CONV_WIDTH - d, :]
    conv = (conv + x3 * cw_ref[CONV_WIDTH - 1:CONV_WIDTH, :]).reshape(Q, CONV_DIM)
    tail_ref[...] = xbc_ref[Q - SUBLANES:Q, :]
    convn_ref[...] = xbc_ref[Q - (CONV_WIDTH - 1):Q, :]

    xc = _silu(conv)
    xs = xc[:, :SSM_INNER]
    Bm = xc[:, SSM_INNER:SSM_INNER + SSM_GROUPS * GW]
    Cm = xc[:, SSM_INNER + SSM_GROUPS * GW:]

    lane1 = lax.broadcasted_iota(jnp.int32, (Q, LANES), 1)
    head_lanes = (lane1 >= DT_LANE0) & (lane1 < DT_LANE0 + SSM_HEADS)
    dt = jnp.where(head_lanes, sm_ref[...], 0.0)
    a = dt * (-jnp.exp(alog_ref[...]))
    r = lax.broadcasted_iota(jnp.int32, (Q, Q), 0)
    c = lax.broadcasted_iota(jnp.int32, (Q, Q), 1)
    causal = r >= c
    a_cs = jnp.dot(causal.astype(F32), a, precision=HIGHEST, preferred_element_type=F32)
    er = lax.broadcasted_iota(jnp.int32, (SSM_HEADS, LANES), 0)
    ec = lax.broadcasted_iota(jnp.int32, (SSM_HEADS, LANES), 1)
    pick = (ec == er + DT_LANE0).astype(F32)
    acs_t = lax.dot_general(pick, a_cs, _NT, precision=HIGHEST, preferred_element_type=F32)
    a_last = a_cs[Q - 1:Q, :]
    e_last = jnp.exp(a_last)

    def per_channel(x):
        x = jnp.where(head_lanes, x, 0.0)
        hi = x.astype(BF16).astype(F32)
        r1 = x - hi
        mid = r1.astype(BF16).astype(F32)
        packed = hi + pltpu.roll(mid, PIECE_LANES, 1) + pltpu.roll(r1 - mid, 2 * PIECE_LANES, 1)
        return jnp.dot(packed.astype(BF16), expand_ref[...], preferred_element_type=F32)

    xd = xs * per_channel(dt)
    xdb = xd.astype(BF16)
    xw = xd * per_channel(jnp.exp(a_last - a_cs))
    e_cs = per_channel(jnp.exp(a_cs))
    Bb = Bm.astype(BF16)
    Cb = Cm.astype(BF16)

    lane_lo = lax.broadcasted_iota(jnp.int32, (Q, PAIR), 1) < SSM_HEAD_DIM
    row_lo = lax.broadcasted_iota(jnp.int32, (PAIR, GW), 0) < SSM_HEAD_DIM

    y_parts = []
    for g in range(SSM_GROUPS):
        Bg = Bb[:, g * GW:(g + 1) * GW]
        Cg = Cb[:, g * GW:(g + 1) * GW]
        cbm = lax.dot_general(Cg, Bg, _NT, preferred_element_type=F32)
        for pr in range(SSM_HEADS // SSM_GROUPS // 2):
            pair = g * (SSM_HEADS // SSM_GROUPS // 2) + pr
            sl = slice(pair * PAIR, (pair + 1) * PAIR)
            s_prev = st_ref[sl, :]
            ys = []
            for hh in (2 * pair, 2 * pair + 1):
                ln = DT_LANE0 + hh
                lm = jnp.exp(jnp.where(causal, a_cs[:, ln:ln + 1] - acs_t[hh:hh + 1, :], -jnp.inf))
                ys.append(jnp.dot((cbm * lm).astype(BF16), xdb[:, sl], preferred_element_type=F32))
            y_off = lax.dot_general(Cg, s_prev.astype(BF16), _NT, preferred_element_type=F32) * e_cs[:, sl]
            ln0 = DT_LANE0 + 2 * pair
            dec = jnp.where(row_lo, e_last[:, ln0:ln0 + 1], e_last[:, ln0 + 1:ln0 + 2])
            st_ref[sl, :] = s_prev * dec + lax.dot_general(xw[:, sl], Bm[:, g * GW:(g + 1) * GW], _TN,
                                                           preferred_element_type=F32)
            y_parts.append(jnp.where(lane_lo, ys[0], ys[1]) + y_off + xs[:, sl] * dexp_ref[:, sl])

    y = jnp.concatenate(y_parts, axis=1) * _silu(z_ref[...].astype(F32))
    gw = SSM_INNER // SSM_GROUPS
    normed = []
    for g in range(SSM_GROUPS):
        yg = y[:, g * gw:(g + 1) * gw]
        normed.append(yg * lax.rsqrt(jnp.mean(yg * yg, axis=-1, keepdims=True) + EPS))
    y_ref[...] = (jnp.concatenate(normed, axis=1) * ng_ref[...]).astype(y_ref.dtype)

    @pl.when(ci == nchunks - 1)
    def _():
        ssmn_ref[...] = st_ref[...]


def _head_expander():
    k = lax.broadcasted_iota(jnp.int32, (LANES, SSM_INNER), 0)
    j = lax.broadcasted_iota(jnp.int32, (LANES, SSM_INNER), 1)
    hit = (k < BIAS_PARTS * PIECE_LANES) & (k % PIECE_LANES == DT_LANE0 + j // SSM_HEAD_DIM)
    return hit.astype(BF16)


def _ssd(xbc, z, small, conv_past, ssm_past, conv_w, conv_b, alog_pad, d_exp, norm_g, batch, Q):
    rows = xbc.shape[0]
    nchunks = rows // batch // Q
    assert Q % SUBLANES == 0 and Q >= SUBLANES and nchunks * Q * batch == rows
    row = lambda w: pl.BlockSpec((Q, w), lambda b, c: (b * nchunks + c, 0))
    state_rows = SSM_HEADS * SSM_HEAD_DIM
    per_seq = lambda a, b_: pl.BlockSpec((None, a, b_), lambda b, c: (b, 0, 0))
    return pl.pallas_call(
        functools.partial(_ssd_kernel, Q=Q, nchunks=nchunks),
        grid=(batch, nchunks),
        in_specs=[row(CONV_DIM), row(SSM_INNER), row(LANES),
                  per_seq(CONV_WIDTH - 1, CONV_DIM), per_seq(state_rows, SSM_STATE),
                  _resident((CONV_WIDTH, CONV_DIM)), _resident((1, CONV_DIM)), _resident((1, LANES)),
                  _resident((1, SSM_INNER)), _resident((1, SSM_INNER)), _resident((LANES, SSM_INNER))],
        out_specs=[row(SSM_INNER), per_seq(CONV_WIDTH - 1, CONV_DIM), per_seq(state_rows, SSM_STATE)],
        out_shape=[jax.ShapeDtypeStruct((rows, SSM_INNER), BF16),
                   jax.ShapeDtypeStruct((batch, CONV_WIDTH - 1, CONV_DIM), F32),
                   jax.ShapeDtypeStruct((batch, state_rows, SSM_STATE), F32)],
        scratch_shapes=[pltpu.VMEM((SUBLANES, CONV_DIM), F32), pltpu.VMEM((state_rows, SSM_STATE), F32)],
        compiler_params=_params("parallel", "arbitrary"),
        name="ssd_scan",
    )(xbc, z, small, conv_past, ssm_past, conv_w, conv_b, alog_pad, d_exp, norm_g, _head_expander())


def _outffn_kernel(ya_ref, ys_ref, ga_ref, gb_ref, x_ref, gt1_ref, sc2_ref, sh2_ref, gt2_ref,
                   gpm_ref, gpf_ref, gqf_ref, wo_ref, wu_ref, wd_ref, o_ref):
    merged = (_sigmoid(ga_ref[...].astype(F32)) * ya_ref[...].astype(F32)
              + _sigmoid(gb_ref[...].astype(F32)) * ys_ref[...].astype(F32))
    m = jnp.dot(merged.astype(BF16), wo_ref[...], preferred_element_type=F32)
    x1 = x_ref[...] + gt1_ref[...] * _rms(m, gpm_ref[...])
    hb = (_rms(x1, gpf_ref[...]) * (1.0 + sc2_ref[...]) + sh2_ref[...]).astype(BF16)
    f = jnp.zeros(x1.shape, F32)
    for cc in range(D_FF // D_MODEL):
        sl = slice(cc * D_MODEL, (cc + 1) * D_MODEL)
        up = jnp.dot(hb, wu_ref[:, sl], preferred_element_type=F32)
        act = jnp.square(jnp.maximum(up, 0.0)).astype(BF16)
        f = f + jnp.dot(act, wd_ref[sl, :], preferred_element_type=F32)
    o_ref[...] = x1 + gt2_ref[...] * _rms(f, gqf_ref[...])


def _out_ffn(ya, ys, ga, gb, x2d, gt1, sc2, sh2, gt2, g_post_mix, g_pre_ffn, g_post_ffn, w_out, w_up, w_down, tm):
    rows = x2d.shape[0]
    row = pl.BlockSpec((tm, D_MODEL), lambda i: (i, 0))
    vec = _resident((1, D_MODEL))
    return pl.pallas_call(
        _outffn_kernel,
        grid=(rows // tm,),
        in_specs=[row, row, row, row, row,
                  _mod_spec(gt1, tm), _mod_spec(sc2, tm), _mod_spec(sh2, tm), _mod_spec(gt2, tm),
                  vec, vec, vec,
                  _resident((D_MODEL, D_MODEL)), _resident((D_MODEL, D_FF)), _resident((D_FF, D_MODEL))],
        out_specs=row,
        out_shape=jax.ShapeDtypeStruct((rows, D_MODEL), F32),
        compiler_params=_params("parallel"),
        name="out_ffn",
    )(ya, ys, ga, gb, x2d, gt1, sc2, sh2, gt2, g_post_mix, g_pre_ffn, g_post_ffn, w_out, w_up, w_down)


def _layer(x, mod, past, wts, *, tm, tq, Q):
    b, L, _ = x.shape
    x2d = x.reshape(b * L, D_MODEL)
    sh1, sc1, gt1, sh2, sc2, gt2 = mod
    q, kf, vf, kb, vb, z, xbc, ga, gb, small = _in_proj(
        x2d, wts["g_pre_mix"], sc1, sh1, wts["w_big"], wts["w_small"], wts["b_small"], tm, past is None)

    if past is None:
        assert b == 1
        y_att = _fox_prompt(q, kb, vb, tq)
        conv_past = jnp.zeros((b, CONV_WIDTH - 1, CONV_DIM), F32)
        ssm_past = jnp.zeros((b, SSM_INNER, SSM_STATE), F32)
    else:
        k_past, v_past, logf_past, conv_past, ssm_past = past
        P = k_past.shape[1]
        lpt = jnp.swapaxes(logf_past, 1, 2)
        lnt = jnp.swapaxes(small[:, :FOX_HEADS].reshape(b, L, FOX_HEADS), 1, 2)
        y_att = _fox_cached(q, kb, vb, k_past, v_past, lpt, lnt, L)
        ssm_past = ssm_past.reshape(b, SSM_INNER, SSM_STATE)

    y_ssm, conv_new, ssm_new = _ssd(xbc, z, small, conv_past, ssm_past, wts["conv_w"], wts["conv_b"],
                                    wts["alog_pad"], wts["d_exp"], wts["ssm_norm_g"], b, Q)
    y = _out_ffn(y_att, y_ssm, ga, gb, x2d, gt1, sc2, sh2, gt2, wts["g_post_mix"], wts["g_pre_ffn"],
                 wts["g_post_ffn"], wts["w_out"], wts["w_up"], wts["w_down"], tm)
    return (y.reshape(b, L, D_MODEL),
            kf.reshape(b, L, FOX_HEADS, FOX_HEAD_DIM), vf.reshape(b, L, FOX_HEADS, FOX_HEAD_DIM),
            small[:, :FOX_HEADS].reshape(b, L, FOX_HEADS), conv_new,
            ssm_new.reshape(b, SSM_HEADS, SSM_HEAD_DIM, SSM_STATE))


def _prep_weights(w_ada, b_ada, g_pre_mix, g_post_mix, g_pre_ffn, g_post_ffn, w_in, b_f, conv_w, conv_b,
                  dt_bias, a_log, d_skip, ssm_norm_g, w_out, w_up, w_down):
    sizes = (D_MODEL, D_MODEL, D_MODEL, FOX_HEADS, SSM_INNER, CONV_DIM, SSM_HEADS, D_MODEL, D_MODEL)
    offs = [0]
    for s in sizes:
        offs.append(offs[-1] + s)
    piece = lambda i: w_in[:, offs[i]:offs[i + 1]]
    pad = LANES - FOX_HEADS - SSM_HEADS
    row = lambda v: v.reshape(1, -1).astype(F32)
    return {
        "w_big": jnp.concatenate([piece(0), piece(1), piece(2), piece(4), piece(5), piece(7), piece(8)],
                                 axis=1).astype(BF16),
        "w_small": jnp.pad(jnp.concatenate([piece(3), piece(6)], axis=1), ((0, 0), (0, pad))).astype(BF16),
        "b_small": jnp.pad(jnp.concatenate([b_f, dt_bias]), (0, pad)).reshape(1, LANES).astype(F32),
        "alog_pad": jnp.pad(a_log, (DT_LANE0, LANES - DT_LANE0 - SSM_HEADS)).reshape(1, LANES).astype(F32),
        "d_exp": jnp.repeat(d_skip, SSM_HEAD_DIM).reshape(1, SSM_INNER).astype(F32),
        "g_pre_mix": row(g_pre_mix), "g_post_mix": row(g_post_mix),
        "g_pre_ffn": row(g_pre_ffn), "g_post_ffn": row(g_post_ffn),
        "conv_w": conv_w.astype(F32), "conv_b": row(conv_b), "ssm_norm_g": row(ssm_norm_g),
        "w_out": w_out.astype(BF16), "w_up": w_up.astype(BF16), "w_down": w_down.astype(BF16),
    }


def _forward(x_prompt, x_sample, c_prompt, c_sample, cache_fox_k, cache_fox_v, cache_fox_logf,
             state_ssm_conv, state_ssm, w_ada, b_ada, *layer_w, tm_prompt, tq, q_prompt):
    depth = w_ada.shape[0]
    bp, Lp, _ = x_prompt.shape
    bs, Ls, _ = x_sample.shape
    yp, ys = x_prompt, x_sample
    outs_p, outs_s = [], []
    for i in range(depth):
        wts = _prep_weights(w_ada[i], b_ada[i], *[w[i] for w in layer_w])
        mod = _ada(jnp.concatenate([c_prompt, c_sample], axis=0), w_ada[i], b_ada[i].reshape(1, -1))
        mod_p = [m for m in jnp.split(mod[:bp], N_MOD, axis=-1)]
        mod_s = [jnp.repeat(m, Ls, axis=0) for m in jnp.split(mod[bp:], N_MOD, axis=-1)]
        rp = _layer(yp, mod_p, None, wts, tm=tm_prompt, tq=tq, Q=q_prompt)
        rs = _layer(ys, mod_s, (cache_fox_k[i], cache_fox_v[i], cache_fox_logf[i], state_ssm_conv[i], state_ssm[i]),
                    wts, tm=bs * Ls, tq=None, Q=Ls)
        yp, ys = rp[0], rs[0]
        outs_p.append(rp[1:])
        outs_s.append(rs[1:])
    stack = lambda outs, j: jnp.stack([o[j] for o in outs])
    return (yp, ys) + tuple(stack(outs_p, j) for j in range(5)) + tuple(stack(outs_s, j) for j in range(5))


def kernel(x_prompt, x_sample, c_prompt, c_sample, cache_fox_k, cache_fox_v, cache_fox_logf, state_ssm_conv,
           state_ssm, w_ada, b_ada, g_pre_mix, g_post_mix, g_pre_ffn, g_post_ffn, w_in, b_f, conv_w, conv_b,
           dt_bias, a_log, d_skip, ssm_norm_g, w_out, w_up, w_down):
    assert x_prompt.shape[0] == 1, "the prompt path carries one sequence"
    L = x_prompt.shape[1]
    return _forward(x_prompt, x_sample, c_prompt, c_sample, cache_fox_k, cache_fox_v, cache_fox_logf,
                    state_ssm_conv, state_ssm, w_ada, b_ada, g_pre_mix, g_post_mix, g_pre_ffn, g_post_ffn,
                    w_in, b_f, conv_w, conv_b, dt_bias, a_log, d_skip, ssm_norm_g, w_out, w_up, w_down,
                    tm_prompt=min(512, L), tq=min(512, L), q_prompt=min(256, L))
```

```python
import functools

import jax
import jax.numpy as jnp
from jax import lax
from jax.experimental import pallas as pl
from jax.experimental.pallas import tpu as pltpu

F32 = jnp.float32
BF16 = jnp.bfloat16
HIGHEST = lax.Precision.HIGHEST

D_MODEL = 1024
FOX_HEADS = 8
FOX_HEAD_DIM = 128
FOX_SCALE = FOX_HEAD_DIM ** -0.5
SSM_HEADS = 16
SSM_HEAD_DIM = 64
SSM_GROUPS = 4
SSM_STATE = 128
SSM_INNER = SSM_HEADS * SSM_HEAD_DIM
CONV_WIDTH = 4
CONV_DIM = SSM_INNER + 2 * SSM_GROUPS * SSM_STATE
D_FF = 4 * D_MODEL
N_MOD = 6
EPS = 1e-6

LANES = 128
SUBLANES = 8
DT_LANE0 = FOX_HEADS
VMEM_LIMIT = 56 * 1024 * 1024
LOG2E = 1.4426950408889634
BIAS_PARTS = 3
PIECE_LANES = 32
CUM_ROWS = 128
ROW_TILE = 16
UNROLL = 4
SKIP_MARGIN = 160.0

BIG_WIDTH = 3 * D_MODEL + SSM_INNER + CONV_DIM + 2 * D_MODEL

_NT = (((1,), (1,)), ((), ()))
_TN = (((0,), (0,)), ((), ()))


def _params(*sem):
    return pltpu.CompilerParams(dimension_semantics=sem, vmem_limit_bytes=VMEM_LIMIT)


def _resident(shape):
    zeros = (0,) * len(shape)
    return pl.BlockSpec(shape, lambda *_: zeros, pipeline_mode=pl.Buffered(1))


def _rms(x, g):
    return x * lax.rsqrt(jnp.mean(x * x, axis=-1, keepdims=True) + EPS) * g


def _sigmoid(x):
    return 0.5 * jnp.tanh(0.5 * x) + 0.5


def _silu(x):
    h = 0.5 * x
    return h * jnp.tanh(h) + h


def _ada_kernel(c_ref, w_ref, b_ref, o_ref):
    s = _silu(c_ref[...]).astype(BF16)
    o_ref[...] = jnp.dot(s, w_ref[...].astype(BF16), preferred_element_type=F32) + b_ref[...]


def _ada(c_all, w_ada, b_ada):
    rows = c_all.shape[0]
    width = w_ada.shape[1]
    tn = 1536
    return pl.pallas_call(
        _ada_kernel,
        grid=(width // tn,),
        in_specs=[pl.BlockSpec((rows, D_MODEL), lambda j: (0, 0)),
                  pl.BlockSpec((D_MODEL, tn), lambda j: (0, j)),
                  pl.BlockSpec((1, tn), lambda j: (0, j))],
        out_specs=pl.BlockSpec((rows, tn), lambda j: (0, j)),
        out_shape=jax.ShapeDtypeStruct((rows, width), F32),
        compiler_params=_params("parallel"),
        name="ada_mod",
    )(c_all, w_ada, b_ada)


def _inproj_kernel(x_ref, g_ref, sc_ref, sh_ref, wb_ref, ws_ref, bs_ref, *rest, with_bias):
    if with_bias:
        (sel_ref, q_ref, kf_ref, vf_ref, kb_ref, vb_ref, z_ref, xbc_ref, ga_ref, gb_ref, sm_ref, stats_ref,
         carry_ref) = rest
    else:
        q_ref, kf_ref, vf_ref, kb_ref, vb_ref, z_ref, xbc_ref, ga_ref, gb_ref, sm_ref = rest
    h = _rms(x_ref[...], g_ref[...]) * (1.0 + sc_ref[...]) + sh_ref[...]
    hb = h.astype(BF16)

    def proj(lo, width):
        return jnp.dot(hb, wb_ref[:, lo:lo + width], preferred_element_type=F32)

    u = jnp.dot(hb, ws_ref[...], preferred_element_type=F32) + bs_ref[...]
    t = jnp.log1p(jnp.exp(-jnp.abs(u)))
    lane = lax.broadcasted_iota(jnp.int32, u.shape, 1)
    sm = jnp.where(lane < DT_LANE0, jnp.minimum(u, 0.0) - t,
                   jnp.where(lane < DT_LANE0 + SSM_HEADS, jnp.maximum(u, 0.0) + t, 0.0))
    sm_ref[...] = sm

    qb = (proj(0, D_MODEL) * (FOX_SCALE * LOG2E)).astype(BF16)
    q_ref[...] = qb
    k = proj(D_MODEL, D_MODEL)
    kf_ref[...] = k
    if with_bias:
        @pl.when(pl.program_id(0) == 0)
        def _():
            carry_ref[...] = jnp.zeros_like(carry_ref)

        r = lax.broadcasted_iota(jnp.int32, (CUM_ROWS, CUM_ROWS), 0)
        c = lax.broadcasted_iota(jnp.int32, (CUM_ROWS, CUM_ROWS), 1)
        tri = (r >= c).astype(F32)
        run = carry_ref[0:1, :]
        parts = []
        for b in range(sm.shape[0] // CUM_ROWS):
            cs = jnp.dot(tri, sm[b * CUM_ROWS:(b + 1) * CUM_ROWS, :], precision=HIGHEST,
                         preferred_element_type=F32) + run
            run = cs[CUM_ROWS - 1:CUM_ROWS, :]
            parts.append(cs)
        carry_ref[0:1, :] = run
        nb = jnp.concatenate(parts, axis=0) * (-LOG2E)
        hi = nb.astype(BF16)
        r1 = nb - hi.astype(F32)
        mid = r1.astype(BF16)
        lo = (r1 - mid.astype(F32)).astype(BF16)
        aug = jnp.dot(jnp.concatenate([hi, mid, lo], axis=1), sel_ref[...], preferred_element_type=F32)
        kb = k.astype(BF16)
        for hd in range(FOX_HEADS):
            sl = slice(hd * FOX_HEAD_DIM, (hd + 1) * FOX_HEAD_DIM)
            kb_ref[:, 2 * hd * FOX_HEAD_DIM:(2 * hd + 1) * FOX_HEAD_DIM] = kb[:, sl]
            kb_ref[:, (2 * hd + 1) * FOX_HEAD_DIM:(2 * hd + 2) * FOX_HEAD_DIM] = aug[:, sl].astype(BF16)

        def max_sq_norms(xb):
            x2 = xb.astype(F32)
            x2 = x2 * x2
            out = jnp.zeros((1, LANES), F32)
            for hd in range(FOX_HEADS):
                n2 = jnp.sum(x2[:, hd * FOX_HEAD_DIM:(hd + 1) * FOX_HEAD_DIM], axis=-1, keepdims=True)
                out = jnp.where(lane[0:1, :] == hd, jnp.max(n2, axis=0, keepdims=True), out)
            return out

        stats_ref[...] = jnp.zeros(stats_ref.shape, F32)
        stats_ref[0:1, :] = max_sq_norms(qb)
        stats_ref[1:2, :] = max_sq_norms(kb)
        stats_ref[2:3, :] = nb[0:1, :]
        stats_ref[3:4, :] = nb[nb.shape[0] - 1:, :]
    else:
        kb_ref[...] = k.astype(BF16)
    v = proj(2 * D_MODEL, D_MODEL)
    vf_ref[...] = v
    vb_ref[...] = v.astype(BF16)
    z_ref[...] = proj(3 * D_MODEL, SSM_INNER).astype(BF16)
    off = 3 * D_MODEL + SSM_INNER
    xbc_ref[...] = proj(off, CONV_DIM)
    ga_ref[...] = proj(off + CONV_DIM, D_MODEL).astype(BF16)
    gb_ref[...] = proj(off + CONV_DIM + D_MODEL, D_MODEL).astype(BF16)


def _mod_spec(arr, tm):
    if arr.shape[0] == 1:
        return pl.BlockSpec((1, D_MODEL), lambda i: (0, 0))
    return pl.BlockSpec((tm, D_MODEL), lambda i: (i, 0))


def _bias_selector():
    r = lax.broadcasted_iota(jnp.int32, (BIAS_PARTS * LANES, FOX_HEADS * FOX_HEAD_DIM), 0)
    c = lax.broadcasted_iota(jnp.int32, (BIAS_PARTS * LANES, FOX_HEADS * FOX_HEAD_DIM), 1)
    hit = (r % LANES < FOX_HEADS) & (c == (r % LANES) * FOX_HEAD_DIM + r // LANES)
    return hit.astype(BF16)


def _in_proj(x2d, g, sc, sh, w_big, w_small, b_small, tm, with_bias):
    rows = x2d.shape[0]
    assert not with_bias or tm % CUM_ROWS == 0
    row = lambda w: pl.BlockSpec((tm, w), lambda i: (i, 0))
    shp = lambda w, dt: jax.ShapeDtypeStruct((rows, w), dt)
    kb_width = 2 * D_MODEL if with_bias else D_MODEL
    in_specs = [row(D_MODEL), _resident((1, D_MODEL)), _mod_spec(sc, tm), _mod_spec(sh, tm),
                _resident((D_MODEL, BIG_WIDTH)), _resident((D_MODEL, LANES)), _resident((1, LANES))]
    args = [x2d, g, sc, sh, w_big, w_small, b_small]
    out_specs = [row(D_MODEL), row(D_MODEL), row(D_MODEL), row(kb_width), row(D_MODEL),
                 row(SSM_INNER), row(CONV_DIM), row(D_MODEL), row(D_MODEL), row(LANES)]
    out_shape = [shp(D_MODEL, BF16), shp(D_MODEL, F32), shp(D_MODEL, F32), shp(kb_width, BF16),
                 shp(D_MODEL, BF16), shp(SSM_INNER, BF16), shp(CONV_DIM, F32), shp(D_MODEL, BF16),
                 shp(D_MODEL, BF16), shp(LANES, F32)]
    if with_bias:
        in_specs.append(_resident((BIAS_PARTS * LANES, FOX_HEADS * FOX_HEAD_DIM)))
        args.append(_bias_selector())
        out_specs.append(pl.BlockSpec((SUBLANES, LANES), lambda i: (i, 0)))
        out_shape.append(jax.ShapeDtypeStruct((rows // tm * SUBLANES, LANES), F32))
    return pl.pallas_call(
        functools.partial(_inproj_kernel, with_bias=with_bias),
        grid=(rows // tm,),
        in_specs=in_specs,
        out_specs=out_specs,
        out_shape=out_shape,
        scratch_shapes=[pltpu.VMEM((SUBLANES, LANES), F32)] if with_bias else [],
        compiler_params=_params("arbitrary" if with_bias else "parallel"),
        name="in_proj",
    )(*args)


def _flash_kernel(first_ref, q_ref, k_ref, v_ref, o_ref, s0_ref, s1_ref, p0_ref, p1_ref, m_ref, l_ref, alpha0_ref,
                  alpha1_ref, acc_ref, *, tq, tk):
    i = pl.program_id(1)
    nlt = tk // LANES
    lane = lax.broadcasted_iota(jnp.int32, (tq, FOX_HEAD_DIM), 1)
    ones = jnp.where(lane < BIAS_PARTS, 1.0, 0.0).astype(BF16)
    q2 = jnp.concatenate([q_ref[...], ones], axis=1)

    def scores(j, s_ref):
        off = pl.multiple_of(j * tk, tk)
        s_ref[...] = lax.dot_general(q2, k_ref[pl.ds(off, tk), :], _NT, preferred_element_type=F32)

    def softmax(s_ref, p_ref, alpha_ref, mask_shift):
        for rt in range(tq // ROW_TILE):
            rows = slice(rt * ROW_TILE, (rt + 1) * ROW_TILE)
            cols = [s_ref[rows, c * LANES:(c + 1) * LANES] for c in range(nlt)]
            if mask_shift is not None:
                ri = lax.broadcasted_iota(jnp.int32, (ROW_TILE, LANES), 0) + rt * ROW_TILE
                ci = lax.broadcasted_iota(jnp.int32, (ROW_TILE, LANES), 1) + mask_shift
                cols = [jnp.where(ci + c * LANES <= ri, cols[c], -jnp.inf) for c in range(nlt)]
            mx = functools.reduce(jnp.maximum, cols)
            m_old = m_ref[rows, :]
            m_new = jnp.maximum(m_old, jnp.max(mx, axis=-1, keepdims=True))
            alpha = jnp.exp2(m_old - m_new)
            ps = [jnp.exp2(cc - m_new) for cc in cols]
            l_ref[rows, :] = alpha * l_ref[rows, :] + jnp.sum(functools.reduce(jnp.add, ps), axis=-1, keepdims=True)
            m_ref[rows, :] = m_new
            alpha_ref[rows, :] = alpha
            for c in range(nlt):
                p_ref[rows, c * LANES:(c + 1) * LANES] = ps[c].astype(BF16)

    def weighted_values(j, p_ref, alpha_ref):
        off = pl.multiple_of(j * tk, tk)
        acc_ref[...] = alpha_ref[...] * acc_ref[...] + jnp.dot(p_ref[...], v_ref[pl.ds(off, tk), :],
                                                               preferred_element_type=F32)

    m_ref[...] = jnp.full(m_ref.shape, -jnp.inf, F32)
    l_ref[...] = jnp.zeros(l_ref.shape, F32)
    acc_ref[...] = jnp.zeros(acc_ref.shape, F32)
    bufs = ((s0_ref, p0_ref, alpha0_ref), (s1_ref, p1_ref, alpha1_ref))

    def run(first, n, diagonal_last, next_scores):
        for k in range(n):
            s_ref, p_ref, alpha_ref = bufs[k % 2]
            s_nxt, p_prv, alpha_prv = bufs[(k + 1) % 2]
            weighted_values(jnp.maximum(first + k - 1, 0), p_prv, alpha_prv)
            if k + 1 < n or next_scores:
                scores(first + k + 1, s_nxt)
            softmax(s_ref, p_ref, alpha_ref, 0 if (diagonal_last and k == n - 1) else None)

    p1_ref[...] = jnp.zeros(p1_ref.shape, BF16)
    alpha1_ref[...] = jnp.ones(alpha1_ref.shape, F32)
    j0 = first_ref[pl.program_id(0), i]
    n_full = i - j0
    scores(j0, s0_ref)

    @pl.loop(0, n_full // UNROLL)
    def _(t):
        run(j0 + UNROLL * t, UNROLL, False, True)

    for rem in range(UNROLL):
        @pl.when(n_full % UNROLL == rem)
        def _():
            run(i - rem, rem + 1, True, False)
            weighted_values(i, *bufs[rem % 2][1:])

    o_ref[...] = (acc_ref[...] / l_ref[...]).astype(o_ref.dtype)


def _first_blocks(stats, nq):
    st = stats.reshape(nq, SUBLANES, LANES)
    qn = jnp.sqrt(st[:, 0, :FOX_HEADS])
    kn = jnp.sqrt(st[:, 1, :FOX_HEADS])
    b_first, b_last = st[:, 2, :FOX_HEADS], st[:, 3, :FOX_HEADS]
    bound = qn[:, None, :] * (kn[None, :, :] + kn[:, None, :]) + b_last[None, :, :] - b_first[:, None, :]
    leading = jnp.cumprod((bound <= -SKIP_MARGIN).astype(jnp.int32), axis=1).sum(axis=1)
    return jnp.minimum(leading, jnp.arange(nq, dtype=jnp.int32)[:, None]).T


def _fox_prompt(q, kp, v, first, tq):
    L = q.shape[0]
    tk = tq
    return pl.pallas_call(
        functools.partial(_flash_kernel, tq=tq, tk=tk),
        grid=(FOX_HEADS, L // tq),
        in_specs=[pl.BlockSpec(memory_space=pltpu.SMEM),
                  pl.BlockSpec((tq, FOX_HEAD_DIM), lambda h, i: (i, h)),
                  pl.BlockSpec((L, 2 * FOX_HEAD_DIM), lambda h, i: (0, h)),
                  pl.BlockSpec((L, FOX_HEAD_DIM), lambda h, i: (0, h))],
        out_specs=pl.BlockSpec((tq, FOX_HEAD_DIM), lambda h, i: (i, h)),
        out_shape=jax.ShapeDtypeStruct((L, FOX_HEADS * FOX_HEAD_DIM), BF16),
        scratch_shapes=[pltpu.VMEM((tq, tk), F32)] * 2 + [pltpu.VMEM((tq, tk), BF16)] * 2
        + [pltpu.VMEM((tq, FOX_HEAD_DIM), F32)] * 5,
        compiler_params=_params("parallel", "arbitrary"),
        name="fox_flash",
    )(first, q, kp, v)


def _cached_attn_kernel(q_ref, kn_ref, vn_ref, kc_ref, vc_ref, lpt_ref, lnt_ref, o_ref):
    n = q_ref.shape[0]
    P = kc_ref.shape[0]
    r = lax.broadcasted_iota(jnp.int32, (P, P), 0)
    c = lax.broadcasted_iota(jnp.int32, (P, P), 1)
    G = jnp.dot(lpt_ref[...], (r > c).astype(F32), precision=HIGHEST, preferred_element_type=F32) * LOG2E
    rn = lax.broadcasted_iota(jnp.int32, (n, n), 0)
    cn = lax.broadcasted_iota(jnp.int32, (n, n), 1)
    Hn = jnp.dot(lnt_ref[...], (rn <= cn).astype(F32), precision=HIGHEST, preferred_element_type=F32) * LOG2E
    causal = cn <= rn
    for h in range(FOX_HEADS):
        sl = slice(h * FOX_HEAD_DIM, (h + 1) * FOX_HEAD_DIM)
        qh = q_ref[:, sl]
        sp = lax.dot_general(qh, kc_ref[:, h, :].astype(BF16), _NT, preferred_element_type=F32) + G[h:h + 1, :]
        sn = lax.dot_general(qh, kn_ref[:, sl], _NT, preferred_element_type=F32) - Hn[h:h + 1, :]
        sn = jnp.where(causal, sn, -jnp.inf)
        m = jnp.maximum(jnp.max(sp, axis=-1, keepdims=True), jnp.max(sn, axis=-1, keepdims=True))
        pp = jnp.exp2(sp - m)
        pn = jnp.exp2(sn - m)
        l = jnp.sum(pp, axis=-1, keepdims=True) + jnp.sum(pn, axis=-1, keepdims=True)
        o = (jnp.dot(pp.astype(BF16), vc_ref[:, h, :].astype(BF16), preferred_element_type=F32)
             + jnp.dot(pn.astype(BF16), vn_ref[:, sl], preferred_element_type=F32))
        o_ref[:, sl] = (o / l).astype(o_ref.dtype)


def _fox_cached(q, kn, vn, kc, vc, lpt, lnt, n):
    B, P = kc.shape[:2]
    W = FOX_HEADS * FOX_HEAD_DIM
    row = pl.BlockSpec((n, W), lambda b: (b, 0))
    cache = pl.BlockSpec((None, P, FOX_HEADS, FOX_HEAD_DIM), lambda b: (b, 0, 0, 0))
    return pl.pallas_call(
        _cached_attn_kernel,
        grid=(B,),
        in_specs=[row, row, row, cache, cache,
                  pl.BlockSpec((None, FOX_HEADS, P), lambda b: (b, 0, 0)),
                  pl.BlockSpec((None, FOX_HEADS, n), lambda b: (b, 0, 0))],
        out_specs=row,
        out_shape=jax.ShapeDtypeStruct((B * n, W), BF16),
        compiler_params=_params("parallel"),
        name="fox_cached",
    )(q, kn, vn, kc, vc, lpt, lnt)


def _ssd_kernel(xbc_ref, z_ref, sm_ref, convp_ref, ssmp_ref, cw_ref, cb_ref, alog_ref, dexp_ref, ng_ref,
                expand_ref, y_ref, convn_ref, ssmn_ref, tail_ref, st_ref, *, Q, nchunks):
    ci = pl.program_id(1)
    PAIR = 2 * SSM_HEAD_DIM
    GW = SSM_STATE

    @pl.when(ci == 0)
    def _():
        st_ref[...] = ssmp_ref[...]
        tail_ref[...] = jnp.zeros((SUBLANES, CONV_DIM), F32)
        tail_ref[SUBLANES - (CONV_WIDTH - 1):SUBLANES, :] = convp_ref[...]

    x3 = xbc_ref[...].reshape(Q // SUBLANES, SUBLANES, CONV_DIM)
    tail = tail_ref[...].reshape(1, SUBLANES, CONV_DIM)
    sub = lax.broadcasted_iota(jnp.int32, x3.shape, 1)
    conv = cb_ref[...]
    for d in range(CONV_WIDTH - 1, 0, -1):
        rot = pltpu.roll(x3, d, 1)
        prev = jnp.concatenate([pltpu.roll(tail, d, 1), rot[:-1]], axis=0)
        conv = conv + jnp.where(sub < d, prev, rot) * cw_ref[CONV_WIDTH - 1 - d:CONV_WIDTH - d, :]
    conv = (conv + x3 * cw_ref[CONV_WIDTH - 1:CONV_WIDTH, :]).reshape(Q, CONV_DIM)
    tail_ref[...] = xbc_ref[Q - SUBLANES:Q, :]
    convn_ref[...] = xbc_ref[Q - (CONV_WIDTH - 1):Q, :]

    xc = _silu(conv)
    xs = xc[:, :SSM_INNER]
    Bm = xc[:, SSM_INNER:SSM_INNER + SSM_GROUPS * GW]
    Cm = xc[:, SSM_INNER + SSM_GROUPS * GW:]

    lane1 = lax.broadcasted_iota(jnp.int32, (Q, LANES), 1)
    head_lanes = (lane1 >= DT_LANE0) & (lane1 < DT_LANE0 + SSM_HEADS)
    dt = jnp.where(head_lanes, sm_ref[...], 0.0)
    a = dt * (-jnp.exp(alog_ref[...]))
    r = lax.broadcasted_iota(jnp.int32, (Q, Q), 0)
    c = lax.broadcasted_iota(jnp.int32, (Q, Q), 1)
    causal = r >= c
    a_cs = jnp.dot(causal.astype(F32), a, precision=HIGHEST, preferred_element_type=F32)
    er = lax.broadcasted_iota(jnp.int32, (SSM_HEADS, LANES), 0)
    ec = lax.broadcasted_iota(jnp.int32, (SSM_HEADS, LANES), 1)
    pick = (ec == er + DT_LANE0).astype(F32)
    acs_t = lax.dot_general(pick, a_cs, _NT, precision=HIGHEST, preferred_element_type=F32)
    a_last = a_cs[Q - 1:Q, :]
    e_last = jnp.exp(a_last)

    def per_channel(x):
        x = jnp.where(head_lanes, x, 0.0)
        hi = x.astype(BF16).astype(F32)
        r1 = x - hi
        mid = r1.astype(BF16).astype(F32)
        packed = hi + pltpu.roll(mid, PIECE_LANES, 1) + pltpu.roll(r1 - mid, 2 * PIECE_LANES, 1)
        return jnp.dot(packed.astype(BF16), expand_ref[...], preferred_element_type=F32)

    xd = xs * per_channel(dt)
    xdb = xd.astype(BF16)
    xw = xd * per_channel(jnp.exp(a_last - a_cs))
    e_cs = per_channel(jnp.exp(a_cs))
    Bb = Bm.astype(BF16)
    Cb = Cm.astype(BF16)

    lane_lo = lax.broadcasted_iota(jnp.int32, (Q, PAIR), 1) < SSM_HEAD_DIM
    row_lo = lax.broadcasted_iota(jnp.int32, (PAIR, GW), 0) < SSM_HEAD_DIM

    y_parts = []
    for g in range(SSM_GROUPS):
        Bg = Bb[:, g * GW:(g + 1) * GW]
        Cg = Cb[:, g * GW:(g + 1) * GW]
        cbm = lax.dot_general(Cg, Bg, _NT, preferred_element_type=F32)
        for pr in range(SSM_HEADS // SSM_GROUPS // 2):
            pair = g * (SSM_HEADS // SSM_GROUPS // 2) + pr
            sl = slice(pair * PAIR, (pair + 1) * PAIR)
            s_prev = st_ref[sl, :]
            ys = []
            for hh in (2 * pair, 2 * pair + 1):
                ln = DT_LANE0 + hh
                lm = jnp.exp(jnp.where(causal, a_cs[:, ln:ln + 1] - acs_t[hh:hh + 1, :], -jnp.inf))
                ys.append(jnp.dot((cbm * lm).astype(BF16), xdb[:, sl], preferred_element_type=F32))
            y_off = lax.dot_general(Cg, s_prev.astype(BF16), _NT, preferred_element_type=F32) * e_cs[:, sl]
            ln0 = DT_LANE0 + 2 * pair
            dec = jnp.where(row_lo, e_last[:, ln0:ln0 + 1], e_last[:, ln0 + 1:ln0 + 2])
            st_ref[sl, :] = s_prev * dec + lax.dot_general(xw[:, sl], Bm[:, g * GW:(g + 1) * GW], _TN,
                                                           preferred_element_type=F32)
            y_parts.append(jnp.where(lane_lo, ys[0], ys[1]) + y_off + xs[:, sl] * dexp_ref[:, sl])

    y = jnp.concatenate(y_parts, axis=1) * _silu(z_ref[...].astype(F32))
    gw = SSM_INNER // SSM_GROUPS
    normed = []
    for g in range(SSM_GROUPS):
        yg = y[:, g * gw:(g + 1) * gw]
        normed.append(yg * lax.rsqrt(jnp.mean(yg * yg, axis=-1, keepdims=True) + EPS))
    y_ref[...] = (jnp.concatenate(normed, axis=1) * ng_ref[...]).astype(y_ref.dtype)

    @pl.when(ci == nchunks - 1)
    def _():
        ssmn_ref[...] = st_ref[...]


def _head_expander():
    k = lax.broadcasted_iota(jnp.int32, (LANES, SSM_INNER), 0)
    j = lax.broadcasted_iota(jnp.int32, (LANES, SSM_INNER), 1)
    hit = (k < BIAS_PARTS * PIECE_LANES) & (k % PIECE_LANES == DT_LANE0 + j // SSM_HEAD_DIM)
    return hit.astype(BF16)


def _ssd(xbc, z, small, conv_past, ssm_past, conv_w, conv_b, alog_pad, d_exp, norm_g, batch, Q):
    rows = xbc.shape[0]
    nchunks = rows // batch // Q
    assert Q % SUBLANES == 0 and Q >= SUBLANES and nchunks * Q * batch == rows
    row = lambda w: pl.BlockSpec((Q, w), lambda b, c: (b * nchunks + c, 0))
    state_rows = SSM_HEADS * SSM_HEAD_DIM
    per_seq = lambda a, b_: pl.BlockSpec((None, a, b_), lambda b, c: (b, 0, 0))
    return pl.pallas_call(
        functools.partial(_ssd_kernel, Q=Q, nchunks=nchunks),
        grid=(batch, nchunks),
        in_specs=[row(CONV_DIM), row(SSM_INNER), row(LANES),
                  per_seq(CONV_WIDTH - 1, CONV_DIM), per_seq(state_rows, SSM_STATE),
                  _resident((CONV_WIDTH, CONV_DIM)), _resident((1, CONV_DIM)), _resident((1, LANES)),
                  _resident((1, SSM_INNER)), _resident((1, SSM_INNER)), _resident((LANES, SSM_INNER))],
        out_specs=[row(SSM_INNER), per_seq(CONV_WIDTH - 1, CONV_DIM), per_seq(state_rows, SSM_STATE)],
        out_shape=[jax.ShapeDtypeStruct((rows, SSM_INNER), BF16),
                   jax.ShapeDtypeStruct((batch, CONV_WIDTH - 1, CONV_DIM), F32),
                   jax.ShapeDtypeStruct((batch, state_rows, SSM_STATE), F32)],
        scratch_shapes=[pltpu.VMEM((SUBLANES, CONV_DIM), F32), pltpu.VMEM((state_rows, SSM_STATE), F32)],
        compiler_params=_params("parallel", "arbitrary"),
        name="ssd_scan",
    )(xbc, z, small, conv_past, ssm_past, conv_w, conv_b, alog_pad, d_exp, norm_g, _head_expander())


def _outffn_kernel(ya_ref, ys_ref, ga_ref, gb_ref, x_ref, gt1_ref, sc2_ref, sh2_ref, gt2_ref,
                   gpm_ref, gpf_ref, gqf_ref, wo_ref, wu_ref, wd_ref, o_ref):
    merged = (_sigmoid(ga_ref[...].astype(F32)) * ya_ref[...].astype(F32)
              + _sigmoid(gb_ref[...].astype(F32)) * ys_ref[...].astype(F32))
    m = jnp.dot(merged.astype(BF16), wo_ref[...], preferred_element_type=F32)
    x1 = x_ref[...] + gt1_ref[...] * _rms(m, gpm_ref[...])
    hb = (_rms(x1, gpf_ref[...]) * (1.0 + sc2_ref[...]) + sh2_ref[...]).astype(BF16)
    f = jnp.zeros(x1.shape, F32)
    for cc in range(D_FF // D_MODEL):
        sl = slice(cc * D_MODEL, (cc + 1) * D_MODEL)
        up = jnp.dot(hb, wu_ref[:, sl], preferred_element_type=F32)
        act = jnp.square(jnp.maximum(up, 0.0)).astype(BF16)
        f = f + jnp.dot(act, wd_ref[sl, :], preferred_element_type=F32)
    o_ref[...] = x1 + gt2_ref[...] * _rms(f, gqf_ref[...])


def _out_ffn(ya, ys, ga, gb, x2d, gt1, sc2, sh2, gt2, g_post_mix, g_pre_ffn, g_post_ffn, w_out, w_up, w_down, tm):
    rows = x2d.shape[0]
    row = pl.BlockSpec((tm, D_MODEL), lambda i: (i, 0))
    vec = _resident((1, D_MODEL))
    return pl.pallas_call(
        _outffn_kernel,
        grid=(rows // tm,),
        in_specs=[row, row, row, row, row,
                  _mod_spec(gt1, tm), _mod_spec(sc2, tm), _mod_spec(sh2, tm), _mod_spec(gt2, tm),
                  vec, vec, vec,
                  _resident((D_MODEL, D_MODEL)), _resident((D_MODEL, D_FF)), _resident((D_FF, D_MODEL))],
        out_specs=row,
        out_shape=jax.ShapeDtypeStruct((rows, D_MODEL), F32),
        compiler_params=_params("parallel"),
        name="out_ffn",
    )(ya, ys, ga, gb, x2d, gt1, sc2, sh2, gt2, g_post_mix, g_pre_ffn, g_post_ffn, w_out, w_up, w_down)


def _layer(x, mod, past, wts, *, tm, tq, Q):
    b, L, _ = x.shape
    x2d = x.reshape(b * L, D_MODEL)
    sh1, sc1, gt1, sh2, sc2, gt2 = mod
    q, kf, vf, kb, vb, z, xbc, ga, gb, small, *stats = _in_proj(
        x2d, wts["g_pre_mix"], sc1, sh1, wts["w_big"], wts["w_small"], wts["b_small"], tm, past is None)

    if past is None:
        assert b == 1 and tm == tq, "block statistics are per projection row block"
        y_att = _fox_prompt(q, kb, vb, _first_blocks(stats[0], L // tq), tq)
        conv_past = jnp.zeros((b, CONV_WIDTH - 1, CONV_DIM), F32)
        ssm_past = jnp.zeros((b, SSM_INNER, SSM_STATE), F32)
    else:
        k_past, v_past, logf_past, conv_past, ssm_past = past
        P = k_past.shape[1]
        lpt = jnp.swapaxes(logf_past, 1, 2)
        lnt = jnp.swapaxes(small[:, :FOX_HEADS].reshape(b, L, FOX_HEADS), 1, 2)
        y_att = _fox_cached(q, kb, vb, k_past, v_past, lpt, lnt, L)
        ssm_past = ssm_past.reshape(b, SSM_INNER, SSM_STATE)

    y_ssm, conv_new, ssm_new = _ssd(xbc, z, small, conv_past, ssm_past, wts["conv_w"], wts["conv_b"],
                                    wts["alog_pad"], wts["d_exp"], wts["ssm_norm_g"], b, Q)
    y = _out_ffn(y_att, y_ssm, ga, gb, x2d, gt1, sc2, sh2, gt2, wts["g_post_mix"], wts["g_pre_ffn"],
                 wts["g_post_ffn"], wts["w_out"], wts["w_up"], wts["w_down"], tm)
    return (y.reshape(b, L, D_MODEL),
            kf.reshape(b, L, FOX_HEADS, FOX_HEAD_DIM), vf.reshape(b, L, FOX_HEADS, FOX_HEAD_DIM),
            small[:, :FOX_HEADS].reshape(b, L, FOX_HEADS), conv_new,
            ssm_new.reshape(b, SSM_HEADS, SSM_HEAD_DIM, SSM_STATE))


def _prep_weights(w_ada, b_ada, g_pre_mix, g_post_mix, g_pre_ffn, g_post_ffn, w_in, b_f, conv_w, conv_b,
                  dt_bias, a_log, d_skip, ssm_norm_g, w_out, w_up, w_down):
    sizes = (D_MODEL, D_MODEL, D_MODEL, FOX_HEADS, SSM_INNER, CONV_DIM, SSM_HEADS, D_MODEL, D_MODEL)
    offs = [0]
    for s in sizes:
        offs.append(offs[-1] + s)
    piece = lambda i: w_in[:, offs[i]:offs[i + 1]]
    pad = LANES - FOX_HEADS - SSM_HEADS
    row = lambda v: v.reshape(1, -1).astype(F32)
    return {
        "w_big": jnp.concatenate([piece(0), piece(1), piece(2), piece(4), piece(5), piece(7), piece(8)],
                                 axis=1).astype(BF16),
        "w_small": jnp.pad(jnp.concatenate([piece(3), piece(6)], axis=1), ((0, 0), (0, pad))).astype(BF16),
        "b_small": jnp.pad(jnp.concatenate([b_f, dt_bias]), (0, pad)).reshape(1, LANES).astype(F32),
        "alog_pad": jnp.pad(a_log, (DT_LANE0, LANES - DT_LANE0 - SSM_HEADS)).reshape(1, LANES).astype(F32),
        "d_exp": jnp.repeat(d_skip, SSM_HEAD_DIM).reshape(1, SSM_INNER).astype(F32),
        "g_pre_mix": row(g_pre_mix), "g_post_mix": row(g_post_mix),
        "g_pre_ffn": row(g_pre_ffn), "g_post_ffn": row(g_post_ffn),
        "conv_w": conv_w.astype(F32), "conv_b": row(conv_b), "ssm_norm_g": row(ssm_norm_g),
        "w_out": w_out.astype(BF16), "w_up": w_up.astype(BF16), "w_down": w_down.astype(BF16),
    }


def _forward(x_prompt, x_sample, c_prompt, c_sample, cache_fox_k, cache_fox_v, cache_fox_logf,
             state_ssm_conv, state_ssm, w_ada, b_ada, *layer_w, tm_prompt, tq, q_prompt):
    depth = w_ada.shape[0]
    bp, Lp, _ = x_prompt.shape
    bs, Ls, _ = x_sample.shape
    yp, ys = x_prompt, x_sample
    outs_p, outs_s = [], []
    for i in range(depth):
        wts = _prep_weights(w_ada[i], b_ada[i], *[w[i] for w in layer_w])
        mod = _ada(jnp.concatenate([c_prompt, c_sample], axis=0), w_ada[i], b_ada[i].reshape(1, -1))
        mod_p = [m for m in jnp.split(mod[:bp], N_MOD, axis=-1)]
        mod_s = [jnp.repeat(m, Ls, axis=0) for m in jnp.split(mod[bp:], N_MOD, axis=-1)]
        rp = _layer(yp, mod_p, None, wts, tm=tm_prompt, tq=tq, Q=q_prompt)
        rs = _layer(ys, mod_s, (cache_fox_k[i], cache_fox_v[i], cache_fox_logf[i], state_ssm_conv[i], state_ssm[i]),
                    wts, tm=bs * Ls, tq=None, Q=Ls)
        yp, ys = rp[0], rs[0]
        outs_p.append(rp[1:])
        outs_s.append(rs[1:])
    stack = lambda outs, j: jnp.stack([o[j] for o in outs])
    return (yp, ys) + tuple(stack(outs_p, j) for j in range(5)) + tuple(stack(outs_s, j) for j in range(5))


def kernel(x_prompt, x_sample, c_prompt, c_sample, cache_fox_k, cache_fox_v, cache_fox_logf, state_ssm_conv,
           state_ssm, w_ada, b_ada, g_pre_mix, g_post_mix, g_pre_ffn, g_post_ffn, w_in, b_f, conv_w, conv_b,
           dt_bias, a_log, d_skip, ssm_norm_g, w_out, w_up, w_down):
    assert x_prompt.shape[0] == 1, "the prompt path carries one sequence"
    L = x_prompt.shape[1]
    return _forward(x_prompt, x_sample, c_prompt, c_sample, cache_fox_k, cache_fox_v, cache_fox_logf,
                    state_ssm_conv, state_ssm, w_ada, b_ada, g_pre_mix, g_post_mix, g_pre_ffn, g_post_ffn,
                    w_in, b_f, conv_w, conv_b, dt_bias, a_log, d_skip, ssm_norm_g, w_out, w_up, w_down,
                    tm_prompt=min(512, L), tq=min(512, L), q_prompt=min(256, L))
```

```python
import functools

import jax
import jax.numpy as jnp
from jax import lax
from jax.experimental import pallas as pl
from jax.experimental.pallas import tpu as pltpu

F32 = jnp.float32
BF16 = jnp.bfloat16
HIGHEST = lax.Precision.HIGHEST

D_MODEL = 1024
FOX_HEADS = 8
FOX_HEAD_DIM = 128
FOX_SCALE = FOX_HEAD_DIM ** -0.5
SSM_HEADS = 16
SSM_HEAD_DIM = 64
SSM_GROUPS = 4
SSM_STATE = 128
SSM_INNER = SSM_HEADS * SSM_HEAD_DIM
CONV_WIDTH = 4
CONV_DIM = SSM_INNER + 2 * SSM_GROUPS * SSM_STATE
D_FF = 4 * D_MODEL
N_MOD = 6
EPS = 1e-6

LANES = 128
SUBLANES = 8
DT_LANE0 = FOX_HEADS
VMEM_LIMIT = 56 * 1024 * 1024
LOG2E = 1.4426950408889634
BIAS_PARTS = 3
PIECE_LANES = 32
CUM_ROWS = 128
ROW_TILE = 16
UNROLL = 4
SKIP_MARGIN = 160.0


_NT = (((1,), (1,)), ((), ()))
_TN = (((0,), (0,)), ((), ()))


def _params(*sem):
    return pltpu.CompilerParams(dimension_semantics=sem, vmem_limit_bytes=VMEM_LIMIT)


def _resident(shape):
    zeros = (0,) * len(shape)
    return pl.BlockSpec(shape, lambda *_: zeros, pipeline_mode=pl.Buffered(1))


def _rms(x, g):
    return x * lax.rsqrt(jnp.mean(x * x, axis=-1, keepdims=True) + EPS) * g


def _sigmoid(x):
    return 0.5 * jnp.tanh(0.5 * x) + 0.5


def _silu(x):
    h = 0.5 * x
    return h * jnp.tanh(h) + h


def _ada_kernel(c_ref, w_ref, b_ref, o_ref):
    s = _silu(c_ref[...]).astype(BF16)
    o_ref[...] = jnp.dot(s, w_ref[...].astype(BF16), preferred_element_type=F32) + b_ref[...]


def _ada(c_all, w_ada, b_ada):
    rows = c_all.shape[0]
    width = w_ada.shape[1]
    tn = 1536
    return pl.pallas_call(
        _ada_kernel,
        grid=(width // tn,),
        in_specs=[pl.BlockSpec((rows, D_MODEL), lambda j: (0, 0)),
                  pl.BlockSpec((D_MODEL, tn), lambda j: (0, j)),
                  pl.BlockSpec((1, tn), lambda j: (0, j))],
        out_specs=pl.BlockSpec((rows, tn), lambda j: (0, j)),
        out_shape=jax.ShapeDtypeStruct((rows, width), F32),
        compiler_params=_params("parallel"),
        name="ada_mod",
    )(c_all, w_ada, b_ada)


def _inproj_kernel(x_ref, g_ref, sc_ref, sh_ref, wqkv_ref, wssm_ref, wgate_ref, ws_ref, bs_ref, *rest, with_bias):
    if with_bias:
        (sel_ref, q_ref, kf_ref, vf_ref, kb_ref, vb_ref, z_ref, xbc_ref, ga_ref, gb_ref, sm_ref, stats_ref,
         carry_ref) = rest
    else:
        q_ref, kf_ref, vf_ref, kb_ref, vb_ref, z_ref, xbc_ref, ga_ref, gb_ref, sm_ref = rest
    h = _rms(x_ref[...], g_ref[...]) * (1.0 + sc_ref[...]) + sh_ref[...]
    hb = h.astype(BF16)

    def proj(w_ref, lo, width):
        return jnp.dot(hb, w_ref[:, lo:lo + width], preferred_element_type=F32)

    u = jnp.dot(hb, ws_ref[...], preferred_element_type=F32) + bs_ref[...]
    t = jnp.log1p(jnp.exp(-jnp.abs(u)))
    lane = lax.broadcasted_iota(jnp.int32, u.shape, 1)
    sm = jnp.where(lane < DT_LANE0, jnp.minimum(u, 0.0) - t,
                   jnp.where(lane < DT_LANE0 + SSM_HEADS, jnp.maximum(u, 0.0) + t, 0.0))
    sm_ref[...] = sm

    qb = (proj(wqkv_ref, 0, D_MODEL) * (FOX_SCALE * LOG2E)).astype(BF16)
    q_ref[...] = qb
    k = proj(wqkv_ref, D_MODEL, D_MODEL)
    kf_ref[...] = k
    if with_bias:
        @pl.when(pl.program_id(0) == 0)
        def _():
            carry_ref[...] = jnp.zeros_like(carry_ref)

        r = lax.broadcasted_iota(jnp.int32, (CUM_ROWS, CUM_ROWS), 0)
        c = lax.broadcasted_iota(jnp.int32, (CUM_ROWS, CUM_ROWS), 1)
        tri = (r >= c).astype(F32)
        run = carry_ref[0:1, :]
        parts = []
        for b in range(sm.shape[0] // CUM_ROWS):
            cs = jnp.dot(tri, sm[b * CUM_ROWS:(b + 1) * CUM_ROWS, :], precision=HIGHEST,
                         preferred_element_type=F32) + run
            run = cs[CUM_ROWS - 1:CUM_ROWS, :]
            parts.append(cs)
        carry_ref[0:1, :] = run
        nb = jnp.concatenate(parts, axis=0) * (-LOG2E)
        hi = nb.astype(BF16)
        r1 = nb - hi.astype(F32)
        mid = r1.astype(BF16)
        lo = (r1 - mid.astype(F32)).astype(BF16)
        aug = jnp.dot(jnp.concatenate([hi, mid, lo], axis=1), sel_ref[...], preferred_element_type=F32)
        kb = k.astype(BF16)
        for hd in range(FOX_HEADS):
            sl = slice(hd * FOX_HEAD_DIM, (hd + 1) * FOX_HEAD_DIM)
            kb_ref[:, 2 * hd * FOX_HEAD_DIM:(2 * hd + 1) * FOX_HEAD_DIM] = kb[:, sl]
            kb_ref[:, (2 * hd + 1) * FOX_HEAD_DIM:(2 * hd + 2) * FOX_HEAD_DIM] = aug[:, sl].astype(BF16)

        def max_sq_norms(xb):
            x2 = xb.astype(F32)
            x2 = x2 * x2
            out = jnp.zeros((1, LANES), F32)
            for hd in range(FOX_HEADS):
                n2 = jnp.sum(x2[:, hd * FOX_HEAD_DIM:(hd + 1) * FOX_HEAD_DIM], axis=-1, keepdims=True)
                out = jnp.where(lane[0:1, :] == hd, jnp.max(n2, axis=0, keepdims=True), out)
            return out

        stats_ref[...] = jnp.zeros(stats_ref.shape, F32)
        stats_ref[0:1, :] = max_sq_norms(qb)
        stats_ref[1:2, :] = max_sq_norms(kb)
        stats_ref[2:3, :] = nb[0:1, :]
        stats_ref[3:4, :] = nb[nb.shape[0] - 1:, :]
    else:
        kb_ref[...] = k.astype(BF16)
    v = proj(wqkv_ref, 2 * D_MODEL, D_MODEL)
    vf_ref[...] = v
    vb_ref[...] = v.astype(BF16)
    z_ref[...] = proj(wssm_ref, 0, SSM_INNER).astype(BF16)
    xbc_ref[...] = proj(wssm_ref, SSM_INNER, CONV_DIM)
    ga_ref[...] = proj(wgate_ref, 0, D_MODEL).astype(BF16)
    gb_ref[...] = proj(wgate_ref, D_MODEL, D_MODEL).astype(BF16)


def _mod_spec(arr, tm):
    if arr.shape[0] == 1:
        return pl.BlockSpec((1, D_MODEL), lambda i: (0, 0))
    return pl.BlockSpec((tm, D_MODEL), lambda i: (i, 0))


def _bias_selector():
    r = lax.broadcasted_iota(jnp.int32, (BIAS_PARTS * LANES, FOX_HEADS * FOX_HEAD_DIM), 0)
    c = lax.broadcasted_iota(jnp.int32, (BIAS_PARTS * LANES, FOX_HEADS * FOX_HEAD_DIM), 1)
    hit = (r % LANES < FOX_HEADS) & (c == (r % LANES) * FOX_HEAD_DIM + r // LANES)
    return hit.astype(BF16)


def _in_proj(x2d, g, sc, sh, w_qkv, w_ssm, w_gate, w_small, b_small, tm, with_bias):
    rows = x2d.shape[0]
    assert not with_bias or tm % CUM_ROWS == 0
    row = lambda w: pl.BlockSpec((tm, w), lambda i: (i, 0))
    shp = lambda w, dt: jax.ShapeDtypeStruct((rows, w), dt)
    kb_width = 2 * D_MODEL if with_bias else D_MODEL
    in_specs = [row(D_MODEL), _resident((1, D_MODEL)), _mod_spec(sc, tm), _mod_spec(sh, tm),
                _resident(w_qkv.shape), _resident(w_ssm.shape), _resident(w_gate.shape),
                _resident((D_MODEL, LANES)), _resident((1, LANES))]
    args = [x2d, g, sc, sh, w_qkv, w_ssm, w_gate, w_small, b_small]
    out_specs = [row(D_MODEL), row(D_MODEL), row(D_MODEL), row(kb_width), row(D_MODEL),
                 row(SSM_INNER), row(CONV_DIM), row(D_MODEL), row(D_MODEL), row(LANES)]
    out_shape = [shp(D_MODEL, BF16), shp(D_MODEL, F32), shp(D_MODEL, F32), shp(kb_width, BF16),
                 shp(D_MODEL, BF16), shp(SSM_INNER, BF16), shp(CONV_DIM, F32), shp(D_MODEL, BF16),
                 shp(D_MODEL, BF16), shp(LANES, F32)]
    if with_bias:
        in_specs.append(_resident((BIAS_PARTS * LANES, FOX_HEADS * FOX_HEAD_DIM)))
        args.append(_bias_selector())
        out_specs.append(pl.BlockSpec((SUBLANES, LANES), lambda i: (i, 0)))
        out_shape.append(jax.ShapeDtypeStruct((rows // tm * SUBLANES, LANES), F32))
    return pl.pallas_call(
        functools.partial(_inproj_kernel, with_bias=with_bias),
        grid=(rows // tm,),
        in_specs=in_specs,
        out_specs=out_specs,
        out_shape=out_shape,
        scratch_shapes=[pltpu.VMEM((SUBLANES, LANES), F32)] if with_bias else [],
        compiler_params=_params("arbitrary" if with_bias else "parallel"),
        name="in_proj",
    )(*args)


def _flash_kernel(first_ref, q_ref, k_ref, v_ref, o_ref, s0_ref, s1_ref, p0_ref, p1_ref, m_ref, l_ref, alpha0_ref,
                  alpha1_ref, acc_ref, *, tq, tk):
    i = pl.program_id(1)
    nlt = tk // LANES
    lane = lax.broadcasted_iota(jnp.int32, (tq, FOX_HEAD_DIM), 1)
    ones = jnp.where(lane < BIAS_PARTS, 1.0, 0.0).astype(BF16)
    q2 = jnp.concatenate([q_ref[...], ones], axis=1)

    def scores(j, s_ref):
        off = pl.multiple_of(j * tk, tk)
        s_ref[...] = lax.dot_general(q2, k_ref[pl.ds(off, tk), :], _NT, preferred_element_type=F32)

    def softmax(s_ref, p_ref, alpha_ref, mask_shift):
        for rt in range(tq // ROW_TILE):
            rows = slice(rt * ROW_TILE, (rt + 1) * ROW_TILE)
            cols = [s_ref[rows, c * LANES:(c + 1) * LANES] for c in range(nlt)]
            if mask_shift is not None:
                ri = lax.broadcasted_iota(jnp.int32, (ROW_TILE, LANES), 0) + rt * ROW_TILE
                ci = lax.broadcasted_iota(jnp.int32, (ROW_TILE, LANES), 1) + mask_shift
                cols = [jnp.where(ci + c * LANES <= ri, cols[c], -jnp.inf) for c in range(nlt)]
            mx = functools.reduce(jnp.maximum, cols)
            m_old = m_ref[rows, :]
            m_new = jnp.maximum(m_old, jnp.max(mx, axis=-1, keepdims=True))
            alpha = jnp.exp2(m_old - m_new)
            ps = [jnp.exp2(cc - m_new) for cc in cols]
            l_ref[rows, :] = alpha * l_ref[rows, :] + jnp.sum(functools.reduce(jnp.add, ps), axis=-1, keepdims=True)
            m_ref[rows, :] = m_new
            alpha_ref[rows, :] = alpha
            for c in range(nlt):
                p_ref[rows, c * LANES:(c + 1) * LANES] = ps[c].astype(BF16)

    def weighted_values(j, p_ref, alpha_ref):
        off = pl.multiple_of(j * tk, tk)
        acc_ref[...] = alpha_ref[...] * acc_ref[...] + jnp.dot(p_ref[...], v_ref[pl.ds(off, tk), :],
                                                               preferred_element_type=F32)

    m_ref[...] = jnp.full(m_ref.shape, -jnp.inf, F32)
    l_ref[...] = jnp.zeros(l_ref.shape, F32)
    acc_ref[...] = jnp.zeros(acc_ref.shape, F32)
    bufs = ((s0_ref, p0_ref, alpha0_ref), (s1_ref, p1_ref, alpha1_ref))

    def run(first, n, diagonal_last, next_scores):
        for k in range(n):
            s_ref, p_ref, alpha_ref = bufs[k % 2]
            s_nxt, p_prv, alpha_prv = bufs[(k + 1) % 2]
            weighted_values(jnp.maximum(first + k - 1, 0), p_prv, alpha_prv)
            if k + 1 < n or next_scores:
                scores(first + k + 1, s_nxt)
            softmax(s_ref, p_ref, alpha_ref, 0 if (diagonal_last and k == n - 1) else None)

    p1_ref[...] = jnp.zeros(p1_ref.shape, BF16)
    alpha1_ref[...] = jnp.ones(alpha1_ref.shape, F32)
    j0 = first_ref[pl.program_id(0), i]
    n_full = i - j0
    scores(j0, s0_ref)

    @pl.loop(0, n_full // UNROLL)
    def _(t):
        run(j0 + UNROLL * t, UNROLL, False, True)

    for rem in range(UNROLL):
        @pl.when(n_full % UNROLL == rem)
        def _():
            run(i - rem, rem + 1, True, False)
            weighted_values(i, *bufs[rem % 2][1:])

    o_ref[...] = (acc_ref[...] / l_ref[...]).astype(o_ref.dtype)


def _first_blocks(stats, nq):
    st = stats.reshape(nq, SUBLANES, LANES)
    qn = jnp.sqrt(st[:, 0, :FOX_HEADS])
    kn = jnp.sqrt(st[:, 1, :FOX_HEADS])
    b_first, b_last = st[:, 2, :FOX_HEADS], st[:, 3, :FOX_HEADS]
    bound = qn[:, None, :] * (kn[None, :, :] + kn[:, None, :]) + b_last[None, :, :] - b_first[:, None, :]
    blocks = jnp.arange(nq, dtype=jnp.int32)
    first_needed = jnp.min(jnp.where(bound <= -SKIP_MARGIN, nq, blocks[None, :, None]), axis=1)
    return jnp.minimum(first_needed, blocks[:, None]).T


def _fox_prompt(q, kp, v, first, tq):
    L = q.shape[0]
    tk = tq
    return pl.pallas_call(
        functools.partial(_flash_kernel, tq=tq, tk=tk),
        grid=(FOX_HEADS, L // tq),
        in_specs=[pl.BlockSpec(memory_space=pltpu.SMEM),
                  pl.BlockSpec((tq, FOX_HEAD_DIM), lambda h, i: (i, h)),
                  pl.BlockSpec((L, 2 * FOX_HEAD_DIM), lambda h, i: (0, h)),
                  pl.BlockSpec((L, FOX_HEAD_DIM), lambda h, i: (0, h))],
        out_specs=pl.BlockSpec((tq, FOX_HEAD_DIM), lambda h, i: (i, h)),
        out_shape=jax.ShapeDtypeStruct((L, FOX_HEADS * FOX_HEAD_DIM), BF16),
        scratch_shapes=[pltpu.VMEM((tq, tk), F32)] * 2 + [pltpu.VMEM((tq, tk), BF16)] * 2
        + [pltpu.VMEM((tq, FOX_HEAD_DIM), F32)] * 5,
        compiler_params=_params("parallel", "arbitrary"),
        name="fox_flash",
    )(first, q, kp, v)


def _cached_attn_kernel(q_ref, kn_ref, vn_ref, kc_ref, vc_ref, lpt_ref, lnt_ref, o_ref):
    n = q_ref.shape[0]
    P = kc_ref.shape[0]
    r = lax.broadcasted_iota(jnp.int32, (P, P), 0)
    c = lax.broadcasted_iota(jnp.int32, (P, P), 1)
    G = jnp.dot(lpt_ref[...], (r > c).astype(F32), precision=HIGHEST, preferred_element_type=F32) * LOG2E
    rn = lax.broadcasted_iota(jnp.int32, (n, n), 0)
    cn = lax.broadcasted_iota(jnp.int32, (n, n), 1)
    Hn = jnp.dot(lnt_ref[...], (rn <= cn).astype(F32), precision=HIGHEST, preferred_element_type=F32) * LOG2E
    causal = cn <= rn
    for h in range(FOX_HEADS):
        sl = slice(h * FOX_HEAD_DIM, (h + 1) * FOX_HEAD_DIM)
        qh = q_ref[:, sl]
        sp = lax.dot_general(qh, kc_ref[:, sl], _NT, preferred_element_type=F32) + G[h:h + 1, :]
        sn = lax.dot_general(qh, kn_ref[:, sl], _NT, preferred_element_type=F32) - Hn[h:h + 1, :]
        sn = jnp.where(causal, sn, -jnp.inf)
        m = jnp.maximum(jnp.max(sp, axis=-1, keepdims=True), jnp.max(sn, axis=-1, keepdims=True))
        pp = jnp.exp2(sp - m)
        pn = jnp.exp2(sn - m)
        l = jnp.sum(pp, axis=-1, keepdims=True) + jnp.sum(pn, axis=-1, keepdims=True)
        o = (jnp.dot(pp.astype(BF16), vc_ref[:, sl], preferred_element_type=F32)
             + jnp.dot(pn.astype(BF16), vn_ref[:, sl], preferred_element_type=F32))
        o_ref[:, sl] = (o / l).astype(o_ref.dtype)


def _fox_cached(q, kn, vn, kc, vc, lpt, lnt, n):
    B, P, W = kc.shape
    row = pl.BlockSpec((n, W), lambda b: (b, 0))
    cache = pl.BlockSpec((None, P, W), lambda b: (b, 0, 0))
    return pl.pallas_call(
        _cached_attn_kernel,
        grid=(B,),
        in_specs=[row, row, row, cache, cache,
                  pl.BlockSpec((None, FOX_HEADS, P), lambda b: (b, 0, 0)),
                  pl.BlockSpec((None, FOX_HEADS, n), lambda b: (b, 0, 0))],
        out_specs=row,
        out_shape=jax.ShapeDtypeStruct((B * n, W), BF16),
        compiler_params=_params("parallel"),
        name="fox_cached",
    )(q, kn, vn, kc, vc, lpt, lnt)


def _ssd_kernel(xbc_ref, z_ref, sm_ref, convp_ref, ssmp_ref, cw_ref, cb_ref, alog_ref, dexp_ref, ng_ref,
                expand_ref, y_ref, convn_ref, ssmn_ref, tail_ref, st_ref, *, Q, nchunks):
    ci = pl.program_id(1)
    PAIR = 2 * SSM_HEAD_DIM
    GW = SSM_STATE

    @pl.when(ci == 0)
    def _():
        st_ref[...] = ssmp_ref[...]
        tail_ref[...] = jnp.zeros((SUBLANES, CONV_DIM), F32)
        tail_ref[SUBLANES - (CONV_WIDTH - 1):SUBLANES, :] = convp_ref[...]

    x3 = xbc_ref[...].reshape(Q // SUBLANES, SUBLANES, CONV_DIM)
    tail = tail_ref[...].reshape(1, SUBLANES, CONV_DIM)
    sub = lax.broadcasted_iota(jnp.int32, x3.shape, 1)
    conv = cb_ref[...]
    for d in range(CONV_WIDTH - 1, 0, -1):
        rot = pltpu.roll(x3, d, 1)
        prev = jnp.concatenate([pltpu.roll(tail, d, 1), rot[:-1]], axis=0)
        conv = conv + jnp.where(sub < d, prev, rot) * cw_ref[CONV_WIDTH - 1 - d:CONV_WIDTH - d, :]
    conv = (conv + x3 * cw_ref[CONV_WIDTH - 1:CONV_WIDTH, :]).reshape(Q, CONV_DIM)
    tail_ref[...] = xbc_ref[Q - SUBLANES:Q, :]
    convn_ref[...] = xbc_ref[Q - (CONV_WIDTH - 1):Q, :]

    xc = _silu(conv)
    xs = xc[:, :SSM_INNER]
    Bm = xc[:, SSM_INNER:SSM_INNER + SSM_GROUPS * GW]
    Cm = xc[:, SSM_INNER + SSM_GROUPS * GW:]

    lane1 = lax.broadcasted_iota(jnp.int32, (Q, LANES), 1)
    head_lanes = (lane1 >= DT_LANE0) & (lane1 < DT_LANE0 + SSM_HEADS)
    dt = jnp.where(head_lanes, sm_ref[...], 0.0)
    a = dt * (-jnp.exp(alog_ref[...]))
    r = lax.broadcasted_iota(jnp.int32, (Q, Q), 0)
    c = lax.broadcasted_iota(jnp.int32, (Q, Q), 1)
    causal = r >= c
    a_cs = jnp.dot(causal.astype(F32), a, precision=HIGHEST, preferred_element_type=F32)
    er = lax.broadcasted_iota(jnp.int32, (SSM_HEADS, LANES), 0)
    ec = lax.broadcasted_iota(jnp.int32, (SSM_HEADS, LANES), 1)
    pick = (ec == er + DT_LANE0).astype(F32)
    acs_t = lax.dot_general(pick, a_cs, _NT, precision=HIGHEST, preferred_element_type=F32)
    a_last = a_cs[Q - 1:Q, :]
    e_last = jnp.exp(a_last)

    def per_channel(x):
        x = jnp.where(head_lanes, x, 0.0)
        hi = x.astype(BF16).astype(F32)
        r1 = x - hi
        mid = r1.astype(BF16).astype(F32)
        packed = hi + pltpu.roll(mid, PIECE_LANES, 1) + pltpu.roll(r1 - mid, 2 * PIECE_LANES, 1)
        return jnp.dot(packed.astype(BF16), expand_ref[...], preferred_element_type=F32)

    xd = xs * per_channel(dt)
    xdb = xd.astype(BF16)
    xw = xd * per_channel(jnp.exp(a_last - a_cs))
    e_cs = per_channel(jnp.exp(a_cs))
    Bb = Bm.astype(BF16)
    Cb = Cm.astype(BF16)

    lane_lo = lax.broadcasted_iota(jnp.int32, (Q, PAIR), 1) < SSM_HEAD_DIM
    row_lo = lax.broadcasted_iota(jnp.int32, (PAIR, GW), 0) < SSM_HEAD_DIM

    y_parts = []
    for g in range(SSM_GROUPS):
        Bg = Bb[:, g * GW:(g + 1) * GW]
        Cg = Cb[:, g * GW:(g + 1) * GW]
        cbm = lax.dot_general(Cg, Bg, _NT, preferred_element_type=F32)
        for pr in range(SSM_HEADS // SSM_GROUPS // 2):
            pair = g * (SSM_HEADS // SSM_GROUPS // 2) + pr
            sl = slice(pair * PAIR, (pair + 1) * PAIR)
            s_prev = st_ref[sl, :]
            ys = []
            for hh in (2 * pair, 2 * pair + 1):
                ln = DT_LANE0 + hh
                lm = jnp.exp(jnp.where(causal, a_cs[:, ln:ln + 1] - acs_t[hh:hh + 1, :], -jnp.inf))
                ys.append(jnp.dot((cbm * lm).astype(BF16), xdb[:, sl], preferred_element_type=F32))
            y_off = lax.dot_general(Cg, s_prev.astype(BF16), _NT, preferred_element_type=F32) * e_cs[:, sl]
            ln0 = DT_LANE0 + 2 * pair
            dec = jnp.where(row_lo, e_last[:, ln0:ln0 + 1], e_last[:, ln0 + 1:ln0 + 2])
            st_ref[sl, :] = s_prev * dec + lax.dot_general(xw[:, sl], Bm[:, g * GW:(g + 1) * GW], _TN,
                                                           preferred_element_type=F32)
            y_parts.append(jnp.where(lane_lo, ys[0], ys[1]) + y_off + xs[:, sl] * dexp_ref[:, sl])

    y = jnp.concatenate(y_parts, axis=1) * _silu(z_ref[...].astype(F32))
    gw = SSM_INNER // SSM_GROUPS
    normed = []
    for g in range(SSM_GROUPS):
        yg = y[:, g * gw:(g + 1) * gw]
        normed.append(yg * lax.rsqrt(jnp.mean(yg * yg, axis=-1, keepdims=True) + EPS))
    y_ref[...] = (jnp.concatenate(normed, axis=1) * ng_ref[...]).astype(y_ref.dtype)

    @pl.when(ci == nchunks - 1)
    def _():
        ssmn_ref[...] = st_ref[...]


def _head_expander():
    k = lax.broadcasted_iota(jnp.int32, (LANES, SSM_INNER), 0)
    j = lax.broadcasted_iota(jnp.int32, (LANES, SSM_INNER), 1)
    hit = (k < BIAS_PARTS * PIECE_LANES) & (k % PIECE_LANES == DT_LANE0 + j // SSM_HEAD_DIM)
    return hit.astype(BF16)


def _ssd(xbc, z, small, conv_past, ssm_past, conv_w, conv_b, alog_pad, d_exp, norm_g, batch, Q):
    rows = xbc.shape[0]
    nchunks = rows // batch // Q
    assert Q % SUBLANES == 0 and Q >= SUBLANES and nchunks * Q * batch == rows
    row = lambda w: pl.BlockSpec((Q, w), lambda b, c: (b * nchunks + c, 0))
    state_rows = SSM_HEADS * SSM_HEAD_DIM
    per_seq = lambda a, b_: pl.BlockSpec((None, a, b_), lambda b, c: (b, 0, 0))
    return pl.pallas_call(
        functools.partial(_ssd_kernel, Q=Q, nchunks=nchunks),
        grid=(batch, nchunks),
        in_specs=[row(CONV_DIM), row(SSM_INNER), row(LANES),
                  per_seq(CONV_WIDTH - 1, CONV_DIM), per_seq(state_rows, SSM_STATE),
                  _resident((CONV_WIDTH, CONV_DIM)), _resident((1, CONV_DIM)), _resident((1, LANES)),
                  _resident((1, SSM_INNER)), _resident((1, SSM_INNER)), _resident((LANES, SSM_INNER))],
        out_specs=[row(SSM_INNER), per_seq(CONV_WIDTH - 1, CONV_DIM), per_seq(state_rows, SSM_STATE)],
        out_shape=[jax.ShapeDtypeStruct((rows, SSM_INNER), BF16),
                   jax.ShapeDtypeStruct((batch, CONV_WIDTH - 1, CONV_DIM), F32),
                   jax.ShapeDtypeStruct((batch, state_rows, SSM_STATE), F32)],
        scratch_shapes=[pltpu.VMEM((SUBLANES, CONV_DIM), F32), pltpu.VMEM((state_rows, SSM_STATE), F32)],
        compiler_params=_params("parallel", "arbitrary"),
        name="ssd_scan",
    )(xbc, z, small, conv_past, ssm_past, conv_w, conv_b, alog_pad, d_exp, norm_g, _head_expander())


def _outffn_kernel(ya_ref, ys_ref, ga_ref, gb_ref, x_ref, gt1_ref, sc2_ref, sh2_ref, gt2_ref,
                   gpm_ref, gpf_ref, gqf_ref, wo_ref, wu_ref, wd_ref, o_ref):
    merged = (_sigmoid(ga_ref[...].astype(F32)) * ya_ref[...].astype(F32)
              + _sigmoid(gb_ref[...].astype(F32)) * ys_ref[...].astype(F32))
    m = jnp.dot(merged.astype(BF16), wo_ref[...], preferred_element_type=F32)
    x1 = x_ref[...] + gt1_ref[...] * _rms(m, gpm_ref[...])
    hb = (_rms(x1, gpf_ref[...]) * (1.0 + sc2_ref[...]) + sh2_ref[...]).astype(BF16)
    f = jnp.zeros(x1.shape, F32)
    for cc in range(D_FF // D_MODEL):
        sl = slice(cc * D_MODEL, (cc + 1) * D_MODEL)
        up = jnp.dot(hb, wu_ref[:, sl], preferred_element_type=F32)
        act = jnp.square(jnp.maximum(up, 0.0)).astype(BF16)
        f = f + jnp.dot(act, wd_ref[sl, :], preferred_element_type=F32)
    o_ref[...] = x1 + gt2_ref[...] * _rms(f, gqf_ref[...])


def _out_ffn(ya, ys, ga, gb, x2d, gt1, sc2, sh2, gt2, g_post_mix, g_pre_ffn, g_post_ffn, w_out, w_up, w_down, tm):
    rows = x2d.shape[0]
    row = pl.BlockSpec((tm, D_MODEL), lambda i: (i, 0))
    vec = _resident((1, D_MODEL))
    return pl.pallas_call(
        _outffn_kernel,
        grid=(rows // tm,),
        in_specs=[row, row, row, row, row,
                  _mod_spec(gt1, tm), _mod_spec(sc2, tm), _mod_spec(sh2, tm), _mod_spec(gt2, tm),
                  vec, vec, vec,
                  _resident((D_MODEL, D_MODEL)), _resident((D_MODEL, D_FF)), _resident((D_FF, D_MODEL))],
        out_specs=row,
        out_shape=jax.ShapeDtypeStruct((rows, D_MODEL), F32),
        compiler_params=_params("parallel"),
        name="out_ffn",
    )(ya, ys, ga, gb, x2d, gt1, sc2, sh2, gt2, g_post_mix, g_pre_ffn, g_post_ffn, w_out, w_up, w_down)


def _layer(x, mod, past, wts, *, tm, tq, Q):
    b, L, _ = x.shape
    x2d = x.reshape(b * L, D_MODEL)
    sh1, sc1, gt1, sh2, sc2, gt2 = mod
    q, kf, vf, kb, vb, z, xbc, ga, gb, small, *stats = _in_proj(
        x2d, wts["g_pre_mix"], sc1, sh1, wts["w_qkv"], wts["w_ssm"], wts["w_gate"], wts["w_small"], wts["b_small"],
        tm, past is None)

    if past is None:
        assert b == 1 and tm == tq, "block statistics are per projection row block"
        y_att = _fox_prompt(q, kb, vb, _first_blocks(stats[0], L // tq), tq)
        conv_past = jnp.zeros((b, CONV_WIDTH - 1, CONV_DIM), F32)
        ssm_past = jnp.zeros((b, SSM_INNER, SSM_STATE), F32)
    else:
        k_past, v_past, logf_past, conv_past, ssm_past = past
        P = k_past.shape[1]
        lpt = jnp.swapaxes(logf_past, 1, 2)
        lnt = jnp.swapaxes(small[:, :FOX_HEADS].reshape(b, L, FOX_HEADS), 1, 2)
        to_rows = lambda a: a.astype(BF16).reshape(b, P, D_MODEL)
        y_att = _fox_cached(q, kb, vb, to_rows(k_past), to_rows(v_past), lpt, lnt, L)
        ssm_past = ssm_past.reshape(b, SSM_INNER, SSM_STATE)

    y_ssm, conv_new, ssm_new = _ssd(xbc, z, small, conv_past, ssm_past, wts["conv_w"], wts["conv_b"],
                                    wts["alog_pad"], wts["d_exp"], wts["ssm_norm_g"], b, Q)
    y = _out_ffn(y_att, y_ssm, ga, gb, x2d, gt1, sc2, sh2, gt2, wts["g_post_mix"], wts["g_pre_ffn"],
                 wts["g_post_ffn"], wts["w_out"], wts["w_up"], wts["w_down"], tm)
    return (y.reshape(b, L, D_MODEL),
            kf.reshape(b, L, FOX_HEADS, FOX_HEAD_DIM), vf.reshape(b, L, FOX_HEADS, FOX_HEAD_DIM),
            small[:, :FOX_HEADS].reshape(b, L, FOX_HEADS), conv_new,
            ssm_new.reshape(b, SSM_HEADS, SSM_HEAD_DIM, SSM_STATE))


def _prep_weights(w_ada, b_ada, g_pre_mix, g_post_mix, g_pre_ffn, g_post_ffn, w_in, b_f, conv_w, conv_b,
                  dt_bias, a_log, d_skip, ssm_norm_g, w_out, w_up, w_down):
    sizes = (D_MODEL, D_MODEL, D_MODEL, FOX_HEADS, SSM_INNER, CONV_DIM, SSM_HEADS, D_MODEL, D_MODEL)
    offs = [0]
    for s in sizes:
        offs.append(offs[-1] + s)
    piece = lambda i: w_in[:, offs[i]:offs[i + 1]]
    pad = LANES - FOX_HEADS - SSM_HEADS
    row = lambda v: v.reshape(1, -1).astype(F32)
    return {
        "w_qkv": w_in[:, offs[0]:offs[3]].astype(BF16),
        "w_ssm": w_in[:, offs[4]:offs[6]].astype(BF16),
        "w_gate": w_in[:, offs[7]:offs[9]].astype(BF16),
        "w_small": jnp.pad(jnp.concatenate([piece(3), piece(6)], axis=1), ((0, 0), (0, pad))).astype(BF16),
        "b_small": jnp.pad(jnp.concatenate([b_f, dt_bias]), (0, pad)).reshape(1, LANES).astype(F32),
        "alog_pad": jnp.pad(a_log, (DT_LANE0, LANES - DT_LANE0 - SSM_HEADS)).reshape(1, LANES).astype(F32),
        "d_exp": jnp.repeat(d_skip, SSM_HEAD_DIM).reshape(1, SSM_INNER).astype(F32),
        "g_pre_mix": row(g_pre_mix), "g_post_mix": row(g_post_mix),
        "g_pre_ffn": row(g_pre_ffn), "g_post_ffn": row(g_post_ffn),
        "conv_w": conv_w.astype(F32), "conv_b": row(conv_b), "ssm_norm_g": row(ssm_norm_g),
        "w_out": w_out.astype(BF16), "w_up": w_up.astype(BF16), "w_down": w_down.astype(BF16),
    }


def _forward(x_prompt, x_sample, c_prompt, c_sample, cache_fox_k, cache_fox_v, cache_fox_logf,
             state_ssm_conv, state_ssm, w_ada, b_ada, *layer_w, tm_prompt, tq, q_prompt):
    depth = w_ada.shape[0]
    bp, Lp, _ = x_prompt.shape
    bs, Ls, _ = x_sample.shape
    yp, ys = x_prompt, x_sample
    outs_p, outs_s = [], []
    for i in range(depth):
        wts = _prep_weights(w_ada[i], b_ada[i], *[w[i] for w in layer_w])
        mod = _ada(jnp.concatenate([c_prompt, c_sample], axis=0), w_ada[i], b_ada[i].reshape(1, -1))
        mod_p = [m for m in jnp.split(mod[:bp], N_MOD, axis=-1)]
        mod_s = [jnp.repeat(m, Ls, axis=0) for m in jnp.split(mod[bp:], N_MOD, axis=-1)]
        rp = _layer(yp, mod_p, None, wts, tm=tm_prompt, tq=tq, Q=q_prompt)
        rs = _layer(ys, mod_s, (cache_fox_k[i], cache_fox_v[i], cache_fox_logf[i], state_ssm_conv[i], state_ssm[i]),
                    wts, tm=bs * Ls, tq=None, Q=Ls)
        yp, ys = rp[0], rs[0]
        outs_p.append(rp[1:])
        outs_s.append(rs[1:])
    stack = lambda outs, j: jnp.stack([o[j] for o in outs])
    return (yp, ys) + tuple(stack(outs_p, j) for j in range(5)) + tuple(stack(outs_s, j) for j in range(5))


def kernel(x_prompt, x_sample, c_prompt, c_sample, cache_fox_k, cache_fox_v, cache_fox_logf, state_ssm_conv,
           state_ssm, w_ada, b_ada, g_pre_mix, g_post_mix, g_pre_ffn, g_post_ffn, w_in, b_f, conv_w, conv_b,
           dt_bias, a_log, d_skip, ssm_norm_g, w_out, w_up, w_down):
    assert x_prompt.shape[0] == 1, "the prompt path carries one sequence"
    L = x_prompt.shape[1]
    return _forward(x_prompt, x_sample, c_prompt, c_sample, cache_fox_k, cache_fox_v, cache_fox_logf,
                    state_ssm_conv, state_ssm, w_ada, b_ada, g_pre_mix, g_post_mix, g_pre_ffn, g_post_ffn,
                    w_in, b_f, conv_w, conv_b, dt_bias, a_log, d_skip, ssm_norm_g, w_out, w_up, w_down,
                    tm_prompt=min(512, L), tq=min(512, L), q_prompt=min(256, L))
```

```python
import functools

import jax
import jax.numpy as jnp
from jax import lax
from jax.experimental import pallas as pl
from jax.experimental.pallas import tpu as pltpu

F32 = jnp.float32
BF16 = jnp.bfloat16
HIGHEST = lax.Precision.HIGHEST

D_MODEL = 1024
FOX_HEADS = 8
FOX_HEAD_DIM = 128
FOX_SCALE = FOX_HEAD_DIM ** -0.5
SSM_HEADS = 16
SSM_HEAD_DIM = 64
SSM_GROUPS = 4
SSM_STATE = 128
SSM_INNER = SSM_HEADS * SSM_HEAD_DIM
CONV_WIDTH = 4
CONV_DIM = SSM_INNER + 2 * SSM_GROUPS * SSM_STATE
D_FF = 4 * D_MODEL
N_MOD = 6
EPS = 1e-6

LANES = 128
SUBLANES = 8
DT_LANE0 = FOX_HEADS
VMEM_LIMIT = 56 * 1024 * 1024
LOG2E = 1.4426950408889634
BIAS_PARTS = 3
PIECE_LANES = 32
CUM_ROWS = 128
ROW_TILE = 16
UNROLL = 4
SKIP_MARGIN = 160.0


_NT = (((1,), (1,)), ((), ()))
_TN = (((0,), (0,)), ((), ()))


def _params(*sem):
    return pltpu.CompilerParams(dimension_semantics=sem, vmem_limit_bytes=VMEM_LIMIT)


def _resident(shape):
    zeros = (0,) * len(shape)
    return pl.BlockSpec(shape, lambda *_: zeros, pipeline_mode=pl.Buffered(1))


def _rms(x, g):
    return x * lax.rsqrt(jnp.mean(x * x, axis=-1, keepdims=True) + EPS) * g


def _sigmoid(x):
    return 0.5 * jnp.tanh(0.5 * x) + 0.5


def _silu(x):
    h = 0.5 * x
    return h * jnp.tanh(h) + h


def _ada_kernel(c_ref, w_ref, b_ref, o_ref):
    s = _silu(c_ref[...]).astype(BF16)
    o_ref[...] = jnp.dot(s, w_ref[...].astype(BF16), preferred_element_type=F32) + b_ref[...]


def _ada(c_all, w_ada, b_ada):
    rows = c_all.shape[0]
    width = w_ada.shape[1]
    tn = 1536
    return pl.pallas_call(
        _ada_kernel,
        grid=(width // tn,),
        in_specs=[pl.BlockSpec((rows, D_MODEL), lambda j: (0, 0)),
                  pl.BlockSpec((D_MODEL, tn), lambda j: (0, j)),
                  pl.BlockSpec((1, tn), lambda j: (0, j))],
        out_specs=pl.BlockSpec((rows, tn), lambda j: (0, j)),
        out_shape=jax.ShapeDtypeStruct((rows, width), F32),
        compiler_params=_params("parallel"),
        name="ada_mod",
    )(c_all, w_ada, b_ada)


def _inproj_kernel(x_ref, g_ref, sc_ref, sh_ref, wqkv_ref, wssm_ref, wgate_ref, ws_ref, bs_ref, *rest, with_bias):
    if with_bias:
        (sel_ref, q_ref, kf_ref, vf_ref, kb_ref, vb_ref, z_ref, xbc_ref, ga_ref, gb_ref, sm_ref, stats_ref,
         carry_ref) = rest
    else:
        q_ref, kf_ref, vf_ref, kb_ref, vb_ref, z_ref, xbc_ref, ga_ref, gb_ref, sm_ref = rest
    h = _rms(x_ref[...], g_ref[...]) * (1.0 + sc_ref[...]) + sh_ref[...]
    hb = h.astype(BF16)

    def proj(w_ref, lo, width):
        return jnp.dot(hb, w_ref[:, lo:lo + width], preferred_element_type=F32)

    u = jnp.dot(hb, ws_ref[...], preferred_element_type=F32) + bs_ref[...]
    t = jnp.log1p(jnp.exp(-jnp.abs(u)))
    lane = lax.broadcasted_iota(jnp.int32, u.shape, 1)
    sm = jnp.where(lane < DT_LANE0, jnp.minimum(u, 0.0) - t,
                   jnp.where(lane < DT_LANE0 + SSM_HEADS, jnp.maximum(u, 0.0) + t, 0.0))
    sm_ref[...] = sm

    qb = (proj(wqkv_ref, 0, D_MODEL) * (FOX_SCALE * LOG2E)).astype(BF16)
    q_ref[...] = qb
    k = proj(wqkv_ref, D_MODEL, D_MODEL)
    kf_ref[...] = pltpu.einshape("t(hd)->thd", k, h=FOX_HEADS)
    if with_bias:
        @pl.when(pl.program_id(0) == 0)
        def _():
            carry_ref[...] = jnp.zeros_like(carry_ref)

        r = lax.broadcasted_iota(jnp.int32, (CUM_ROWS, CUM_ROWS), 0)
        c = lax.broadcasted_iota(jnp.int32, (CUM_ROWS, CUM_ROWS), 1)
        tri = (r >= c).astype(F32)
        run = carry_ref[0:1, :]
        parts = []
        for b in range(sm.shape[0] // CUM_ROWS):
            cs = jnp.dot(tri, sm[b * CUM_ROWS:(b + 1) * CUM_ROWS, :], precision=HIGHEST,
                         preferred_element_type=F32) + run
            run = cs[CUM_ROWS - 1:CUM_ROWS, :]
            parts.append(cs)
        carry_ref[0:1, :] = run
        nb = jnp.concatenate(parts, axis=0) * (-LOG2E)
        hi = nb.astype(BF16)
        r1 = nb - hi.astype(F32)
        mid = r1.astype(BF16)
        lo = (r1 - mid.astype(F32)).astype(BF16)
        aug = jnp.dot(jnp.concatenate([hi, mid, lo], axis=1), sel_ref[...], preferred_element_type=F32)
        kb = k.astype(BF16)
        for hd in range(FOX_HEADS):
            sl = slice(hd * FOX_HEAD_DIM, (hd + 1) * FOX_HEAD_DIM)
            kb_ref[:, 2 * hd * FOX_HEAD_DIM:(2 * hd + 1) * FOX_HEAD_DIM] = kb[:, sl]
            kb_ref[:, (2 * hd + 1) * FOX_HEAD_DIM:(2 * hd + 2) * FOX_HEAD_DIM] = aug[:, sl].astype(BF16)

        def max_sq_norms(xb):
            x2 = xb.astype(F32)
            x2 = x2 * x2
            out = jnp.zeros((1, LANES), F32)
            for hd in range(FOX_HEADS):
                n2 = jnp.sum(x2[:, hd * FOX_HEAD_DIM:(hd + 1) * FOX_HEAD_DIM], axis=-1, keepdims=True)
                out = jnp.where(lane[0:1, :] == hd, jnp.max(n2, axis=0, keepdims=True), out)
            return out

        stats_ref[...] = jnp.zeros(stats_ref.shape, F32)
        stats_ref[0:1, :] = max_sq_norms(qb)
        stats_ref[1:2, :] = max_sq_norms(kb)
        stats_ref[2:3, :] = nb[0:1, :]
        stats_ref[3:4, :] = nb[nb.shape[0] - 1:, :]
    else:
        kb_ref[...] = k.astype(BF16)
    v = proj(wqkv_ref, 2 * D_MODEL, D_MODEL)
    vf_ref[...] = pltpu.einshape("t(hd)->thd", v, h=FOX_HEADS)
    vb_ref[...] = v.astype(BF16)
    z_ref[...] = proj(wssm_ref, 0, SSM_INNER).astype(BF16)
    xbc_ref[...] = proj(wssm_ref, SSM_INNER, CONV_DIM)
    ga_ref[...] = proj(wgate_ref, 0, D_MODEL).astype(BF16)
    gb_ref[...] = proj(wgate_ref, D_MODEL, D_MODEL).astype(BF16)


def _mod_spec(arr, tm):
    if arr.shape[0] == 1:
        return pl.BlockSpec((1, D_MODEL), lambda i: (0, 0))
    return pl.BlockSpec((tm, D_MODEL), lambda i: (i, 0))


def _bias_selector():
    r = lax.broadcasted_iota(jnp.int32, (BIAS_PARTS * LANES, FOX_HEADS * FOX_HEAD_DIM), 0)
    c = lax.broadcasted_iota(jnp.int32, (BIAS_PARTS * LANES, FOX_HEADS * FOX_HEAD_DIM), 1)
    hit = (r % LANES < FOX_HEADS) & (c == (r % LANES) * FOX_HEAD_DIM + r // LANES)
    return hit.astype(BF16)


def _in_proj(x2d, g, sc, sh, w_qkv, w_ssm, w_gate, w_small, b_small, tm, with_bias):
    rows = x2d.shape[0]
    assert not with_bias or tm % CUM_ROWS == 0
    row = lambda w: pl.BlockSpec((tm, w), lambda i: (i, 0))
    shp = lambda w, dt: jax.ShapeDtypeStruct((rows, w), dt)
    kb_width = 2 * D_MODEL if with_bias else D_MODEL
    in_specs = [row(D_MODEL), _resident((1, D_MODEL)), _mod_spec(sc, tm), _mod_spec(sh, tm),
                _resident(w_qkv.shape), _resident(w_ssm.shape), _resident(w_gate.shape),
                _resident((D_MODEL, LANES)), _resident((1, LANES))]
    args = [x2d, g, sc, sh, w_qkv, w_ssm, w_gate, w_small, b_small]
    heads = pl.BlockSpec((tm, FOX_HEADS, FOX_HEAD_DIM), lambda i: (i, 0, 0))
    heads_shape = jax.ShapeDtypeStruct((rows, FOX_HEADS, FOX_HEAD_DIM), F32)
    out_specs = [row(D_MODEL), heads, heads, row(kb_width), row(D_MODEL),
                 row(SSM_INNER), row(CONV_DIM), row(D_MODEL), row(D_MODEL), row(LANES)]
    out_shape = [shp(D_MODEL, BF16), heads_shape, heads_shape, shp(kb_width, BF16),
                 shp(D_MODEL, BF16), shp(SSM_INNER, BF16), shp(CONV_DIM, F32), shp(D_MODEL, BF16),
                 shp(D_MODEL, BF16), shp(LANES, F32)]
    if with_bias:
        in_specs.append(_resident((BIAS_PARTS * LANES, FOX_HEADS * FOX_HEAD_DIM)))
        args.append(_bias_selector())
        out_specs.append(pl.BlockSpec((SUBLANES, LANES), lambda i: (i, 0)))
        out_shape.append(jax.ShapeDtypeStruct((rows // tm * SUBLANES, LANES), F32))
    return pl.pallas_call(
        functools.partial(_inproj_kernel, with_bias=with_bias),
        grid=(rows // tm,),
        in_specs=in_specs,
        out_specs=out_specs,
        out_shape=out_shape,
        scratch_shapes=[pltpu.VMEM((SUBLANES, LANES), F32)] if with_bias else [],
        compiler_params=_params("arbitrary" if with_bias else "parallel"),
        name="in_proj",
    )(*args)


def _flash_kernel(first_ref, q_ref, k_ref, v_ref, o_ref, s0_ref, s1_ref, p0_ref, p1_ref, m_ref, l_ref, alpha0_ref,
                  alpha1_ref, acc_ref, *, tq, tk):
    i = pl.program_id(1)
    nlt = tk // LANES
    lane = lax.broadcasted_iota(jnp.int32, (tq, FOX_HEAD_DIM), 1)
    ones = jnp.where(lane < BIAS_PARTS, 1.0, 0.0).astype(BF16)
    q2 = jnp.concatenate([q_ref[...], ones], axis=1)

    def scores(j, s_ref):
        off = pl.multiple_of(j * tk, tk)
        s_ref[...] = lax.dot_general(q2, k_ref[pl.ds(off, tk), :], _NT, preferred_element_type=F32)

    def softmax(s_ref, p_ref, alpha_ref, mask_shift):
        for rt in range(tq // ROW_TILE):
            rows = slice(rt * ROW_TILE, (rt + 1) * ROW_TILE)
            cols = [s_ref[rows, c * LANES:(c + 1) * LANES] for c in range(nlt)]
            if mask_shift is not None:
                ri = lax.broadcasted_iota(jnp.int32, (ROW_TILE, LANES), 0) + rt * ROW_TILE
                ci = lax.broadcasted_iota(jnp.int32, (ROW_TILE, LANES), 1) + mask_shift
                cols = [jnp.where(ci + c * LANES <= ri, cols[c], -jnp.inf) for c in range(nlt)]
            mx = functools.reduce(jnp.maximum, cols)
            m_old = m_ref[rows, :]
            m_new = jnp.maximum(m_old, jnp.max(mx, axis=-1, keepdims=True))
            alpha = jnp.exp2(m_old - m_new)
            ps = [jnp.exp2(cc - m_new) for cc in cols]
            l_ref[rows, :] = alpha * l_ref[rows, :] + jnp.sum(functools.reduce(jnp.add, ps), axis=-1, keepdims=True)
            m_ref[rows, :] = m_new
            alpha_ref[rows, :] = alpha
            for c in range(nlt):
                p_ref[rows, c * LANES:(c + 1) * LANES] = ps[c].astype(BF16)

    def weighted_values(j, p_ref, alpha_ref):
        off = pl.multiple_of(j * tk, tk)
        acc_ref[...] = alpha_ref[...] * acc_ref[...] + jnp.dot(p_ref[...], v_ref[pl.ds(off, tk), :],
                                                               preferred_element_type=F32)

    m_ref[...] = jnp.full(m_ref.shape, -jnp.inf, F32)
    l_ref[...] = jnp.zeros(l_ref.shape, F32)
    acc_ref[...] = jnp.zeros(acc_ref.shape, F32)
    bufs = ((s0_ref, p0_ref, alpha0_ref), (s1_ref, p1_ref, alpha1_ref))

    def run(first, n, diagonal_last, next_scores):
        for k in range(n):
            s_ref, p_ref, alpha_ref = bufs[k % 2]
            s_nxt, p_prv, alpha_prv = bufs[(k + 1) % 2]
            weighted_values(jnp.maximum(first + k - 1, 0), p_prv, alpha_prv)
            if k + 1 < n or next_scores:
                scores(first + k + 1, s_nxt)
            softmax(s_ref, p_ref, alpha_ref, 0 if (diagonal_last and k == n - 1) else None)

    p1_ref[...] = jnp.zeros(p1_ref.shape, BF16)
    alpha1_ref[...] = jnp.ones(alpha1_ref.shape, F32)
    j0 = first_ref[pl.program_id(0), i]
    n_full = i - j0
    scores(j0, s0_ref)

    @pl.loop(0, n_full // UNROLL)
    def _(t):
        run(j0 + UNROLL * t, UNROLL, False, True)

    for rem in range(UNROLL):
        @pl.when(n_full % UNROLL == rem)
        def _():
            run(i - rem, rem + 1, True, False)
            weighted_values(i, *bufs[rem % 2][1:])

    o_ref[...] = (acc_ref[...] / l_ref[...]).astype(o_ref.dtype)


def _first_blocks(stats, nq):
    st = stats.reshape(nq, SUBLANES, LANES)
    qn = jnp.sqrt(st[:, 0, :FOX_HEADS])
    kn = jnp.sqrt(st[:, 1, :FOX_HEADS])
    b_first, b_last = st[:, 2, :FOX_HEADS], st[:, 3, :FOX_HEADS]
    bound = qn[:, None, :] * (kn[None, :, :] + kn[:, None, :]) + b_last[None, :, :] - b_first[:, None, :]
    blocks = jnp.arange(nq, dtype=jnp.int32)
    first_needed = jnp.min(jnp.where(bound <= -SKIP_MARGIN, nq, blocks[None, :, None]), axis=1)
    return jnp.minimum(first_needed, blocks[:, None]).T


def _fox_prompt(q, kp, v, first, tq):
    L = q.shape[0]
    tk = tq
    return pl.pallas_call(
        functools.partial(_flash_kernel, tq=tq, tk=tk),
        grid=(FOX_HEADS, L // tq),
        in_specs=[pl.BlockSpec(memory_space=pltpu.SMEM),
                  pl.BlockSpec((tq, FOX_HEAD_DIM), lambda h, i: (i, h)),
                  pl.BlockSpec((L, 2 * FOX_HEAD_DIM), lambda h, i: (0, h)),
                  pl.BlockSpec((L, FOX_HEAD_DIM), lambda h, i: (0, h))],
        out_specs=pl.BlockSpec((tq, FOX_HEAD_DIM), lambda h, i: (i, h)),
        out_shape=jax.ShapeDtypeStruct((L, FOX_HEADS * FOX_HEAD_DIM), BF16),
        scratch_shapes=[pltpu.VMEM((tq, tk), F32)] * 2 + [pltpu.VMEM((tq, tk), BF16)] * 2
        + [pltpu.VMEM((tq, FOX_HEAD_DIM), F32)] * 5,
        compiler_params=_params("parallel", "arbitrary"),
        name="fox_flash",
    )(first, q, kp, v)


def _cached_attn_kernel(q_ref, kn_ref, vn_ref, kc_ref, vc_ref, lpt_ref, lnt_ref, o_ref):
    n = q_ref.shape[0]
    P = kc_ref.shape[0]
    r = lax.broadcasted_iota(jnp.int32, (P, P), 0)
    c = lax.broadcasted_iota(jnp.int32, (P, P), 1)
    G = jnp.dot(lpt_ref[...], (r > c).astype(F32), precision=HIGHEST, preferred_element_type=F32) * LOG2E
    rn = lax.broadcasted_iota(jnp.int32, (n, n), 0)
    cn = lax.broadcasted_iota(jnp.int32, (n, n), 1)
    Hn = jnp.dot(lnt_ref[...], (rn <= cn).astype(F32), precision=HIGHEST, preferred_element_type=F32) * LOG2E
    causal = cn <= rn
    kt = pltpu.einshape("shd->hsd", kc_ref[...])
    vt = pltpu.einshape("shd->hsd", vc_ref[...])
    for h in range(FOX_HEADS):
        sl = slice(h * FOX_HEAD_DIM, (h + 1) * FOX_HEAD_DIM)
        qh = q_ref[:, sl]
        sp = lax.dot_general(qh, kt[h].astype(BF16), _NT, preferred_element_type=F32) + G[h:h + 1, :]
        sn = lax.dot_general(qh, kn_ref[:, sl], _NT, preferred_element_type=F32) - Hn[h:h + 1, :]
        sn = jnp.where(causal, sn, -jnp.inf)
        m = jnp.maximum(jnp.max(sp, axis=-1, keepdims=True), jnp.max(sn, axis=-1, keepdims=True))
        pp = jnp.exp2(sp - m)
        pn = jnp.exp2(sn - m)
        l = jnp.sum(pp, axis=-1, keepdims=True) + jnp.sum(pn, axis=-1, keepdims=True)
        o = (jnp.dot(pp.astype(BF16), vt[h].astype(BF16), preferred_element_type=F32)
             + jnp.dot(pn.astype(BF16), vn_ref[:, sl], preferred_element_type=F32))
        o_ref[:, sl] = (o / l).astype(o_ref.dtype)


def _fox_cached(q, kn, vn, kc, vc, lpt, lnt, n):
    B, P = kc.shape[:2]
    W = FOX_HEADS * FOX_HEAD_DIM
    row = pl.BlockSpec((n, W), lambda b: (b, 0))
    cache = pl.BlockSpec((None, P, FOX_HEADS, FOX_HEAD_DIM), lambda b: (b, 0, 0, 0))
    return pl.pallas_call(
        _cached_attn_kernel,
        grid=(B,),
        in_specs=[row, row, row, cache, cache,
                  pl.BlockSpec((None, FOX_HEADS, P), lambda b: (b, 0, 0)),
                  pl.BlockSpec((None, FOX_HEADS, n), lambda b: (b, 0, 0))],
        out_specs=row,
        out_shape=jax.ShapeDtypeStruct((B * n, W), BF16),
        compiler_params=_params("parallel"),
        name="fox_cached",
    )(q, kn, vn, kc, vc, lpt, lnt)


def _ssd_kernel(xbc_ref, z_ref, sm_ref, convp_ref, ssmp_ref, cw_ref, cb_ref, alog_ref, dexp_ref, ng_ref,
                expand_ref, y_ref, convn_ref, ssmn_ref, tail_ref, st_ref, *, Q, nchunks):
    ci = pl.program_id(1)
    PAIR = 2 * SSM_HEAD_DIM
    GW = SSM_STATE

    @pl.when(ci == 0)
    def _():
        st_ref[...] = ssmp_ref[...]
        tail_ref[...] = jnp.zeros((SUBLANES, CONV_DIM), F32)
        tail_ref[SUBLANES - (CONV_WIDTH - 1):SUBLANES, :] = convp_ref[...]

    x3 = xbc_ref[...].reshape(Q // SUBLANES, SUBLANES, CONV_DIM)
    tail = tail_ref[...].reshape(1, SUBLANES, CONV_DIM)
    sub = lax.broadcasted_iota(jnp.int32, x3.shape, 1)
    conv = cb_ref[...]
    for d in range(CONV_WIDTH - 1, 0, -1):
        rot = pltpu.roll(x3, d, 1)
        prev = jnp.concatenate([pltpu.roll(tail, d, 1), rot[:-1]], axis=0)
        conv = conv + jnp.where(sub < d, prev, rot) * cw_ref[CONV_WIDTH - 1 - d:CONV_WIDTH - d, :]
    conv = (conv + x3 * cw_ref[CONV_WIDTH - 1:CONV_WIDTH, :]).reshape(Q, CONV_DIM)
    tail_ref[...] = xbc_ref[Q - SUBLANES:Q, :]
    convn_ref[...] = xbc_ref[Q - (CONV_WIDTH - 1):Q, :]

    xc = _silu(conv)
    xs = xc[:, :SSM_INNER]
    Bm = xc[:, SSM_INNER:SSM_INNER + SSM_GROUPS * GW]
    Cm = xc[:, SSM_INNER + SSM_GROUPS * GW:]

    lane1 = lax.broadcasted_iota(jnp.int32, (Q, LANES), 1)
    head_lanes = (lane1 >= DT_LANE0) & (lane1 < DT_LANE0 + SSM_HEADS)
    dt = jnp.where(head_lanes, sm_ref[...], 0.0)
    a = dt * (-jnp.exp(alog_ref[...]))
    r = lax.broadcasted_iota(jnp.int32, (Q, Q), 0)
    c = lax.broadcasted_iota(jnp.int32, (Q, Q), 1)
    causal = r >= c
    a_cs = jnp.dot(causal.astype(F32), a, precision=HIGHEST, preferred_element_type=F32)
    er = lax.broadcasted_iota(jnp.int32, (SSM_HEADS, LANES), 0)
    ec = lax.broadcasted_iota(jnp.int32, (SSM_HEADS, LANES), 1)
    pick = (ec == er + DT_LANE0).astype(F32)
    acs_t = lax.dot_general(pick, a_cs, _NT, precision=HIGHEST, preferred_element_type=F32)
    a_last = a_cs[Q - 1:Q, :]
    e_last = jnp.exp(a_last)

    def per_channel(x):
        x = jnp.where(head_lanes, x, 0.0)
        hi = x.astype(BF16).astype(F32)
        r1 = x - hi
        mid = r1.astype(BF16).astype(F32)
        packed = hi + pltpu.roll(mid, PIECE_LANES, 1) + pltpu.roll(r1 - mid, 2 * PIECE_LANES, 1)
        return jnp.dot(packed.astype(BF16), expand_ref[...], preferred_element_type=F32)

    xd = xs * per_channel(dt)
    xdb = xd.astype(BF16)
    xw = xd * per_channel(jnp.exp(a_last - a_cs))
    e_cs = per_channel(jnp.exp(a_cs))
    Bb = Bm.astype(BF16)
    Cb = Cm.astype(BF16)

    lane_lo = lax.broadcasted_iota(jnp.int32, (Q, PAIR), 1) < SSM_HEAD_DIM
    row_lo = lax.broadcasted_iota(jnp.int32, (PAIR, GW), 0) < SSM_HEAD_DIM

    y_parts = []
    for g in range(SSM_GROUPS):
        Bg = Bb[:, g * GW:(g + 1) * GW]
        Cg = Cb[:, g * GW:(g + 1) * GW]
        cbm = lax.dot_general(Cg, Bg, _NT, preferred_element_type=F32)
        for pr in range(SSM_HEADS // SSM_GROUPS // 2):
            pair = g * (SSM_HEADS // SSM_GROUPS // 2) + pr
            sl = slice(pair * PAIR, (pair + 1) * PAIR)
            s_prev = st_ref[sl, :]
            ys = []
            for hh in (2 * pair, 2 * pair + 1):
                ln = DT_LANE0 + hh
                lm = jnp.exp(jnp.where(causal, a_cs[:, ln:ln + 1] - acs_t[hh:hh + 1, :], -jnp.inf))
                ys.append(jnp.dot((cbm * lm).astype(BF16), xdb[:, sl], preferred_element_type=F32))
            y_off = lax.dot_general(Cg, s_prev.astype(BF16), _NT, preferred_element_type=F32) * e_cs[:, sl]
            ln0 = DT_LANE0 + 2 * pair
            dec = jnp.where(row_lo, e_last[:, ln0:ln0 + 1], e_last[:, ln0 + 1:ln0 + 2])
            st_ref[sl, :] = s_prev * dec + lax.dot_general(xw[:, sl], Bm[:, g * GW:(g + 1) * GW], _TN,
                                                           preferred_element_type=F32)
            y_parts.append(jnp.where(lane_lo, ys[0], ys[1]) + y_off + xs[:, sl] * dexp_ref[:, sl])

    y = jnp.concatenate(y_parts, axis=1) * _silu(z_ref[...].astype(F32))
    gw = SSM_INNER // SSM_GROUPS
    normed = []
    for g in range(SSM_GROUPS):
        yg = y[:, g * gw:(g + 1) * gw]
        normed.append(yg * lax.rsqrt(jnp.mean(yg * yg, axis=-1, keepdims=True) + EPS))
    y_ref[...] = (jnp.concatenate(normed, axis=1) * ng_ref[...]).astype(y_ref.dtype)

    @pl.when(ci == nchunks - 1)
    def _():
        ssmn_ref[...] = st_ref[...]


def _head_expander():
    k = lax.broadcasted_iota(jnp.int32, (LANES, SSM_INNER), 0)
    j = lax.broadcasted_iota(jnp.int32, (LANES, SSM_INNER), 1)
    hit = (k < BIAS_PARTS * PIECE_LANES) & (k % PIECE_LANES == DT_LANE0 + j // SSM_HEAD_DIM)
    return hit.astype(BF16)


def _ssd(xbc, z, small, conv_past, ssm_past, conv_w, conv_b, alog_pad, d_exp, norm_g, batch, Q):
    rows = xbc.shape[0]
    nchunks = rows // batch // Q
    assert Q % SUBLANES == 0 and Q >= SUBLANES and nchunks * Q * batch == rows
    row = lambda w: pl.BlockSpec((Q, w), lambda b, c: (b * nchunks + c, 0))
    state_rows = SSM_HEADS * SSM_HEAD_DIM
    per_seq = lambda a, b_: pl.BlockSpec((None, a, b_), lambda b, c: (b, 0, 0))
    return pl.pallas_call(
        functools.partial(_ssd_kernel, Q=Q, nchunks=nchunks),
        grid=(batch, nchunks),
        in_specs=[row(CONV_DIM), row(SSM_INNER), row(LANES),
                  per_seq(CONV_WIDTH - 1, CONV_DIM), per_seq(state_rows, SSM_STATE),
                  _resident((CONV_WIDTH, CONV_DIM)), _resident((1, CONV_DIM)), _resident((1, LANES)),
                  _resident((1, SSM_INNER)), _resident((1, SSM_INNER)), _resident((LANES, SSM_INNER))],
        out_specs=[row(SSM_INNER), per_seq(CONV_WIDTH - 1, CONV_DIM), per_seq(state_rows, SSM_STATE)],
        out_shape=[jax.ShapeDtypeStruct((rows, SSM_INNER), BF16),
                   jax.ShapeDtypeStruct((batch, CONV_WIDTH - 1, CONV_DIM), F32),
                   jax.ShapeDtypeStruct((batch, state_rows, SSM_STATE), F32)],
        scratch_shapes=[pltpu.VMEM((SUBLANES, CONV_DIM), F32), pltpu.VMEM((state_rows, SSM_STATE), F32)],
        compiler_params=_params("parallel", "arbitrary"),
        name="ssd_scan",
    )(xbc, z, small, conv_past, ssm_past, conv_w, conv_b, alog_pad, d_exp, norm_g, _head_expander())


def _outffn_kernel(ya_ref, ys_ref, ga_ref, gb_ref, x_ref, gt1_ref, sc2_ref, sh2_ref, gt2_ref,
                   gpm_ref, gpf_ref, gqf_ref, wo_ref, wu_ref, wd_ref, o_ref):
    merged = (_sigmoid(ga_ref[...].astype(F32)) * ya_ref[...].astype(F32)
              + _sigmoid(gb_ref[...].astype(F32)) * ys_ref[...].astype(F32))
    m = jnp.dot(merged.astype(BF16), wo_ref[...], preferred_element_type=F32)
    x1 = x_ref[...] + gt1_ref[...] * _rms(m, gpm_ref[...])
    hb = (_rms(x1, gpf_ref[...]) * (1.0 + sc2_ref[...]) + sh2_ref[...]).astype(BF16)
    f = jnp.zeros(x1.shape, F32)
    for cc in range(D_FF // D_MODEL):
        sl = slice(cc * D_MODEL, (cc + 1) * D_MODEL)
        up = jnp.dot(hb, wu_ref[:, sl], preferred_element_type=F32)
        act = jnp.square(jnp.maximum(up, 0.0)).astype(BF16)
        f = f + jnp.dot(act, wd_ref[sl, :], preferred_element_type=F32)
    o_ref[...] = x1 + gt2_ref[...] * _rms(f, gqf_ref[...])


def _out_ffn(ya, ys, ga, gb, x2d, gt1, sc2, sh2, gt2, g_post_mix, g_pre_ffn, g_post_ffn, w_out, w_up, w_down, tm):
    rows = x2d.shape[0]
    row = pl.BlockSpec((tm, D_MODEL), lambda i: (i, 0))
    vec = _resident((1, D_MODEL))
    return pl.pallas_call(
        _outffn_kernel,
        grid=(rows // tm,),
        in_specs=[row, row, row, row, row,
                  _mod_spec(gt1, tm), _mod_spec(sc2, tm), _mod_spec(sh2, tm), _mod_spec(gt2, tm),
                  vec, vec, vec,
                  _resident((D_MODEL, D_MODEL)), _resident((D_MODEL, D_FF)), _resident((D_FF, D_MODEL))],
        out_specs=row,
        out_shape=jax.ShapeDtypeStruct((rows, D_MODEL), F32),
        compiler_params=_params("parallel"),
        name="out_ffn",
    )(ya, ys, ga, gb, x2d, gt1, sc2, sh2, gt2, g_post_mix, g_pre_ffn, g_post_ffn, w_out, w_up, w_down)


def _layer(x, mod, past, wts, *, tm, tq, Q):
    b, L, _ = x.shape
    x2d = x.reshape(b * L, D_MODEL)
    sh1, sc1, gt1, sh2, sc2, gt2 = mod
    q, kf, vf, kb, vb, z, xbc, ga, gb, small, *stats = _in_proj(
        x2d, wts["g_pre_mix"], sc1, sh1, wts["w_qkv"], wts["w_ssm"], wts["w_gate"], wts["w_small"], wts["b_small"],
        tm, past is None)

    if past is None:
        assert b == 1 and tm == tq, "block statistics are per projection row block"
        y_att = _fox_prompt(q, kb, vb, _first_blocks(stats[0], L // tq), tq)
        conv_past = jnp.zeros((b, CONV_WIDTH - 1, CONV_DIM), F32)
        ssm_past = jnp.zeros((b, SSM_INNER, SSM_STATE), F32)
    else:
        k_past, v_past, logf_past, conv_past, ssm_past = past
        P = k_past.shape[1]
        lpt = jnp.swapaxes(logf_past, 1, 2)
        lnt = jnp.swapaxes(small[:, :FOX_HEADS].reshape(b, L, FOX_HEADS), 1, 2)
        y_att = _fox_cached(q, kb, vb, k_past, v_past, lpt, lnt, L)
        ssm_past = ssm_past.reshape(b, SSM_INNER, SSM_STATE)

    y_ssm, conv_new, ssm_new = _ssd(xbc, z, small, conv_past, ssm_past, wts["conv_w"], wts["conv_b"],
                                    wts["alog_pad"], wts["d_exp"], wts["ssm_norm_g"], b, Q)
    y = _out_ffn(y_att, y_ssm, ga, gb, x2d, gt1, sc2, sh2, gt2, wts["g_post_mix"], wts["g_pre_ffn"],
                 wts["g_post_ffn"], wts["w_out"], wts["w_up"], wts["w_down"], tm)
    return (y.reshape(b, L, D_MODEL),
            kf.reshape(b, L, FOX_HEADS, FOX_HEAD_DIM), vf.reshape(b, L, FOX_HEADS, FOX_HEAD_DIM),
            small[:, :FOX_HEADS].reshape(b, L, FOX_HEADS), conv_new,
            ssm_new.reshape(b, SSM_HEADS, SSM_HEAD_DIM, SSM_STATE))


def _prep_weights(w_ada, b_ada, g_pre_mix, g_post_mix, g_pre_ffn, g_post_ffn, w_in, b_f, conv_w, conv_b,
                  dt_bias, a_log, d_skip, ssm_norm_g, w_out, w_up, w_down):
    sizes = (D_MODEL, D_MODEL, D_MODEL, FOX_HEADS, SSM_INNER, CONV_DIM, SSM_HEADS, D_MODEL, D_MODEL)
    offs = [0]
    for s in sizes:
        offs.append(offs[-1] + s)
    piece = lambda i: w_in[:, offs[i]:offs[i + 1]]
    pad = LANES - FOX_HEADS - SSM_HEADS
    row = lambda v: v.reshape(1, -1).astype(F32)
    return {
        "w_qkv": w_in[:, offs[0]:offs[3]].astype(BF16),
        "w_ssm": w_in[:, offs[4]:offs[6]].astype(BF16),
        "w_gate": w_in[:, offs[7]:offs[9]].astype(BF16),
        "w_small": jnp.pad(jnp.concatenate([piece(3), piece(6)], axis=1), ((0, 0), (0, pad))).astype(BF16),
        "b_small": jnp.pad(jnp.concatenate([b_f, dt_bias]), (0, pad)).reshape(1, LANES).astype(F32),
        "alog_pad": jnp.pad(a_log, (DT_LANE0, LANES - DT_LANE0 - SSM_HEADS)).reshape(1, LANES).astype(F32),
        "d_exp": jnp.repeat(d_skip, SSM_HEAD_DIM).reshape(1, SSM_INNER).astype(F32),
        "g_pre_mix": row(g_pre_mix), "g_post_mix": row(g_post_mix),
        "g_pre_ffn": row(g_pre_ffn), "g_post_ffn": row(g_post_ffn),
        "conv_w": conv_w.astype(F32), "conv_b": row(conv_b), "ssm_norm_g": row(ssm_norm_g),
        "w_out": w_out.astype(BF16), "w_up": w_up.astype(BF16), "w_down": w_down.astype(BF16),
    }


def _forward(x_prompt, x_sample, c_prompt, c_sample, cache_fox_k, cache_fox_v, cache_fox_logf,
             state_ssm_conv, state_ssm, w_ada, b_ada, *layer_w, tm_prompt, tq, q_prompt):
    depth = w_ada.shape[0]
    bp, Lp, _ = x_prompt.shape
    bs, Ls, _ = x_sample.shape
    yp, ys = x_prompt, x_sample
    outs_p, outs_s = [], []
    for i in range(depth):
        wts = _prep_weights(w_ada[i], b_ada[i], *[w[i] for w in layer_w])
        mod = _ada(jnp.concatenate([c_prompt, c_sample], axis=0), w_ada[i], b_ada[i].reshape(1, -1))
        mod_p = [m for m in jnp.split(mod[:bp], N_MOD, axis=-1)]
        mod_s = [jnp.repeat(m, Ls, axis=0) for m in jnp.split(mod[bp:], N_MOD, axis=-1)]
        rp = _layer(yp, mod_p, None, wts, tm=tm_prompt, tq=tq, Q=q_prompt)
        rs = _layer(ys, mod_s, (cache_fox_k[i], cache_fox_v[i], cache_fox_logf[i], state_ssm_conv[i], state_ssm[i]),
                    wts, tm=bs * Ls, tq=None, Q=Ls)
        yp, ys = rp[0], rs[0]
        outs_p.append(rp[1:])
        outs_s.append(rs[1:])
    stack = lambda outs, j: jnp.stack([o[j] for o in outs])
    return (yp, ys) + tuple(stack(outs_p, j) for j in range(5)) + tuple(stack(outs_s, j) for j in range(5))


def kernel(x_prompt, x_sample, c_prompt, c_sample, cache_fox_k, cache_fox_v, cache_fox_logf, state_ssm_conv,
           state_ssm, w_ada, b_ada, g_pre_mix, g_post_mix, g_pre_ffn, g_post_ffn, w_in, b_f, conv_w, conv_b,
           dt_bias, a_log, d_skip, ssm_norm_g, w_out, w_up, w_down):
    assert x_prompt.shape[0] == 1, "the prompt path carries one sequence"
    L = x_prompt.shape[1]
    return _forward(x_prompt, x_sample, c_prompt, c_sample, cache_fox_k, cache_fox_v, cache_fox_logf,
                    state_ssm_conv, state_ssm, w_ada, b_ada, g_pre_mix, g_post_mix, g_pre_ffn, g_post_ffn,
                    w_in, b_f, conv_w, conv_b, dt_bias, a_log, d_skip, ssm_norm_g, w_out, w_up, w_down,
                    tm_prompt=min(512, L), tq=min(512, L), q_prompt=min(256, L))
```

```python
import functools

import jax
import jax.numpy as jnp
from jax import lax
from jax.experimental import pallas as pl
from jax.experimental.pallas import tpu as pltpu

F32 = jnp.float32
BF16 = jnp.bfloat16
HIGHEST = lax.Precision.HIGHEST

D_MODEL = 1024
FOX_HEADS = 8
FOX_HEAD_DIM = 128
FOX_SCALE = FOX_HEAD_DIM ** -0.5
SSM_HEADS = 16
SSM_HEAD_DIM = 64
SSM_GROUPS = 4
SSM_STATE = 128
SSM_INNER = SSM_HEADS * SSM_HEAD_DIM
CONV_WIDTH = 4
CONV_DIM = SSM_INNER + 2 * SSM_GROUPS * SSM_STATE
D_FF = 4 * D_MODEL
N_MOD = 6
EPS = 1e-6

LANES = 128
SUBLANES = 8
DT_LANE0 = FOX_HEADS
VMEM_LIMIT = 56 * 1024 * 1024
LOG2E = 1.4426950408889634
BIAS_PARTS = 3
PIECE_LANES = 32
CUM_ROWS = 128
ROW_TILE = 16
UNROLL = 4
SKIP_MARGIN = 152.0


_NT = (((1,), (1,)), ((), ()))
_TN = (((0,), (0,)), ((), ()))


def _params(*sem):
    return pltpu.CompilerParams(dimension_semantics=sem, vmem_limit_bytes=VMEM_LIMIT)


def _resident(shape):
    zeros = (0,) * len(shape)
    return pl.BlockSpec(shape, lambda *_: zeros, pipeline_mode=pl.Buffered(1))


def _rms(x, g):
    return x * lax.rsqrt(jnp.mean(x * x, axis=-1, keepdims=True) + EPS) * g


def _sigmoid(x):
    return 0.5 * jnp.tanh(0.5 * x) + 0.5


def _silu(x):
    h = 0.5 * x
    return h * jnp.tanh(h) + h


def _ada_kernel(c_ref, w_ref, b_ref, o_ref):
    s = _silu(c_ref[...]).astype(BF16)
    o_ref[...] = jnp.dot(s, w_ref[...].astype(BF16), preferred_element_type=F32) + b_ref[...]


def _ada(c_all, w_ada, b_ada):
    rows = c_all.shape[0]
    width = w_ada.shape[1]
    tn = 1536
    return pl.pallas_call(
        _ada_kernel,
        grid=(width // tn,),
        in_specs=[pl.BlockSpec((rows, D_MODEL), lambda j: (0, 0)),
                  pl.BlockSpec((D_MODEL, tn), lambda j: (0, j)),
                  pl.BlockSpec((1, tn), lambda j: (0, j))],
        out_specs=pl.BlockSpec((rows, tn), lambda j: (0, j)),
        out_shape=jax.ShapeDtypeStruct((rows, width), F32),
        compiler_params=_params("parallel"),
        name="ada_mod",
    )(c_all, w_ada, b_ada)


def _inproj_kernel(x_ref, g_ref, sc_ref, sh_ref, wqkv_ref, wssm_ref, wgate_ref, ws_ref, bs_ref, *rest, with_bias):
    if with_bias:
        (sel_ref, q_ref, kf_ref, vf_ref, kb_ref, vb_ref, z_ref, xbc_ref, ga_ref, gb_ref, sm_ref, stats_ref,
         carry_ref) = rest
    else:
        q_ref, kf_ref, vf_ref, kb_ref, vb_ref, z_ref, xbc_ref, ga_ref, gb_ref, sm_ref = rest
    h = _rms(x_ref[...], g_ref[...]) * (1.0 + sc_ref[...]) + sh_ref[...]
    hb = h.astype(BF16)

    def proj(w_ref, lo, width):
        return jnp.dot(hb, w_ref[:, lo:lo + width], preferred_element_type=F32)

    u = jnp.dot(hb, ws_ref[...], preferred_element_type=F32) + bs_ref[...]
    t = jnp.log1p(jnp.exp(-jnp.abs(u)))
    lane = lax.broadcasted_iota(jnp.int32, u.shape, 1)
    sm = jnp.where(lane < DT_LANE0, jnp.minimum(u, 0.0) - t,
                   jnp.where(lane < DT_LANE0 + SSM_HEADS, jnp.maximum(u, 0.0) + t, 0.0))
    sm_ref[...] = sm

    qb = (proj(wqkv_ref, 0, D_MODEL) * (FOX_SCALE * LOG2E)).astype(BF16)
    q_ref[...] = qb
    k = proj(wqkv_ref, D_MODEL, D_MODEL)
    kf_ref[...] = k.reshape(k.shape[0], FOX_HEADS, FOX_HEAD_DIM)
    if with_bias:
        @pl.when(pl.program_id(0) == 0)
        def _():
            carry_ref[...] = jnp.zeros_like(carry_ref)

        r = lax.broadcasted_iota(jnp.int32, (CUM_ROWS, CUM_ROWS), 0)
        c = lax.broadcasted_iota(jnp.int32, (CUM_ROWS, CUM_ROWS), 1)
        tri = (r >= c).astype(F32)
        run = carry_ref[0:1, :]
        parts = []
        for b in range(sm.shape[0] // CUM_ROWS):
            cs = jnp.dot(tri, sm[b * CUM_ROWS:(b + 1) * CUM_ROWS, :], precision=HIGHEST,
                         preferred_element_type=F32) + run
            run = cs[CUM_ROWS - 1:CUM_ROWS, :]
            parts.append(cs)
        carry_ref[0:1, :] = run
        nb = jnp.concatenate(parts, axis=0) * (-LOG2E)
        hi = nb.astype(BF16)
        r1 = nb - hi.astype(F32)
        mid = r1.astype(BF16)
        lo = (r1 - mid.astype(F32)).astype(BF16)
        aug = jnp.dot(jnp.concatenate([hi, mid, lo], axis=1), sel_ref[...], preferred_element_type=F32)
        kb = k.astype(BF16)
        for hd in range(FOX_HEADS):
            sl = slice(hd * FOX_HEAD_DIM, (hd + 1) * FOX_HEAD_DIM)
            kb_ref[:, 2 * hd * FOX_HEAD_DIM:(2 * hd + 1) * FOX_HEAD_DIM] = kb[:, sl]
            kb_ref[:, (2 * hd + 1) * FOX_HEAD_DIM:(2 * hd + 2) * FOX_HEAD_DIM] = aug[:, sl].astype(BF16)

        def max_sq_norms(xb):
            x2 = xb.astype(F32)
            x2 = x2 * x2
            out = jnp.zeros((1, LANES), F32)
            for hd in range(FOX_HEADS):
                n2 = jnp.sum(x2[:, hd * FOX_HEAD_DIM:(hd + 1) * FOX_HEAD_DIM], axis=-1, keepdims=True)
                out = jnp.where(lane[0:1, :] == hd, jnp.max(n2, axis=0, keepdims=True), out)
            return out

        stats_ref[...] = jnp.zeros(stats_ref.shape, F32)
        stats_ref[0:1, :] = max_sq_norms(qb)
        stats_ref[1:2, :] = max_sq_norms(kb)
        stats_ref[2:3, :] = nb[0:1, :]
        stats_ref[3:4, :] = nb[nb.shape[0] - 1:, :]
    else:
        kb_ref[...] = k.astype(BF16)
    v = proj(wqkv_ref, 2 * D_MODEL, D_MODEL)
    vf_ref[...] = v.reshape(v.shape[0], FOX_HEADS, FOX_HEAD_DIM)
    vb_ref[...] = v.astype(BF16)
    z_ref[...] = proj(wssm_ref, 0, SSM_INNER).astype(BF16)
    xbc_ref[...] = proj(wssm_ref, SSM_INNER, CONV_DIM)
    ga_ref[...] = proj(wgate_ref, 0, D_MODEL).astype(BF16)
    gb_ref[...] = proj(wgate_ref, D_MODEL, D_MODEL).astype(BF16)


def _mod_spec(arr, tm):
    if arr.shape[0] == 1:
        return pl.BlockSpec((1, D_MODEL), lambda i: (0, 0))
    return pl.BlockSpec((tm, D_MODEL), lambda i: (i, 0))


def _bias_selector():
    r = lax.broadcasted_iota(jnp.int32, (BIAS_PARTS * LANES, FOX_HEADS * FOX_HEAD_DIM), 0)
    c = lax.broadcasted_iota(jnp.int32, (BIAS_PARTS * LANES, FOX_HEADS * FOX_HEAD_DIM), 1)
    hit = (r % LANES < FOX_HEADS) & (c == (r % LANES) * FOX_HEAD_DIM + r // LANES)
    return hit.astype(BF16)


def _in_proj(x2d, g, sc, sh, w_qkv, w_ssm, w_gate, w_small, b_small, tm, with_bias):
    rows = x2d.shape[0]
    assert not with_bias or tm % CUM_ROWS == 0
    row = lambda w: pl.BlockSpec((tm, w), lambda i: (i, 0))
    shp = lambda w, dt: jax.ShapeDtypeStruct((rows, w), dt)
    kb_width = 2 * D_MODEL if with_bias else D_MODEL
    in_specs = [row(D_MODEL), _resident((1, D_MODEL)), _mod_spec(sc, tm), _mod_spec(sh, tm),
                _resident(w_qkv.shape), _resident(w_ssm.shape), _resident(w_gate.shape),
                _resident((D_MODEL, LANES)), _resident((1, LANES))]
    args = [x2d, g, sc, sh, w_qkv, w_ssm, w_gate, w_small, b_small]
    heads = pl.BlockSpec((tm, FOX_HEADS, FOX_HEAD_DIM), lambda i: (i, 0, 0))
    heads_shape = jax.ShapeDtypeStruct((rows, FOX_HEADS, FOX_HEAD_DIM), F32)
    out_specs = [row(D_MODEL), heads, heads, row(kb_width), row(D_MODEL),
                 row(SSM_INNER), row(CONV_DIM), row(D_MODEL), row(D_MODEL), row(LANES)]
    out_shape = [shp(D_MODEL, BF16), heads_shape, heads_shape, shp(kb_width, BF16),
                 shp(D_MODEL, BF16), shp(SSM_INNER, BF16), shp(CONV_DIM, F32), shp(D_MODEL, BF16),
                 shp(D_MODEL, BF16), shp(LANES, F32)]
    if with_bias:
        in_specs.append(_resident((BIAS_PARTS * LANES, FOX_HEADS * FOX_HEAD_DIM)))
        args.append(_bias_selector())
        out_specs.append(pl.BlockSpec((SUBLANES, LANES), lambda i: (i, 0)))
        out_shape.append(jax.ShapeDtypeStruct((rows // tm * SUBLANES, LANES), F32))
    return pl.pallas_call(
        functools.partial(_inproj_kernel, with_bias=with_bias),
        grid=(rows // tm,),
        in_specs=in_specs,
        out_specs=out_specs,
        out_shape=out_shape,
        scratch_shapes=[pltpu.VMEM((SUBLANES, LANES), F32)] if with_bias else [],
        compiler_params=_params("arbitrary" if with_bias else "parallel"),
        name="in_proj",
    )(*args)


def _flash_kernel(first_ref, q_ref, k_ref, v_ref, o_ref, s0_ref, s1_ref, p0_ref, p1_ref, m_ref, l_ref, alpha0_ref,
                  alpha1_ref, acc_ref, *, tq, tk):
    i = pl.program_id(1)
    nlt = tk // LANES
    lane = lax.broadcasted_iota(jnp.int32, (tq, FOX_HEAD_DIM), 1)
    ones = jnp.where(lane < BIAS_PARTS, 1.0, 0.0).astype(BF16)
    q2 = jnp.concatenate([q_ref[...], ones], axis=1)

    def scores(j, s_ref):
        off = pl.multiple_of(j * tk, tk)
        s_ref[...] = lax.dot_general(q2, k_ref[pl.ds(off, tk), :], _NT, preferred_element_type=F32)

    def softmax(s_ref, p_ref, alpha_ref, mask_shift):
        for rt in range(tq // ROW_TILE):
            rows = slice(rt * ROW_TILE, (rt + 1) * ROW_TILE)
            cols = [s_ref[rows, c * LANES:(c + 1) * LANES] for c in range(nlt)]
            if mask_shift is not None:
                ri = lax.broadcasted_iota(jnp.int32, (ROW_TILE, LANES), 0) + rt * ROW_TILE
                ci = lax.broadcasted_iota(jnp.int32, (ROW_TILE, LANES), 1) + mask_shift
                cols = [jnp.where(ci + c * LANES <= ri, cols[c], -jnp.inf) for c in range(nlt)]
            mx = functools.reduce(jnp.maximum, cols)
            m_old = m_ref[rows, :]
            m_new = jnp.maximum(m_old, jnp.max(mx, axis=-1, keepdims=True))
            alpha = jnp.exp2(m_old - m_new)
            ps = [jnp.exp2(cc - m_new) for cc in cols]
            l_ref[rows, :] = alpha * l_ref[rows, :] + jnp.sum(functools.reduce(jnp.add, ps), axis=-1, keepdims=True)
            m_ref[rows, :] = m_new
            alpha_ref[rows, :] = alpha
            for c in range(nlt):
                p_ref[rows, c * LANES:(c + 1) * LANES] = ps[c].astype(BF16)

    def weighted_values(j, p_ref, alpha_ref):
        off = pl.multiple_of(j * tk, tk)
        acc_ref[...] = alpha_ref[...] * acc_ref[...] + jnp.dot(p_ref[...], v_ref[pl.ds(off, tk), :],
                                                               preferred_element_type=F32)

    m_ref[...] = jnp.full(m_ref.shape, -jnp.inf, F32)
    l_ref[...] = jnp.zeros(l_ref.shape, F32)
    acc_ref[...] = jnp.zeros(acc_ref.shape, F32)
    bufs = ((s0_ref, p0_ref, alpha0_ref), (s1_ref, p1_ref, alpha1_ref))

    def run(first, n, diagonal_last, next_scores):
        for k in range(n):
            s_ref, p_ref, alpha_ref = bufs[k % 2]
            s_nxt, p_prv, alpha_prv = bufs[(k + 1) % 2]
            weighted_values(jnp.maximum(first + k - 1, 0), p_prv, alpha_prv)
            if k + 1 < n or next_scores:
                scores(first + k + 1, s_nxt)
            softmax(s_ref, p_ref, alpha_ref, 0 if (diagonal_last and k == n - 1) else None)

    p1_ref[...] = jnp.zeros(p1_ref.shape, BF16)
    alpha1_ref[...] = jnp.ones(alpha1_ref.shape, F32)
    j0 = first_ref[pl.program_id(0), i]
    n_full = i - j0
    scores(j0, s0_ref)

    @pl.loop(0, n_full // UNROLL)
    def _(t):
        run(j0 + UNROLL * t, UNROLL, False, True)

    for rem in range(UNROLL):
        @pl.when(n_full % UNROLL == rem)
        def _():
            run(i - rem, rem + 1, True, False)
            weighted_values(i, *bufs[rem % 2][1:])

    o_ref[...] = (acc_ref[...] / l_ref[...]).astype(o_ref.dtype)


def _first_blocks(stats, nq):
    st = stats.reshape(nq, SUBLANES, LANES)
    qn = jnp.sqrt(st[:, 0, :FOX_HEADS])
    kn = jnp.sqrt(st[:, 1, :FOX_HEADS])
    b_first, b_last = st[:, 2, :FOX_HEADS], st[:, 3, :FOX_HEADS]
    bound = qn[:, None, :] * (kn[None, :, :] + kn[:, None, :]) + b_last[None, :, :] - b_first[:, None, :]
    blocks = jnp.arange(nq, dtype=jnp.int32)
    first_needed = jnp.min(jnp.where(bound <= -SKIP_MARGIN, nq, blocks[None, :, None]), axis=1)
    return jnp.minimum(first_needed, blocks[:, None]).T


def _fox_prompt(q, kp, v, first, tq):
    L = q.shape[0]
    tk = tq
    return pl.pallas_call(
        functools.partial(_flash_kernel, tq=tq, tk=tk),
        grid=(FOX_HEADS, L // tq),
        in_specs=[pl.BlockSpec(memory_space=pltpu.SMEM),
                  pl.BlockSpec((tq, FOX_HEAD_DIM), lambda h, i: (i, h)),
                  pl.BlockSpec((L, 2 * FOX_HEAD_DIM), lambda h, i: (0, h)),
                  pl.BlockSpec((L, FOX_HEAD_DIM), lambda h, i: (0, h))],
        out_specs=pl.BlockSpec((tq, FOX_HEAD_DIM), lambda h, i: (i, h)),
        out_shape=jax.ShapeDtypeStruct((L, FOX_HEADS * FOX_HEAD_DIM), BF16),
        scratch_shapes=[pltpu.VMEM((tq, tk), F32)] * 2 + [pltpu.VMEM((tq, tk), BF16)] * 2
        + [pltpu.VMEM((tq, FOX_HEAD_DIM), F32)] * 5,
        compiler_params=_params("parallel", "arbitrary"),
        name="fox_flash",
    )(first, q, kp, v)


def _cached_attn_kernel(q_ref, kn_ref, vn_ref, kc_ref, vc_ref, lpt_ref, lnt_ref, o_ref):
    n = q_ref.shape[0]
    P = kc_ref.shape[0]
    r = lax.broadcasted_iota(jnp.int32, (P, P), 0)
    c = lax.broadcasted_iota(jnp.int32, (P, P), 1)
    G = jnp.dot(lpt_ref[...], (r > c).astype(F32), precision=HIGHEST, preferred_element_type=F32) * LOG2E
    rn = lax.broadcasted_iota(jnp.int32, (n, n), 0)
    cn = lax.broadcasted_iota(jnp.int32, (n, n), 1)
    Hn = jnp.dot(lnt_ref[...], (rn <= cn).astype(F32), precision=HIGHEST, preferred_element_type=F32) * LOG2E
    causal = cn <= rn
    kt = jnp.swapaxes(kc_ref[...], 0, 1)
    vt = jnp.swapaxes(vc_ref[...], 0, 1)
    for h in range(FOX_HEADS):
        sl = slice(h * FOX_HEAD_DIM, (h + 1) * FOX_HEAD_DIM)
        qh = q_ref[:, sl]
        sp = lax.dot_general(qh, kt[h].astype(BF16), _NT, preferred_element_type=F32) + G[h:h + 1, :]
        sn = lax.dot_general(qh, kn_ref[:, sl], _NT, preferred_element_type=F32) - Hn[h:h + 1, :]
        sn = jnp.where(causal, sn, -jnp.inf)
        m = jnp.maximum(jnp.max(sp, axis=-1, keepdims=True), jnp.max(sn, axis=-1, keepdims=True))
        pp = jnp.exp2(sp - m)
        pn = jnp.exp2(sn - m)
        l = jnp.sum(pp, axis=-1, keepdims=True) + jnp.sum(pn, axis=-1, keepdims=True)
        o = (jnp.dot(pp.astype(BF16), vt[h].astype(BF16), preferred_element_type=F32)
             + jnp.dot(pn.astype(BF16), vn_ref[:, sl], preferred_element_type=F32))
        o_ref[:, sl] = (o / l).astype(o_ref.dtype)


def _fox_cached(q, kn, vn, kc, vc, lpt, lnt, n):
    B, P = kc.shape[:2]
    W = FOX_HEADS * FOX_HEAD_DIM
    row = pl.BlockSpec((n, W), lambda b: (b, 0))
    cache = pl.BlockSpec((None, P, FOX_HEADS, FOX_HEAD_DIM), lambda b: (b, 0, 0, 0))
    return pl.pallas_call(
        _cached_attn_kernel,
        grid=(B,),
        in_specs=[row, row, row, cache, cache,
                  pl.BlockSpec((None, FOX_HEADS, P), lambda b: (b, 0, 0)),
                  pl.BlockSpec((None, FOX_HEADS, n), lambda b: (b, 0, 0))],
        out_specs=row,
        out_shape=jax.ShapeDtypeStruct((B * n, W), BF16),
        compiler_params=_params("parallel"),
        name="fox_cached",
    )(q, kn, vn, kc, vc, lpt, lnt)


def _ssd_kernel(xbc_ref, z_ref, sm_ref, convp_ref, ssmp_ref, cw_ref, cb_ref, alog_ref, dexp_ref, ng_ref,
                expand_ref, y_ref, convn_ref, ssmn_ref, tail_ref, st_ref, *, Q, nchunks):
    ci = pl.program_id(1)
    PAIR = 2 * SSM_HEAD_DIM
    GW = SSM_STATE

    @pl.when(ci == 0)
    def _():
        st_ref[...] = ssmp_ref[...]
        tail_ref[...] = jnp.zeros((SUBLANES, CONV_DIM), F32)
        tail_ref[SUBLANES - (CONV_WIDTH - 1):SUBLANES, :] = convp_ref[...]

    x3 = xbc_ref[...].reshape(Q // SUBLANES, SUBLANES, CONV_DIM)
    tail = tail_ref[...].reshape(1, SUBLANES, CONV_DIM)
    sub = lax.broadcasted_iota(jnp.int32, x3.shape, 1)
    conv = cb_ref[...]
    for d in range(CONV_WIDTH - 1, 0, -1):
        rot = pltpu.roll(x3, d, 1)
        prev = jnp.concatenate([pltpu.roll(tail, d, 1), rot[:-1]], axis=0)
        conv = conv + jnp.where(sub < d, prev, rot) * cw_ref[CONV_WIDTH - 1 - d:CONV_WIDTH - d, :]
    conv = (conv + x3 * cw_ref[CONV_WIDTH - 1:CONV_WIDTH, :]).reshape(Q, CONV_DIM)
    tail_ref[...] = xbc_ref[Q - SUBLANES:Q, :]
    convn_ref[...] = xbc_ref[Q - (CONV_WIDTH - 1):Q, :]

    xc = _silu(conv)
    xs = xc[:, :SSM_INNER]
    Bm = xc[:, SSM_INNER:SSM_INNER + SSM_GROUPS * GW]
    Cm = xc[:, SSM_INNER + SSM_GROUPS * GW:]

    lane1 = lax.broadcasted_iota(jnp.int32, (Q, LANES), 1)
    head_lanes = (lane1 >= DT_LANE0) & (lane1 < DT_LANE0 + SSM_HEADS)
    dt = jnp.where(head_lanes, sm_ref[...], 0.0)
    a = dt * (-jnp.exp(alog_ref[...]))
    r = lax.broadcasted_iota(jnp.int32, (Q, Q), 0)
    c = lax.broadcasted_iota(jnp.int32, (Q, Q), 1)
    causal = r >= c
    a_cs = jnp.dot(causal.astype(F32), a, precision=HIGHEST, preferred_element_type=F32)
    er = lax.broadcasted_iota(jnp.int32, (SSM_HEADS, LANES), 0)
    ec = lax.broadcasted_iota(jnp.int32, (SSM_HEADS, LANES), 1)
    pick = (ec == er + DT_LANE0).astype(F32)
    acs_t = lax.dot_general(pick, a_cs, _NT, precision=HIGHEST, preferred_element_type=F32)
    a_last = a_cs[Q - 1:Q, :]
    e_last = jnp.exp(a_last)

    def per_channel(x):
        x = jnp.where(head_lanes, x, 0.0)
        hi = x.astype(BF16).astype(F32)
        r1 = x - hi
        mid = r1.astype(BF16).astype(F32)
        packed = hi + pltpu.roll(mid, PIECE_LANES, 1) + pltpu.roll(r1 - mid, 2 * PIECE_LANES, 1)
        return jnp.dot(packed.astype(BF16), expand_ref[...], preferred_element_type=F32)

    xd = xs * per_channel(dt)
    xdb = xd.astype(BF16)
    xw = xd * per_channel(jnp.exp(a_last - a_cs))
    e_cs = per_channel(jnp.exp(a_cs))
    Bb = Bm.astype(BF16)
    Cb = Cm.astype(BF16)

    row_bands = [(0, Q // 2, Q // 2), (Q // 2, Q, Q)] if Q % (2 * LANES) == 0 else [(0, Q, Q)]
    lane_lo = lax.broadcasted_iota(jnp.int32, (Q, PAIR), 1) < SSM_HEAD_DIM
    row_lo = lax.broadcasted_iota(jnp.int32, (PAIR, GW), 0) < SSM_HEAD_DIM

    y_parts = []
    for g in range(SSM_GROUPS):
        Bg = Bb[:, g * GW:(g + 1) * GW]
        Cg = Cb[:, g * GW:(g + 1) * GW]
        cbm = lax.dot_general(Cg, Bg, _NT, preferred_element_type=F32)
        for pr in range(SSM_HEADS // SSM_GROUPS // 2):
            pair = g * (SSM_HEADS // SSM_GROUPS // 2) + pr
            sl = slice(pair * PAIR, (pair + 1) * PAIR)
            s_prev = st_ref[sl, :]
            ys = []
            for hh in (2 * pair, 2 * pair + 1):
                ln = DT_LANE0 + hh
                parts = []
                for r0, r1, nc in row_bands:
                    lm = jnp.exp(jnp.where(causal[r0:r1, :nc], a_cs[r0:r1, ln:ln + 1] - acs_t[hh:hh + 1, :nc],
                                           -jnp.inf))
                    parts.append(jnp.dot((cbm[r0:r1, :nc] * lm).astype(BF16), xdb[:nc, sl],
                                         preferred_element_type=F32))
                ys.append(parts[0] if len(parts) == 1 else jnp.concatenate(parts, axis=0))
            y_off = lax.dot_general(Cg, s_prev.astype(BF16), _NT, preferred_element_type=F32) * e_cs[:, sl]
            ln0 = DT_LANE0 + 2 * pair
            dec = jnp.where(row_lo, e_last[:, ln0:ln0 + 1], e_last[:, ln0 + 1:ln0 + 2])
            st_ref[sl, :] = s_prev * dec + lax.dot_general(xw[:, sl], Bm[:, g * GW:(g + 1) * GW], _TN,
                                                           preferred_element_type=F32)
            y_parts.append(jnp.where(lane_lo, ys[0], ys[1]) + y_off + xs[:, sl] * dexp_ref[:, sl])

    y = jnp.concatenate(y_parts, axis=1) * _silu(z_ref[...].astype(F32))
    gw = SSM_INNER // SSM_GROUPS
    normed = []
    for g in range(SSM_GROUPS):
        yg = y[:, g * gw:(g + 1) * gw]
        normed.append(yg * lax.rsqrt(jnp.mean(yg * yg, axis=-1, keepdims=True) + EPS))
    y_ref[...] = (jnp.concatenate(normed, axis=1) * ng_ref[...]).astype(y_ref.dtype)

    @pl.when(ci == nchunks - 1)
    def _():
        ssmn_ref[...] = st_ref[...]


def _head_expander():
    k = lax.broadcasted_iota(jnp.int32, (LANES, SSM_INNER), 0)
    j = lax.broadcasted_iota(jnp.int32, (LANES, SSM_INNER), 1)
    hit = (k < BIAS_PARTS * PIECE_LANES) & (k % PIECE_LANES == DT_LANE0 + j // SSM_HEAD_DIM)
    return hit.astype(BF16)


def _ssd(xbc, z, small, conv_past, ssm_past, conv_w, conv_b, alog_pad, d_exp, norm_g, batch, Q):
    rows = xbc.shape[0]
    nchunks = rows // batch // Q
    assert Q % SUBLANES == 0 and Q >= SUBLANES and nchunks * Q * batch == rows
    row = lambda w: pl.BlockSpec((Q, w), lambda b, c: (b * nchunks + c, 0))
    state_rows = SSM_HEADS * SSM_HEAD_DIM
    per_seq = lambda a, b_: pl.BlockSpec((None, a, b_), lambda b, c: (b, 0, 0))
    return pl.pallas_call(
        functools.partial(_ssd_kernel, Q=Q, nchunks=nchunks),
        grid=(batch, nchunks),
        in_specs=[row(CONV_DIM), row(SSM_INNER), row(LANES),
                  per_seq(CONV_WIDTH - 1, CONV_DIM), per_seq(state_rows, SSM_STATE),
                  _resident((CONV_WIDTH, CONV_DIM)), _resident((1, CONV_DIM)), _resident((1, LANES)),
                  _resident((1, SSM_INNER)), _resident((1, SSM_INNER)), _resident((LANES, SSM_INNER))],
        out_specs=[row(SSM_INNER), per_seq(CONV_WIDTH - 1, CONV_DIM), per_seq(state_rows, SSM_STATE)],
        out_shape=[jax.ShapeDtypeStruct((rows, SSM_INNER), BF16),
                   jax.ShapeDtypeStruct((batch, CONV_WIDTH - 1, CONV_DIM), F32),
                   jax.ShapeDtypeStruct((batch, state_rows, SSM_STATE), F32)],
        scratch_shapes=[pltpu.VMEM((SUBLANES, CONV_DIM), F32), pltpu.VMEM((state_rows, SSM_STATE), F32)],
        compiler_params=_params("parallel", "arbitrary"),
        name="ssd_scan",
    )(xbc, z, small, conv_past, ssm_past, conv_w, conv_b, alog_pad, d_exp, norm_g, _head_expander())


def _outffn_kernel(ya_ref, ys_ref, ga_ref, gb_ref, x_ref, gt1_ref, sc2_ref, sh2_ref, gt2_ref,
                   gpm_ref, gpf_ref, gqf_ref, wo_ref, wu_ref, wd_ref, o_ref):
    merged = (_sigmoid(ga_ref[...].astype(F32)) * ya_ref[...].astype(F32)
              + _sigmoid(gb_ref[...].astype(F32)) * ys_ref[...].astype(F32))
    m = jnp.dot(merged.astype(BF16), wo_ref[...], preferred_element_type=F32)
    x1 = x_ref[...] + gt1_ref[...] * _rms(m, gpm_ref[...])
    hb = (_rms(x1, gpf_ref[...]) * (1.0 + sc2_ref[...]) + sh2_ref[...]).astype(BF16)
    f = jnp.zeros(x1.shape, F32)
    for cc in range(D_FF // D_MODEL):
        sl = slice(cc * D_MODEL, (cc + 1) * D_MODEL)
        up = jnp.dot(hb, wu_ref[:, sl], preferred_element_type=F32)
        act = jnp.square(jnp.maximum(up, 0.0)).astype(BF16)
        f = f + jnp.dot(act, wd_ref[sl, :], preferred_element_type=F32)
    o_ref[...] = x1 + gt2_ref[...] * _rms(f, gqf_ref[...])


def _out_ffn(ya, ys, ga, gb, x2d, gt1, sc2, sh2, gt2, g_post_mix, g_pre_ffn, g_post_ffn, w_out, w_up, w_down, tm):
    rows = x2d.shape[0]
    row = pl.BlockSpec((tm, D_MODEL), lambda i: (i, 0))
    vec = _resident((1, D_MODEL))
    return pl.pallas_call(
        _outffn_kernel,
        grid=(rows // tm,),
        in_specs=[row, row, row, row, row,
                  _mod_spec(gt1, tm), _mod_spec(sc2, tm), _mod_spec(sh2, tm), _mod_spec(gt2, tm),
                  vec, vec, vec,
                  _resident((D_MODEL, D_MODEL)), _resident((D_MODEL, D_FF)), _resident((D_FF, D_MODEL))],
        out_specs=row,
        out_shape=jax.ShapeDtypeStruct((rows, D_MODEL), F32),
        compiler_params=_params("parallel"),
        name="out_ffn",
    )(ya, ys, ga, gb, x2d, gt1, sc2, sh2, gt2, g_post_mix, g_pre_ffn, g_post_ffn, w_out, w_up, w_down)


def _layer(x, mod, past, wts, *, tm, tq, Q):
    b, L, _ = x.shape
    x2d = x.reshape(b * L, D_MODEL)
    sh1, sc1, gt1, sh2, sc2, gt2 = mod
    q, kf, vf, kb, vb, z, xbc, ga, gb, small, *stats = _in_proj(
        x2d, wts["g_pre_mix"], sc1, sh1, wts["w_qkv"], wts["w_ssm"], wts["w_gate"], wts["w_small"], wts["b_small"],
        tm, past is None)

    if past is None:
        assert b == 1 and tm == tq, "block statistics are per projection row block"
        y_att = _fox_prompt(q, kb, vb, _first_blocks(stats[0], L // tq), tq)
        conv_past = jnp.zeros((b, CONV_WIDTH - 1, CONV_DIM), F32)
        ssm_past = jnp.zeros((b, SSM_INNER, SSM_STATE), F32)
    else:
        k_past, v_past, logf_past, conv_past, ssm_past = past
        P = k_past.shape[1]
        lpt = jnp.swapaxes(logf_past, 1, 2)
        lnt = jnp.swapaxes(small[:, :FOX_HEADS].reshape(b, L, FOX_HEADS), 1, 2)
        y_att = _fox_cached(q, kb, vb, k_past, v_past, lpt, lnt, L)
        ssm_past = ssm_past.reshape(b, SSM_INNER, SSM_STATE)

    y_ssm, conv_new, ssm_new = _ssd(xbc, z, small, conv_past, ssm_past, wts["conv_w"], wts["conv_b"],
                                    wts["alog_pad"], wts["d_exp"], wts["ssm_norm_g"], b, Q)
    y = _out_ffn(y_att, y_ssm, ga, gb, x2d, gt1, sc2, sh2, gt2, wts["g_post_mix"], wts["g_pre_ffn"],
                 wts["g_post_ffn"], wts["w_out"], wts["w_up"], wts["w_down"], tm)
    return (y.reshape(b, L, D_MODEL),
            kf.reshape(b, L, FOX_HEADS, FOX_HEAD_DIM), vf.reshape(b, L, FOX_HEADS, FOX_HEAD_DIM),
            small[:, :FOX_HEADS].reshape(b, L, FOX_HEADS), conv_new,
            ssm_new.reshape(b, SSM_HEADS, SSM_HEAD_DIM, SSM_STATE))


def _prep_weights(w_ada, b_ada, g_pre_mix, g_post_mix, g_pre_ffn, g_post_ffn, w_in, b_f, conv_w, conv_b,
                  dt_bias, a_log, d_skip, ssm_norm_g, w_out, w_up, w_down):
    sizes = (D_MODEL, D_MODEL, D_MODEL, FOX_HEADS, SSM_INNER, CONV_DIM, SSM_HEADS, D_MODEL, D_MODEL)
    offs = [0]
    for s in sizes:
        offs.append(offs[-1] + s)
    piece = lambda i: w_in[:, offs[i]:offs[i + 1]]
    pad = LANES - FOX_HEADS - SSM_HEADS
    row = lambda v: v.reshape(1, -1).astype(F32)
    return {
        "w_qkv": w_in[:, offs[0]:offs[3]].astype(BF16),
        "w_ssm": w_in[:, offs[4]:offs[6]].astype(BF16),
        "w_gate": w_in[:, offs[7]:offs[9]].astype(BF16),
        "w_small": jnp.pad(jnp.concatenate([piece(3), piece(6)], axis=1), ((0, 0), (0, pad))).astype(BF16),
        "b_small": jnp.pad(jnp.concatenate([b_f, dt_bias]), (0, pad)).reshape(1, LANES).astype(F32),
        "alog_pad": jnp.pad(a_log, (DT_LANE0, LANES - DT_LANE0 - SSM_HEADS)).reshape(1, LANES).astype(F32),
        "d_exp": jnp.repeat(d_skip, SSM_HEAD_DIM).reshape(1, SSM_INNER).astype(F32),
        "g_pre_mix": row(g_pre_mix), "g_post_mix": row(g_post_mix),
        "g_pre_ffn": row(g_pre_ffn), "g_post_ffn": row(g_post_ffn),
        "conv_w": conv_w.astype(F32), "conv_b": row(conv_b), "ssm_norm_g": row(ssm_norm_g),
        "w_out": w_out.astype(BF16), "w_up": w_up.astype(BF16), "w_down": w_down.astype(BF16),
    }


def _forward(x_prompt, x_sample, c_prompt, c_sample, cache_fox_k, cache_fox_v, cache_fox_logf,
             state_ssm_conv, state_ssm, w_ada, b_ada, *layer_w, tm_prompt, tq, q_prompt):
    depth = w_ada.shape[0]
    bp, Lp, _ = x_prompt.shape
    bs, Ls, _ = x_sample.shape
    yp, ys = x_prompt, x_sample
    outs_p, outs_s = [], []
    for i in range(depth):
        wts = _prep_weights(w_ada[i], b_ada[i], *[w[i] for w in layer_w])
        mod = _ada(jnp.concatenate([c_prompt, c_sample], axis=0), w_ada[i], b_ada[i].reshape(1, -1))
        mod_p = [m for m in jnp.split(mod[:bp], N_MOD, axis=-1)]
        mod_s = [jnp.repeat(m, Ls, axis=0) for m in jnp.split(mod[bp:], N_MOD, axis=-1)]
        rp = _layer(yp, mod_p, None, wts, tm=tm_prompt, tq=tq, Q=q_prompt)
        rs = _layer(ys, mod_s, (cache_fox_k[i], cache_fox_v[i], cache_fox_logf[i], state_ssm_conv[i], state_ssm[i]),
                    wts, tm=bs * Ls, tq=None, Q=Ls)
        yp, ys = rp[0], rs[0]
        outs_p.append(rp[1:])
        outs_s.append(rs[1:])
    stack = lambda outs, j: jnp.stack([o[j] for o in outs])
    return (yp, ys) + tuple(stack(outs_p, j) for j in range(5)) + tuple(stack(outs_s, j) for j in range(5))


def kernel(x_prompt, x_sample, c_prompt, c_sample, cache_fox_k, cache_fox_v, cache_fox_logf, state_ssm_conv,
           state_ssm, w_ada, b_ada, g_pre_mix, g_post_mix, g_pre_ffn, g_post_ffn, w_in, b_f, conv_w, conv_b,
           dt_bias, a_log, d_skip, ssm_norm_g, w_out, w_up, w_down):
    assert x_prompt.shape[0] == 1, "the prompt path carries one sequence"
    L = x_prompt.shape[1]
    return _forward(x_prompt, x_sample, c_prompt, c_sample, cache_fox_k, cache_fox_v, cache_fox_logf,
                    state_ssm_conv, state_ssm, w_ada, b_ada, g_pre_mix, g_post_mix, g_pre_ffn, g_post_ffn,
                    w_in, b_f, conv_w, conv_b, dt_bias, a_log, d_skip, ssm_norm_g, w_out, w_up, w_down,
                    tm_prompt=min(512, L), tq=min(512, L), q_prompt=min(256, L))
```

```python
import functools

import jax
import jax.numpy as jnp
from jax import lax
from jax.experimental import pallas as pl
from jax.experimental.pallas import tpu as pltpu

F32 = jnp.float32
BF16 = jnp.bfloat16
HIGHEST = lax.Precision.HIGHEST

D_MODEL = 1024
FOX_HEADS = 8
FOX_HEAD_DIM = 128
FOX_SCALE = FOX_HEAD_DIM ** -0.5
SSM_HEADS = 16
SSM_HEAD_DIM = 64
SSM_GROUPS = 4
SSM_STATE = 128
SSM_INNER = SSM_HEADS * SSM_HEAD_DIM
CONV_WIDTH = 4
CONV_DIM = SSM_INNER + 2 * SSM_GROUPS * SSM_STATE
D_FF = 4 * D_MODEL
N_MOD = 6
EPS = 1e-6

LANES = 128
SUBLANES = 8
DT_LANE0 = FOX_HEADS
VMEM_LIMIT = 56 * 1024 * 1024
LOG2E = 1.4426950408889634
BIAS_PARTS = 3
PIECE_LANES = 32
CUM_ROWS = 128
ROW_TILE = 16
UNROLL = 4
FLASH_SUBBLOCKS = 4
SKIP_MARGIN = 152.0


_NT = (((1,), (1,)), ((), ()))
_TN = (((0,), (0,)), ((), ()))


def _params(*sem):
    return pltpu.CompilerParams(dimension_semantics=sem, vmem_limit_bytes=VMEM_LIMIT)


def _resident(shape):
    zeros = (0,) * len(shape)
    return pl.BlockSpec(shape, lambda *_: zeros, pipeline_mode=pl.Buffered(1))


def _rms(x, g):
    return x * lax.rsqrt(jnp.mean(x * x, axis=-1, keepdims=True) + EPS) * g


def _sigmoid(x):
    return 0.5 * jnp.tanh(0.5 * x) + 0.5


def _silu(x):
    h = 0.5 * x
    return h * jnp.tanh(h) + h


def _ada_kernel(c_ref, w_ref, b_ref, o_ref):
    s = _silu(c_ref[...]).astype(BF16)
    o_ref[...] = jnp.dot(s, w_ref[...].astype(BF16), preferred_element_type=F32) + b_ref[...]


def _ada(c_all, w_ada, b_ada):
    rows = c_all.shape[0]
    width = w_ada.shape[1]
    tn = 1536
    return pl.pallas_call(
        _ada_kernel,
        grid=(width // tn,),
        in_specs=[pl.BlockSpec((rows, D_MODEL), lambda j: (0, 0)),
                  pl.BlockSpec((D_MODEL, tn), lambda j: (0, j)),
                  pl.BlockSpec((1, tn), lambda j: (0, j))],
        out_specs=pl.BlockSpec((rows, tn), lambda j: (0, j)),
        out_shape=jax.ShapeDtypeStruct((rows, width), F32),
        compiler_params=_params("parallel"),
        name="ada_mod",
    )(c_all, w_ada, b_ada)


def _inproj_kernel(x_ref, g_ref, sc_ref, sh_ref, wqkv_ref, wssm_ref, wgate_ref, ws_ref, bs_ref, *rest, with_bias):
    if with_bias:
        (sel_ref, q_ref, kf_ref, vf_ref, kb_ref, vb_ref, z_ref, xbc_ref, ga_ref, gb_ref, sm_ref, stats_ref,
         carry_ref) = rest
    else:
        q_ref, kf_ref, vf_ref, kb_ref, vb_ref, z_ref, xbc_ref, ga_ref, gb_ref, sm_ref = rest
    h = _rms(x_ref[...], g_ref[...]) * (1.0 + sc_ref[...]) + sh_ref[...]
    hb = h.astype(BF16)

    def proj(w_ref, lo, width):
        return jnp.dot(hb, w_ref[:, lo:lo + width], preferred_element_type=F32)

    u = jnp.dot(hb, ws_ref[...], preferred_element_type=F32) + bs_ref[...]
    t = jnp.log1p(jnp.exp(-jnp.abs(u)))
    lane = lax.broadcasted_iota(jnp.int32, u.shape, 1)
    sm = jnp.where(lane < DT_LANE0, jnp.minimum(u, 0.0) - t,
                   jnp.where(lane < DT_LANE0 + SSM_HEADS, jnp.maximum(u, 0.0) + t, 0.0))
    sm_ref[...] = sm

    qb = (proj(wqkv_ref, 0, D_MODEL) * (FOX_SCALE * LOG2E)).astype(BF16)
    q_ref[...] = qb
    k = proj(wqkv_ref, D_MODEL, D_MODEL)
    kf_ref[...] = k.reshape(k.shape[0], FOX_HEADS, FOX_HEAD_DIM)
    if with_bias:
        @pl.when(pl.program_id(0) == 0)
        def _():
            carry_ref[...] = jnp.zeros_like(carry_ref)

        r = lax.broadcasted_iota(jnp.int32, (CUM_ROWS, CUM_ROWS), 0)
        c = lax.broadcasted_iota(jnp.int32, (CUM_ROWS, CUM_ROWS), 1)
        tri = (r >= c).astype(F32)
        run = carry_ref[0:1, :]
        parts = []
        for b in range(sm.shape[0] // CUM_ROWS):
            cs = jnp.dot(tri, sm[b * CUM_ROWS:(b + 1) * CUM_ROWS, :], precision=HIGHEST,
                         preferred_element_type=F32) + run
            run = cs[CUM_ROWS - 1:CUM_ROWS, :]
            parts.append(cs)
        carry_ref[0:1, :] = run
        nb = jnp.concatenate(parts, axis=0) * (-LOG2E)
        hi = nb.astype(BF16)
        r1 = nb - hi.astype(F32)
        mid = r1.astype(BF16)
        lo = (r1 - mid.astype(F32)).astype(BF16)
        aug = jnp.dot(jnp.concatenate([hi, mid, lo], axis=1), sel_ref[...], preferred_element_type=F32)
        kb = k.astype(BF16)
        for hd in range(FOX_HEADS):
            sl = slice(hd * FOX_HEAD_DIM, (hd + 1) * FOX_HEAD_DIM)
            kb_ref[:, 2 * hd * FOX_HEAD_DIM:(2 * hd + 1) * FOX_HEAD_DIM] = kb[:, sl]
            kb_ref[:, (2 * hd + 1) * FOX_HEAD_DIM:(2 * hd + 2) * FOX_HEAD_DIM] = aug[:, sl].astype(BF16)

        def max_sq_norms(xb):
            x2 = xb.astype(F32)
            x2 = x2 * x2
            out = jnp.zeros((1, LANES), F32)
            for hd in range(FOX_HEADS):
                n2 = jnp.sum(x2[:, hd * FOX_HEAD_DIM:(hd + 1) * FOX_HEAD_DIM], axis=-1, keepdims=True)
                out = jnp.where(lane[0:1, :] == hd, jnp.max(n2, axis=0, keepdims=True), out)
            return out

        stats_ref[...] = jnp.zeros(stats_ref.shape, F32)
        stats_ref[0:1, :] = max_sq_norms(qb)
        stats_ref[1:2, :] = max_sq_norms(kb)
        stats_ref[2:3, :] = nb[0:1, :]
        stats_ref[3:4, :] = nb[nb.shape[0] - 1:, :]
    else:
        kb_ref[...] = k.astype(BF16)
    v = proj(wqkv_ref, 2 * D_MODEL, D_MODEL)
    vf_ref[...] = v.reshape(v.shape[0], FOX_HEADS, FOX_HEAD_DIM)
    vb_ref[...] = v.astype(BF16)
    z_ref[...] = proj(wssm_ref, 0, SSM_INNER).astype(BF16)
    xbc_ref[...] = proj(wssm_ref, SSM_INNER, CONV_DIM)
    ga_ref[...] = proj(wgate_ref, 0, D_MODEL).astype(BF16)
    gb_ref[...] = proj(wgate_ref, D_MODEL, D_MODEL).astype(BF16)


def _mod_spec(arr, tm):
    if arr.shape[0] == 1:
        return pl.BlockSpec((1, D_MODEL), lambda i: (0, 0))
    return pl.BlockSpec((tm, D_MODEL), lambda i: (i, 0))


def _bias_selector():
    r = lax.broadcasted_iota(jnp.int32, (BIAS_PARTS * LANES, FOX_HEADS * FOX_HEAD_DIM), 0)
    c = lax.broadcasted_iota(jnp.int32, (BIAS_PARTS * LANES, FOX_HEADS * FOX_HEAD_DIM), 1)
    hit = (r % LANES < FOX_HEADS) & (c == (r % LANES) * FOX_HEAD_DIM + r // LANES)
    return hit.astype(BF16)


def _in_proj(x2d, g, sc, sh, w_qkv, w_ssm, w_gate, w_small, b_small, tm, with_bias):
    rows = x2d.shape[0]
    assert not with_bias or tm % CUM_ROWS == 0
    row = lambda w: pl.BlockSpec((tm, w), lambda i: (i, 0))
    shp = lambda w, dt: jax.ShapeDtypeStruct((rows, w), dt)
    kb_width = 2 * D_MODEL if with_bias else D_MODEL
    in_specs = [row(D_MODEL), _resident((1, D_MODEL)), _mod_spec(sc, tm), _mod_spec(sh, tm),
                _resident(w_qkv.shape), _resident(w_ssm.shape), _resident(w_gate.shape),
                _resident((D_MODEL, LANES)), _resident((1, LANES))]
    args = [x2d, g, sc, sh, w_qkv, w_ssm, w_gate, w_small, b_small]
    heads = pl.BlockSpec((tm, FOX_HEADS, FOX_HEAD_DIM), lambda i: (i, 0, 0))
    heads_shape = jax.ShapeDtypeStruct((rows, FOX_HEADS, FOX_HEAD_DIM), F32)
    out_specs = [row(D_MODEL), heads, heads, row(kb_width), row(D_MODEL),
                 row(SSM_INNER), row(CONV_DIM), row(D_MODEL), row(D_MODEL), row(LANES)]
    out_shape = [shp(D_MODEL, BF16), heads_shape, heads_shape, shp(kb_width, BF16),
                 shp(D_MODEL, BF16), shp(SSM_INNER, BF16), shp(CONV_DIM, F32), shp(D_MODEL, BF16),
                 shp(D_MODEL, BF16), shp(LANES, F32)]
    if with_bias:
        in_specs.append(_resident((BIAS_PARTS * LANES, FOX_HEADS * FOX_HEAD_DIM)))
        args.append(_bias_selector())
        out_specs.append(pl.BlockSpec((SUBLANES, LANES), lambda i: (i, 0)))
        out_shape.append(jax.ShapeDtypeStruct((rows // tm * SUBLANES, LANES), F32))
    return pl.pallas_call(
        functools.partial(_inproj_kernel, with_bias=with_bias),
        grid=(rows // tm,),
        in_specs=in_specs,
        out_specs=out_specs,
        out_shape=out_shape,
        scratch_shapes=[pltpu.VMEM((SUBLANES, LANES), F32)] if with_bias else [],
        compiler_params=_params("arbitrary" if with_bias else "parallel"),
        name="in_proj",
    )(*args)


def _flash_kernel(first_ref, q_ref, k_ref, v_ref, o_ref, s0_ref, s1_ref, p0_ref, p1_ref, m_ref, l_ref, alpha0_ref,
                  alpha1_ref, acc_ref, *, tq, tk, nsub):
    hd = pl.program_id(0)
    first_sub = pl.program_id(1) * nsub
    nlt = tk // LANES
    lane = lax.broadcasted_iota(jnp.int32, (tq, FOX_HEAD_DIM), 1)
    ones = jnp.where(lane < BIAS_PARTS, 1.0, 0.0).astype(BF16)

    def q_rows(sub):
        return pl.ds(pl.multiple_of(sub * tq, tq), tq)

    def q_aug(sub):
        return jnp.concatenate([q_ref[q_rows(sub), :], ones], axis=1)

    def scores(q2, j, s_ref):
        off = pl.multiple_of(j * tk, tk)
        s_ref[...] = lax.dot_general(q2, k_ref[pl.ds(off, tk), :], _NT, preferred_element_type=F32)

    def softmax(s_ref, p_ref, alpha_ref, mask_shift):
        for rt in range(tq // ROW_TILE):
            rows = slice(rt * ROW_TILE, (rt + 1) * ROW_TILE)
            cols = [s_ref[rows, c * LANES:(c + 1) * LANES] for c in range(nlt)]
            if mask_shift is not None:
                ri = lax.broadcasted_iota(jnp.int32, (ROW_TILE, LANES), 0) + rt * ROW_TILE
                ci = lax.broadcasted_iota(jnp.int32, (ROW_TILE, LANES), 1) + mask_shift
                cols = [jnp.where(ci + c * LANES <= ri, cols[c], -jnp.inf) for c in range(nlt)]
            mx = functools.reduce(jnp.maximum, cols)
            m_old = m_ref[rows, :]
            m_new = jnp.maximum(m_old, jnp.max(mx, axis=-1, keepdims=True))
            alpha = jnp.exp2(m_old - m_new)
            ps = [jnp.exp2(cc - m_new) for cc in cols]
            l_ref[rows, :] = alpha * l_ref[rows, :] + jnp.sum(functools.reduce(jnp.add, ps), axis=-1, keepdims=True)
            m_ref[rows, :] = m_new
            alpha_ref[rows, :] = alpha
            for c in range(nlt):
                p_ref[rows, c * LANES:(c + 1) * LANES] = ps[c].astype(BF16)

    def weighted_values(j, p_ref, alpha_ref):
        off = pl.multiple_of(j * tk, tk)
        acc_ref[...] = alpha_ref[...] * acc_ref[...] + jnp.dot(p_ref[...], v_ref[pl.ds(off, tk), :],
                                                               preferred_element_type=F32)

    bufs = ((s0_ref, p0_ref, alpha0_ref), (s1_ref, p1_ref, alpha1_ref))

    def run(q2, first, n, diagonal_last, next_scores):
        for k in range(n):
            s_ref, p_ref, alpha_ref = bufs[k % 2]
            s_nxt, p_prv, alpha_prv = bufs[(k + 1) % 2]
            weighted_values(jnp.maximum(first + k - 1, 0), p_prv, alpha_prv)
            if k + 1 < n or next_scores:
                scores(q2, first + k + 1, s_nxt)
            softmax(s_ref, p_ref, alpha_ref, 0 if (diagonal_last and k == n - 1) else None)

    scores(q_aug(0), first_ref[hd, first_sub], s0_ref)

    @pl.loop(0, nsub)
    def _(sub):
        i = first_sub + sub
        q2 = q_aug(sub)
        m_ref[...] = jnp.full(m_ref.shape, -jnp.inf, F32)
        l_ref[...] = jnp.zeros(l_ref.shape, F32)
        acc_ref[...] = jnp.zeros(acc_ref.shape, F32)
        p1_ref[...] = jnp.zeros(p1_ref.shape, BF16)
        alpha1_ref[...] = jnp.ones(alpha1_ref.shape, F32)
        j0 = first_ref[hd, i]
        n_full = i - j0

        @pl.loop(0, n_full // UNROLL)
        def _(t):
            run(q2, j0 + UNROLL * t, UNROLL, False, True)

        nxt = jnp.minimum(sub + 1, nsub - 1)
        q2_next = q_aug(nxt)
        j0_next = first_ref[hd, first_sub + nxt]
        for rem in range(UNROLL):
            @pl.when(n_full % UNROLL == rem)
            def _():
                run(q2, i - rem, rem + 1, True, False)
                weighted_values(i, *bufs[rem % 2][1:])
                scores(q2_next, j0_next, s0_ref)
                o_ref[q_rows(sub), :] = (acc_ref[...] / l_ref[...]).astype(o_ref.dtype)


def _first_blocks(stats, nq):
    st = stats.reshape(nq, SUBLANES, LANES)
    qn = jnp.sqrt(st[:, 0, :FOX_HEADS])
    kn = jnp.sqrt(st[:, 1, :FOX_HEADS])
    b_first, b_last = st[:, 2, :FOX_HEADS], st[:, 3, :FOX_HEADS]
    bound = qn[:, None, :] * (kn[None, :, :] + kn[:, None, :]) + b_last[None, :, :] - b_first[:, None, :]
    blocks = jnp.arange(nq, dtype=jnp.int32)
    first_needed = jnp.min(jnp.where(bound <= -SKIP_MARGIN, nq, blocks[None, :, None]), axis=1)
    return jnp.minimum(first_needed, blocks[:, None]).T


def _fox_prompt(q, kp, v, first, tq):
    L = q.shape[0]
    tk = tq
    nsub = min(FLASH_SUBBLOCKS, L // tq)
    return pl.pallas_call(
        functools.partial(_flash_kernel, tq=tq, tk=tk, nsub=nsub),
        grid=(FOX_HEADS, L // (tq * nsub)),
        in_specs=[pl.BlockSpec(memory_space=pltpu.SMEM),
                  pl.BlockSpec((nsub * tq, FOX_HEAD_DIM), lambda h, i: (i, h)),
                  pl.BlockSpec((L, 2 * FOX_HEAD_DIM), lambda h, i: (0, h)),
                  pl.BlockSpec((L, FOX_HEAD_DIM), lambda h, i: (0, h))],
        out_specs=pl.BlockSpec((nsub * tq, FOX_HEAD_DIM), lambda h, i: (i, h)),
        out_shape=jax.ShapeDtypeStruct((L, FOX_HEADS * FOX_HEAD_DIM), BF16),
        scratch_shapes=[pltpu.VMEM((tq, tk), F32)] * 2 + [pltpu.VMEM((tq, tk), BF16)] * 2
        + [pltpu.VMEM((tq, FOX_HEAD_DIM), F32)] * 5,
        compiler_params=_params("parallel", "arbitrary"),
        name="fox_flash",
    )(first, q, kp, v)


def _cached_attn_kernel(q_ref, kn_ref, vn_ref, kc_ref, vc_ref, lpt_ref, lnt_ref, o_ref):
    n = q_ref.shape[0]
    P = kc_ref.shape[0]
    r = lax.broadcasted_iota(jnp.int32, (P, P), 0)
    c = lax.broadcasted_iota(jnp.int32, (P, P), 1)
    G = jnp.dot(lpt_ref[...], (r > c).astype(F32), precision=HIGHEST, preferred_element_type=F32) * LOG2E
    rn = lax.broadcasted_iota(jnp.int32, (n, n), 0)
    cn = lax.broadcasted_iota(jnp.int32, (n, n), 1)
    Hn = jnp.dot(lnt_ref[...], (rn <= cn).astype(F32), precision=HIGHEST, preferred_element_type=F32) * LOG2E
    causal = cn <= rn
    kt = jnp.swapaxes(kc_ref[...], 0, 1)
    vt = jnp.swapaxes(vc_ref[...], 0, 1)
    head = lambda h: slice(h * FOX_HEAD_DIM, (h + 1) * FOX_HEAD_DIM)
    scores = []
    for h in range(FOX_HEADS):
        qh = q_ref[:, head(h)]
        sp = lax.dot_general(qh, kt[h].astype(BF16), _NT, preferred_element_type=F32) + G[h:h + 1, :]
        sn = lax.dot_general(qh, kn_ref[:, head(h)], _NT, preferred_element_type=F32) - Hn[h:h + 1, :]
        scores.append((sp, jnp.where(causal, sn, -jnp.inf)))
    probs = []
    for sp, sn in scores:
        m = jnp.maximum(jnp.max(sp, axis=-1, keepdims=True), jnp.max(sn, axis=-1, keepdims=True))
        pp = jnp.exp2(sp - m)
        pn = jnp.exp2(sn - m)
        l = jnp.sum(pp, axis=-1, keepdims=True) + jnp.sum(pn, axis=-1, keepdims=True)
        probs.append((pp.astype(BF16), pn.astype(BF16), l))
    for h, (pp, pn, l) in enumerate(probs):
        o = (jnp.dot(pp, vt[h].astype(BF16), preferred_element_type=F32)
             + jnp.dot(pn, vn_ref[:, head(h)], preferred_element_type=F32))
        o_ref[:, head(h)] = (o / l).astype(o_ref.dtype)


def _fox_cached(q, kn, vn, kc, vc, lpt, lnt, n):
    B, P = kc.shape[:2]
    W = FOX_HEADS * FOX_HEAD_DIM
    row = pl.BlockSpec((n, W), lambda b: (b, 0))
    cache = pl.BlockSpec((None, P, FOX_HEADS, FOX_HEAD_DIM), lambda b: (b, 0, 0, 0))
    return pl.pallas_call(
        _cached_attn_kernel,
        grid=(B,),
        in_specs=[row, row, row, cache, cache,
                  pl.BlockSpec((None, FOX_HEADS, P), lambda b: (b, 0, 0)),
                  pl.BlockSpec((None, FOX_HEADS, n), lambda b: (b, 0, 0))],
        out_specs=row,
        out_shape=jax.ShapeDtypeStruct((B * n, W), BF16),
        compiler_params=_params("parallel"),
        name="fox_cached",
    )(q, kn, vn, kc, vc, lpt, lnt)


def _ssd_kernel(xbc_ref, z_ref, sm_ref, convp_ref, ssmp_ref, cw_ref, cb_ref, alog_ref, dexp_ref, ng_ref,
                expand_ref, y_ref, convn_ref, ssmn_ref, tail_ref, st_ref, *, Q, nchunks):
    ci = pl.program_id(1)
    PAIR = 2 * SSM_HEAD_DIM
    GW = SSM_STATE

    @pl.when(ci == 0)
    def _():
        st_ref[...] = ssmp_ref[...]
        tail_ref[...] = jnp.zeros((SUBLANES, CONV_DIM), F32)
        tail_ref[SUBLANES - (CONV_WIDTH - 1):SUBLANES, :] = convp_ref[...]

    x3 = xbc_ref[...].reshape(Q // SUBLANES, SUBLANES, CONV_DIM)
    tail = tail_ref[...].reshape(1, SUBLANES, CONV_DIM)
    sub = lax.broadcasted_iota(jnp.int32, x3.shape, 1)
    conv = cb_ref[...]
    for d in range(CONV_WIDTH - 1, 0, -1):
        rot = pltpu.roll(x3, d, 1)
        prev = jnp.concatenate([pltpu.roll(tail, d, 1), rot[:-1]], axis=0)
        conv = conv + jnp.where(sub < d, prev, rot) * cw_ref[CONV_WIDTH - 1 - d:CONV_WIDTH - d, :]
    conv = (conv + x3 * cw_ref[CONV_WIDTH - 1:CONV_WIDTH, :]).reshape(Q, CONV_DIM)
    tail_ref[...] = xbc_ref[Q - SUBLANES:Q, :]
    convn_ref[...] = xbc_ref[Q - (CONV_WIDTH - 1):Q, :]

    xc = _silu(conv)
    xs = xc[:, :SSM_INNER]
    Bm = xc[:, SSM_INNER:SSM_INNER + SSM_GROUPS * GW]
    Cm = xc[:, SSM_INNER + SSM_GROUPS * GW:]

    lane1 = lax.broadcasted_iota(jnp.int32, (Q, LANES), 1)
    head_lanes = (lane1 >= DT_LANE0) & (lane1 < DT_LANE0 + SSM_HEADS)
    dt = jnp.where(head_lanes, sm_ref[...], 0.0)
    a = dt * (-jnp.exp(alog_ref[...]))
    r = lax.broadcasted_iota(jnp.int32, (Q, Q), 0)
    c = lax.broadcasted_iota(jnp.int32, (Q, Q), 1)
    causal = r >= c
    a_cs = jnp.dot(causal.astype(F32), a, precision=HIGHEST, preferred_element_type=F32)
    er = lax.broadcasted_iota(jnp.int32, (SSM_HEADS, LANES), 0)
    ec = lax.broadcasted_iota(jnp.int32, (SSM_HEADS, LANES), 1)
    pick = (ec == er + DT_LANE0).astype(F32)
    acs_t = lax.dot_general(pick, a_cs, _NT, precision=HIGHEST, preferred_element_type=F32)
    a_last = a_cs[Q - 1:Q, :]
    e_last = jnp.exp(a_last)

    def per_channel(x):
        x = jnp.where(head_lanes, x, 0.0)
        hi = x.astype(BF16).astype(F32)
        r1 = x - hi
        mid = r1.astype(BF16).astype(F32)
        packed = hi + pltpu.roll(mid, PIECE_LANES, 1) + pltpu.roll(r1 - mid, 2 * PIECE_LANES, 1)
        return jnp.dot(packed.astype(BF16), expand_ref[...], preferred_element_type=F32)

    xd = xs * per_channel(dt)
    xdb = xd.astype(BF16)
    xw = xd * per_channel(jnp.exp(a_last - a_cs))
    e_cs = per_channel(jnp.exp(a_cs))
    Bb = Bm.astype(BF16)
    Cb = Cm.astype(BF16)

    row_bands = [(0, Q // 2, Q // 2), (Q // 2, Q, Q)] if Q % (2 * LANES) == 0 else [(0, Q, Q)]
    lane_lo = lax.broadcasted_iota(jnp.int32, (Q, PAIR), 1) < SSM_HEAD_DIM
    row_lo = lax.broadcasted_iota(jnp.int32, (PAIR, GW), 0) < SSM_HEAD_DIM

    y_parts = []
    for g in range(SSM_GROUPS):
        Bg = Bb[:, g * GW:(g + 1) * GW]
        Cg = Cb[:, g * GW:(g + 1) * GW]
        cbm = lax.dot_general(Cg, Bg, _NT, preferred_element_type=F32)
        for pr in range(SSM_HEADS // SSM_GROUPS // 2):
            pair = g * (SSM_HEADS // SSM_GROUPS // 2) + pr
            sl = slice(pair * PAIR, (pair + 1) * PAIR)
            s_prev = st_ref[sl, :]
            ys = []
            for hh in (2 * pair, 2 * pair + 1):
                ln = DT_LANE0 + hh
                parts = []
                for r0, r1, nc in row_bands:
                    lm = jnp.exp(jnp.where(causal[r0:r1, :nc], a_cs[r0:r1, ln:ln + 1] - acs_t[hh:hh + 1, :nc],
                                           -jnp.inf))
                    parts.append(jnp.dot((cbm[r0:r1, :nc] * lm).astype(BF16), xdb[:nc, sl],
                                         preferred_element_type=F32))
                ys.append(parts[0] if len(parts) == 1 else jnp.concatenate(parts, axis=0))
            y_off = lax.dot_general(Cg, s_prev.astype(BF16), _NT, preferred_element_type=F32) * e_cs[:, sl]
            ln0 = DT_LANE0 + 2 * pair
            dec = jnp.where(row_lo, e_last[:, ln0:ln0 + 1], e_last[:, ln0 + 1:ln0 + 2])
            st_ref[sl, :] = s_prev * dec + lax.dot_general(xw[:, sl], Bm[:, g * GW:(g + 1) * GW], _TN,
                                                           preferred_element_type=F32)
            y_parts.append(jnp.where(lane_lo, ys[0], ys[1]) + y_off + xs[:, sl] * dexp_ref[:, sl])

    y = jnp.concatenate(y_parts, axis=1) * _silu(z_ref[...].astype(F32))
    gw = SSM_INNER // SSM_GROUPS
    normed = []
    for g in range(SSM_GROUPS):
        yg = y[:, g * gw:(g + 1) * gw]
        normed.append(yg * lax.rsqrt(jnp.mean(yg * yg, axis=-1, keepdims=True) + EPS))
    y_ref[...] = (jnp.concatenate(normed, axis=1) * ng_ref[...]).astype(y_ref.dtype)

    @pl.when(ci == nchunks - 1)
    def _():
        ssmn_ref[...] = st_ref[...]


def _head_expander():
    k = lax.broadcasted_iota(jnp.int32, (LANES, SSM_INNER), 0)
    j = lax.broadcasted_iota(jnp.int32, (LANES, SSM_INNER), 1)
    hit = (k < BIAS_PARTS * PIECE_LANES) & (k % PIECE_LANES == DT_LANE0 + j // SSM_HEAD_DIM)
    return hit.astype(BF16)


def _ssd(xbc, z, small, conv_past, ssm_past, conv_w, conv_b, alog_pad, d_exp, norm_g, batch, Q):
    rows = xbc.shape[0]
    nchunks = rows // batch // Q
    assert Q % SUBLANES == 0 and Q >= SUBLANES and nchunks * Q * batch == rows
    row = lambda w: pl.BlockSpec((Q, w), lambda b, c: (b * nchunks + c, 0))
    state_rows = SSM_HEADS * SSM_HEAD_DIM
    per_seq = lambda a, b_: pl.BlockSpec((None, a, b_), lambda b, c: (b, 0, 0))
    return pl.pallas_call(
        functools.partial(_ssd_kernel, Q=Q, nchunks=nchunks),
        grid=(batch, nchunks),
        in_specs=[row(CONV_DIM), row(SSM_INNER), row(LANES),
                  per_seq(CONV_WIDTH - 1, CONV_DIM), per_seq(state_rows, SSM_STATE),
                  _resident((CONV_WIDTH, CONV_DIM)), _resident((1, CONV_DIM)), _resident((1, LANES)),
                  _resident((1, SSM_INNER)), _resident((1, SSM_INNER)), _resident((LANES, SSM_INNER))],
        out_specs=[row(SSM_INNER), per_seq(CONV_WIDTH - 1, CONV_DIM), per_seq(state_rows, SSM_STATE)],
        out_shape=[jax.ShapeDtypeStruct((rows, SSM_INNER), BF16),
                   jax.ShapeDtypeStruct((batch, CONV_WIDTH - 1, CONV_DIM), F32),
                   jax.ShapeDtypeStruct((batch, state_rows, SSM_STATE), F32)],
        scratch_shapes=[pltpu.VMEM((SUBLANES, CONV_DIM), F32), pltpu.VMEM((state_rows, SSM_STATE), F32)],
        compiler_params=_params("parallel", "arbitrary"),
        name="ssd_scan",
    )(xbc, z, small, conv_past, ssm_past, conv_w, conv_b, alog_pad, d_exp, norm_g, _head_expander())


def _outffn_kernel(ya_ref, ys_ref, ga_ref, gb_ref, x_ref, gt1_ref, sc2_ref, sh2_ref, gt2_ref,
                   gpm_ref, gpf_ref, gqf_ref, wo_ref, wu_ref, wd_ref, o_ref):
    merged = (_sigmoid(ga_ref[...].astype(F32)) * ya_ref[...].astype(F32)
              + _sigmoid(gb_ref[...].astype(F32)) * ys_ref[...].astype(F32))
    m = jnp.dot(merged.astype(BF16), wo_ref[...], preferred_element_type=F32)
    x1 = x_ref[...] + gt1_ref[...] * _rms(m, gpm_ref[...])
    hb = (_rms(x1, gpf_ref[...]) * (1.0 + sc2_ref[...]) + sh2_ref[...]).astype(BF16)
    f = jnp.zeros(x1.shape, F32)
    for cc in range(D_FF // D_MODEL):
        sl = slice(cc * D_MODEL, (cc + 1) * D_MODEL)
        up = jnp.dot(hb, wu_ref[:, sl], preferred_element_type=F32)
        act = jnp.square(jnp.maximum(up, 0.0)).astype(BF16)
        f = f + jnp.dot(act, wd_ref[sl, :], preferred_element_type=F32)
    o_ref[...] = x1 + gt2_ref[...] * _rms(f, gqf_ref[...])


def _out_ffn(ya, ys, ga, gb, x2d, gt1, sc2, sh2, gt2, g_post_mix, g_pre_ffn, g_post_ffn, w_out, w_up, w_down, tm):
    rows = x2d.shape[0]
    row = pl.BlockSpec((tm, D_MODEL), lambda i: (i, 0))
    vec = _resident((1, D_MODEL))
    return pl.pallas_call(
        _outffn_kernel,
        grid=(rows // tm,),
        in_specs=[row, row, row, row, row,
                  _mod_spec(gt1, tm), _mod_spec(sc2, tm), _mod_spec(sh2, tm), _mod_spec(gt2, tm),
                  vec, vec, vec,
                  _resident((D_MODEL, D_MODEL)), _resident((D_MODEL, D_FF)), _resident((D_FF, D_MODEL))],
        out_specs=row,
        out_shape=jax.ShapeDtypeStruct((rows, D_MODEL), F32),
        compiler_params=_params("parallel"),
        name="out_ffn",
    )(ya, ys, ga, gb, x2d, gt1, sc2, sh2, gt2, g_post_mix, g_pre_ffn, g_post_ffn, w_out, w_up, w_down)


def _layer(x, mod, past, wts, *, tm, tq, Q):
    b, L, _ = x.shape
    x2d = x.reshape(b * L, D_MODEL)
    sh1, sc1, gt1, sh2, sc2, gt2 = mod
    q, kf, vf, kb, vb, z, xbc, ga, gb, small, *stats = _in_proj(
        x2d, wts["g_pre_mix"], sc1, sh1, wts["w_qkv"], wts["w_ssm"], wts["w_gate"], wts["w_small"], wts["b_small"],
        tm, past is None)

    if past is None:
        assert b == 1 and tm == tq, "block statistics are per projection row block"
        y_att = _fox_prompt(q, kb, vb, _first_blocks(stats[0], L // tq), tq)
        conv_past = jnp.zeros((b, CONV_WIDTH - 1, CONV_DIM), F32)
        ssm_past = jnp.zeros((b, SSM_INNER, SSM_STATE), F32)
    else:
        k_past, v_past, logf_past, conv_past, ssm_past = past
        P = k_past.shape[1]
        lpt = jnp.swapaxes(logf_past, 1, 2)
        lnt = jnp.swapaxes(small[:, :FOX_HEADS].reshape(b, L, FOX_HEADS), 1, 2)
        y_att = _fox_cached(q, kb, vb, k_past, v_past, lpt, lnt, L)
        ssm_past = ssm_past.reshape(b, SSM_INNER, SSM_STATE)

    y_ssm, conv_new, ssm_new = _ssd(xbc, z, small, conv_past, ssm_past, wts["conv_w"], wts["conv_b"],
                                    wts["alog_pad"], wts["d_exp"], wts["ssm_norm_g"], b, Q)
    y = _out_ffn(y_att, y_ssm, ga, gb, x2d, gt1, sc2, sh2, gt2, wts["g_post_mix"], wts["g_pre_ffn"],
                 wts["g_post_ffn"], wts["w_out"], wts["w_up"], wts["w_down"], tm)
    return (y.reshape(b, L, D_MODEL),
            kf.reshape(b, L, FOX_HEADS, FOX_HEAD_DIM), vf.reshape(b, L, FOX_HEADS, FOX_HEAD_DIM),
            small[:, :FOX_HEADS].reshape(b, L, FOX_HEADS), conv_new,
            ssm_new.reshape(b, SSM_HEADS, SSM_HEAD_DIM, SSM_STATE))


def _prep_weights(w_ada, b_ada, g_pre_mix, g_post_mix, g_pre_ffn, g_post_ffn, w_in, b_f, conv_w, conv_b,
                  dt_bias, a_log, d_skip, ssm_norm_g, w_out, w_up, w_down):
    sizes = (D_MODEL, D_MODEL, D_MODEL, FOX_HEADS, SSM_INNER, CONV_DIM, SSM_HEADS, D_MODEL, D_MODEL)
    offs = [0]
    for s in sizes:
        offs.append(offs[-1] + s)
    piece = lambda i: w_in[:, offs[i]:offs[i + 1]]
    pad = LANES - FOX_HEADS - SSM_HEADS
    row = lambda v: v.reshape(1, -1).astype(F32)
    return {
        "w_qkv": w_in[:, offs[0]:offs[3]].astype(BF16),
        "w_ssm": w_in[:, offs[4]:offs[6]].astype(BF16),
        "w_gate": w_in[:, offs[7]:offs[9]].astype(BF16),
        "w_small": jnp.pad(jnp.concatenate([piece(3), piece(6)], axis=1), ((0, 0), (0, pad))).astype(BF16),
        "b_small": jnp.pad(jnp.concatenate([b_f, dt_bias]), (0, pad)).reshape(1, LANES).astype(F32),
        "alog_pad": jnp.pad(a_log, (DT_LANE0, LANES - DT_LANE0 - SSM_HEADS)).reshape(1, LANES).astype(F32),
        "d_exp": jnp.repeat(d_skip, SSM_HEAD_DIM).reshape(1, SSM_INNER).astype(F32),
        "g_pre_mix": row(g_pre_mix), "g_post_mix": row(g_post_mix),
        "g_pre_ffn": row(g_pre_ffn), "g_post_ffn": row(g_post_ffn),
        "conv_w": conv_w.astype(F32), "conv_b": row(conv_b), "ssm_norm_g": row(ssm_norm_g),
        "w_out": w_out.astype(BF16), "w_up": w_up.astype(BF16), "w_down": w_down.astype(BF16),
    }


def _forward(x_prompt, x_sample, c_prompt, c_sample, cache_fox_k, cache_fox_v, cache_fox_logf,
             state_ssm_conv, state_ssm, w_ada, b_ada, *layer_w, tm_prompt, tq, q_prompt):
    depth = w_ada.shape[0]
    bp, Lp, _ = x_prompt.shape
    bs, Ls, _ = x_sample.shape
    yp, ys = x_prompt, x_sample
    outs_p, outs_s = [], []
    for i in range(depth):
        wts = _prep_weights(w_ada[i], b_ada[i], *[w[i] for w in layer_w])
        mod = _ada(jnp.concatenate([c_prompt, c_sample], axis=0), w_ada[i], b_ada[i].reshape(1, -1))
        mod_p = [m for m in jnp.split(mod[:bp], N_MOD, axis=-1)]
        mod_s = [jnp.repeat(m, Ls, axis=0) for m in jnp.split(mod[bp:], N_MOD, axis=-1)]
        rp = _layer(yp, mod_p, None, wts, tm=tm_prompt, tq=tq, Q=q_prompt)
        rs = _layer(ys, mod_s, (cache_fox_k[i], cache_fox_v[i], cache_fox_logf[i], state_ssm_conv[i], state_ssm[i]),
                    wts, tm=bs * Ls, tq=None, Q=Ls)
        yp, ys = rp[0], rs[0]
        outs_p.append(rp[1:])
        outs_s.append(rs[1:])
    stack = lambda outs, j: jnp.stack([o[j] for o in outs])
    return (yp, ys) + tuple(stack(outs_p, j) for j in range(5)) + tuple(stack(outs_s, j) for j in range(5))


def kernel(x_prompt, x_sample, c_prompt, c_sample, cache_fox_k, cache_fox_v, cache_fox_logf, state_ssm_conv,
           state_ssm, w_ada, b_ada, g_pre_mix, g_post_mix, g_pre_ffn, g_post_ffn, w_in, b_f, conv_w, conv_b,
           dt_bias, a_log, d_skip, ssm_norm_g, w_out, w_up, w_down):
    assert x_prompt.shape[0] == 1, "the prompt path carries one sequence"
    L = x_prompt.shape[1]
    return _forward(x_prompt, x_sample, c_prompt, c_sample, cache_fox_k, cache_fox_v, cache_fox_logf,
                    state_ssm_conv, state_ssm, w_ada, b_ada, g_pre_mix, g_post_mix, g_pre_ffn, g_post_ffn,
                    w_in, b_f, conv_w, conv_b, dt_bias, a_log, d_skip, ssm_norm_g, w_out, w_up, w_down,
                    tm_prompt=min(512, L), tq=min(512, L), q_prompt=min(256, L))
```

```python
import functools

import jax
import jax.numpy as jnp
from jax import lax
from jax.experimental import pallas as pl
from jax.experimental.pallas import tpu as pltpu

F32 = jnp.float32
BF16 = jnp.bfloat16
HIGHEST = lax.Precision.HIGHEST

D_MODEL = 1024
FOX_HEADS = 8
FOX_HEAD_DIM = 128
FOX_SCALE = FOX_HEAD_DIM ** -0.5
SSM_HEADS = 16
SSM_HEAD_DIM = 64
SSM_GROUPS = 4
SSM_STATE = 128
SSM_INNER = SSM_HEADS * SSM_HEAD_DIM
CONV_WIDTH = 4
CONV_DIM = SSM_INNER + 2 * SSM_GROUPS * SSM_STATE
D_FF = 4 * D_MODEL
N_MOD = 6
EPS = 1e-6

LANES = 128
SUBLANES = 8
DT_LANE0 = FOX_HEADS
VMEM_LIMIT = 56 * 1024 * 1024
LOG2E = 1.4426950408889634
BIAS_PARTS = 3
PIECE_LANES = 32
CUM_ROWS = 128
ROW_TILE = 16
UNROLL = 8
FLASH_SUBBLOCKS = 4
SKIP_MARGIN = 152.0


_NT = (((1,), (1,)), ((), ()))
_TN = (((0,), (0,)), ((), ()))


def _params(*sem):
    return pltpu.CompilerParams(dimension_semantics=sem, vmem_limit_bytes=VMEM_LIMIT)


def _resident(shape):
    zeros = (0,) * len(shape)
    return pl.BlockSpec(shape, lambda *_: zeros, pipeline_mode=pl.Buffered(1))


def _rms(x, g):
    return x * lax.rsqrt(jnp.mean(x * x, axis=-1, keepdims=True) + EPS) * g


def _sigmoid(x):
    return 0.5 * jnp.tanh(0.5 * x) + 0.5


def _silu(x):
    h = 0.5 * x
    return h * jnp.tanh(h) + h


def _ada_kernel(c_ref, w_ref, b_ref, o_ref):
    s = _silu(c_ref[...]).astype(BF16)
    o_ref[...] = jnp.dot(s, w_ref[...].astype(BF16), preferred_element_type=F32) + b_ref[...]


def _ada(c_all, w_ada, b_ada):
    rows = c_all.shape[0]
    width = w_ada.shape[1]
    tn = 1536
    return pl.pallas_call(
        _ada_kernel,
        grid=(width // tn,),
        in_specs=[pl.BlockSpec((rows, D_MODEL), lambda j: (0, 0)),
                  pl.BlockSpec((D_MODEL, tn), lambda j: (0, j)),
                  pl.BlockSpec((1, tn), lambda j: (0, j))],
        out_specs=pl.BlockSpec((rows, tn), lambda j: (0, j)),
        out_shape=jax.ShapeDtypeStruct((rows, width), F32),
        compiler_params=_params("parallel"),
        name="ada_mod",
    )(c_all, w_ada, b_ada)


def _inproj_kernel(x_ref, g_ref, sc_ref, sh_ref, wqkv_ref, wssm_ref, wgate_ref, ws_ref, bs_ref, *rest, with_bias):
    if with_bias:
        (sel_ref, q_ref, kf_ref, vf_ref, kb_ref, vb_ref, z_ref, xbc_ref, ga_ref, gb_ref, sm_ref, stats_ref,
         carry_ref) = rest
    else:
        q_ref, kf_ref, vf_ref, kb_ref, vb_ref, z_ref, xbc_ref, ga_ref, gb_ref, sm_ref = rest
    h = _rms(x_ref[...], g_ref[...]) * (1.0 + sc_ref[...]) + sh_ref[...]
    hb = h.astype(BF16)

    def proj(w_ref, lo, width):
        return jnp.dot(hb, w_ref[:, lo:lo + width], preferred_element_type=F32)

    u = jnp.dot(hb, ws_ref[...], preferred_element_type=F32) + bs_ref[...]
    t = jnp.log1p(jnp.exp(-jnp.abs(u)))
    lane = lax.broadcasted_iota(jnp.int32, u.shape, 1)
    sm = jnp.where(lane < DT_LANE0, jnp.minimum(u, 0.0) - t,
                   jnp.where(lane < DT_LANE0 + SSM_HEADS, jnp.maximum(u, 0.0) + t, 0.0))
    sm_ref[...] = sm

    qb = (proj(wqkv_ref, 0, D_MODEL) * (FOX_SCALE * LOG2E)).astype(BF16)
    q_ref[...] = qb
    k = proj(wqkv_ref, D_MODEL, D_MODEL)
    kf_ref[...] = k.reshape(k.shape[0], FOX_HEADS, FOX_HEAD_DIM)
    if with_bias:
        @pl.when(pl.program_id(0) == 0)
        def _():
            carry_ref[...] = jnp.zeros_like(carry_ref)

        r = lax.broadcasted_iota(jnp.int32, (CUM_ROWS, CUM_ROWS), 0)
        c = lax.broadcasted_iota(jnp.int32, (CUM_ROWS, CUM_ROWS), 1)
        tri = (r >= c).astype(F32)
        run = carry_ref[0:1, :]
        parts = []
        for b in range(sm.shape[0] // CUM_ROWS):
            cs = jnp.dot(tri, sm[b * CUM_ROWS:(b + 1) * CUM_ROWS, :], precision=HIGHEST,
                         preferred_element_type=F32) + run
            run = cs[CUM_ROWS - 1:CUM_ROWS, :]
            parts.append(cs)
        carry_ref[0:1, :] = run
        nb = jnp.concatenate(parts, axis=0) * (-LOG2E)
        hi = nb.astype(BF16)
        r1 = nb - hi.astype(F32)
        mid = r1.astype(BF16)
        lo = (r1 - mid.astype(F32)).astype(BF16)
        aug = jnp.dot(jnp.concatenate([hi, mid, lo], axis=1), sel_ref[...], preferred_element_type=F32)
        kb = k.astype(BF16)
        for hd in range(FOX_HEADS):
            sl = slice(hd * FOX_HEAD_DIM, (hd + 1) * FOX_HEAD_DIM)
            kb_ref[:, 2 * hd * FOX_HEAD_DIM:(2 * hd + 1) * FOX_HEAD_DIM] = kb[:, sl]
            kb_ref[:, (2 * hd + 1) * FOX_HEAD_DIM:(2 * hd + 2) * FOX_HEAD_DIM] = aug[:, sl].astype(BF16)

        def max_sq_norms(xb):
            x2 = xb.astype(F32)
            x2 = x2 * x2
            out = jnp.zeros((1, LANES), F32)
            for hd in range(FOX_HEADS):
                n2 = jnp.sum(x2[:, hd * FOX_HEAD_DIM:(hd + 1) * FOX_HEAD_DIM], axis=-1, keepdims=True)
                out = jnp.where(lane[0:1, :] == hd, jnp.max(n2, axis=0, keepdims=True), out)
            return out

        stats_ref[...] = jnp.zeros(stats_ref.shape, F32)
        stats_ref[0:1, :] = max_sq_norms(qb)
        stats_ref[1:2, :] = max_sq_norms(kb)
        stats_ref[2:3, :] = nb[0:1, :]
        stats_ref[3:4, :] = nb[nb.shape[0] - 1:, :]
    else:
        kb_ref[...] = k.astype(BF16)
    v = proj(wqkv_ref, 2 * D_MODEL, D_MODEL)
    vf_ref[...] = v.reshape(v.shape[0], FOX_HEADS, FOX_HEAD_DIM)
    vb_ref[...] = v.astype(BF16)
    z_ref[...] = proj(wssm_ref, 0, SSM_INNER).astype(BF16)
    xbc_ref[...] = proj(wssm_ref, SSM_INNER, CONV_DIM)
    ga_ref[...] = proj(wgate_ref, 0, D_MODEL).astype(BF16)
    gb_ref[...] = proj(wgate_ref, D_MODEL, D_MODEL).astype(BF16)


def _mod_spec(arr, tm):
    if arr.shape[0] == 1:
        return pl.BlockSpec((1, D_MODEL), lambda i: (0, 0))
    return pl.BlockSpec((tm, D_MODEL), lambda i: (i, 0))


def _bias_selector():
    r = lax.broadcasted_iota(jnp.int32, (BIAS_PARTS * LANES, FOX_HEADS * FOX_HEAD_DIM), 0)
    c = lax.broadcasted_iota(jnp.int32, (BIAS_PARTS * LANES, FOX_HEADS * FOX_HEAD_DIM), 1)
    hit = (r % LANES < FOX_HEADS) & (c == (r % LANES) * FOX_HEAD_DIM + r // LANES)
    return hit.astype(BF16)


def _in_proj(x2d, g, sc, sh, w_qkv, w_ssm, w_gate, w_small, b_small, tm, with_bias):
    rows = x2d.shape[0]
    assert not with_bias or tm % CUM_ROWS == 0
    row = lambda w: pl.BlockSpec((tm, w), lambda i: (i, 0))
    shp = lambda w, dt: jax.ShapeDtypeStruct((rows, w), dt)
    kb_width = 2 * D_MODEL if with_bias else D_MODEL
    in_specs = [row(D_MODEL), _resident((1, D_MODEL)), _mod_spec(sc, tm), _mod_spec(sh, tm),
                _resident(w_qkv.shape), _resident(w_ssm.shape), _resident(w_gate.shape),
                _resident((D_MODEL, LANES)), _resident((1, LANES))]
    args = [x2d, g, sc, sh, w_qkv, w_ssm, w_gate, w_small, b_small]
    heads = pl.BlockSpec((tm, FOX_HEADS, FOX_HEAD_DIM), lambda i: (i, 0, 0))
    heads_shape = jax.ShapeDtypeStruct((rows, FOX_HEADS, FOX_HEAD_DIM), F32)
    out_specs = [row(D_MODEL), heads, heads, row(kb_width), row(D_MODEL),
                 row(SSM_INNER), row(CONV_DIM), row(D_MODEL), row(D_MODEL), row(LANES)]
    out_shape = [shp(D_MODEL, BF16), heads_shape, heads_shape, shp(kb_width, BF16),
                 shp(D_MODEL, BF16), shp(SSM_INNER, BF16), shp(CONV_DIM, F32), shp(D_MODEL, BF16),
                 shp(D_MODEL, BF16), shp(LANES, F32)]
    if with_bias:
        in_specs.append(_resident((BIAS_PARTS * LANES, FOX_HEADS * FOX_HEAD_DIM)))
        args.append(_bias_selector())
        out_specs.append(pl.BlockSpec((SUBLANES, LANES), lambda i: (i, 0)))
        out_shape.append(jax.ShapeDtypeStruct((rows // tm * SUBLANES, LANES), F32))
    return pl.pallas_call(
        functools.partial(_inproj_kernel, with_bias=with_bias),
        grid=(rows // tm,),
        in_specs=in_specs,
        out_specs=out_specs,
        out_shape=out_shape,
        scratch_shapes=[pltpu.VMEM((SUBLANES, LANES), F32)] if with_bias else [],
        compiler_params=_params("arbitrary" if with_bias else "parallel"),
        name="in_proj",
    )(*args)


def _flash_kernel(first_ref, q_ref, k_ref, v_ref, o_ref, s0_ref, s1_ref, p0_ref, p1_ref, m_ref, l_ref, alpha0_ref,
                  alpha1_ref, acc_ref, *, tq, tk, nsub):
    hd = pl.program_id(0)
    first_sub = pl.program_id(1) * nsub
    nlt = tk // LANES
    lane = lax.broadcasted_iota(jnp.int32, (tq, FOX_HEAD_DIM), 1)
    ones = jnp.where(lane < BIAS_PARTS, 1.0, 0.0).astype(BF16)

    def q_rows(sub):
        return pl.ds(pl.multiple_of(sub * tq, tq), tq)

    def q_aug(sub):
        return jnp.concatenate([q_ref[q_rows(sub), :], ones], axis=1)

    def scores(q2, j, s_ref):
        off = pl.multiple_of(j * tk, tk)
        s_ref[...] = lax.dot_general(q2, k_ref[pl.ds(off, tk), :], _NT, preferred_element_type=F32)

    def softmax(s_ref, p_ref, alpha_ref, mask_shift):
        for rt in range(tq // ROW_TILE):
            rows = slice(rt * ROW_TILE, (rt + 1) * ROW_TILE)
            cols = [s_ref[rows, c * LANES:(c + 1) * LANES] for c in range(nlt)]
            if mask_shift is not None:
                ri = lax.broadcasted_iota(jnp.int32, (ROW_TILE, LANES), 0) + rt * ROW_TILE
                ci = lax.broadcasted_iota(jnp.int32, (ROW_TILE, LANES), 1) + mask_shift
                cols = [jnp.where(ci + c * LANES <= ri, cols[c], -jnp.inf) for c in range(nlt)]
            mx = functools.reduce(jnp.maximum, cols)
            m_old = m_ref[rows, :]
            m_new = jnp.maximum(m_old, jnp.max(mx, axis=-1, keepdims=True))
            alpha = jnp.exp2(m_old - m_new)
            ps = [jnp.exp2(cc - m_new) for cc in cols]
            l_ref[rows, :] = alpha * l_ref[rows, :] + jnp.sum(functools.reduce(jnp.add, ps), axis=-1, keepdims=True)
            m_ref[rows, :] = m_new
            alpha_ref[rows, :] = alpha
            for c in range(nlt):
                p_ref[rows, c * LANES:(c + 1) * LANES] = ps[c].astype(BF16)

    def weighted_values(j, p_ref, alpha_ref):
        off = pl.multiple_of(j * tk, tk)
        acc_ref[...] = alpha_ref[...] * acc_ref[...] + jnp.dot(p_ref[...], v_ref[pl.ds(off, tk), :],
                                                               preferred_element_type=F32)

    bufs = ((s0_ref, p0_ref, alpha0_ref), (s1_ref, p1_ref, alpha1_ref))

    def run(q2, first, n, diagonal_last, next_scores):
        for k in range(n):
            s_ref, p_ref, alpha_ref = bufs[k % 2]
            s_nxt, p_prv, alpha_prv = bufs[(k + 1) % 2]
            weighted_values(jnp.maximum(first + k - 1, 0), p_prv, alpha_prv)
            if k + 1 < n or next_scores:
                scores(q2, first + k + 1, s_nxt)
            softmax(s_ref, p_ref, alpha_ref, 0 if (diagonal_last and k == n - 1) else None)

    scores(q_aug(0), first_ref[hd, first_sub], s0_ref)

    @pl.loop(0, nsub)
    def _(sub):
        i = first_sub + sub
        q2 = q_aug(sub)
        m_ref[...] = jnp.full(m_ref.shape, -jnp.inf, F32)
        l_ref[...] = jnp.zeros(l_ref.shape, F32)
        acc_ref[...] = jnp.zeros(acc_ref.shape, F32)
        p1_ref[...] = jnp.zeros(p1_ref.shape, BF16)
        alpha1_ref[...] = jnp.ones(alpha1_ref.shape, F32)
        j0 = first_ref[hd, i]
        n_full = i - j0

        @pl.loop(0, n_full // UNROLL)
        def _(t):
            run(q2, j0 + UNROLL * t, UNROLL, False, True)

        nxt = jnp.minimum(sub + 1, nsub - 1)
        q2_next = q_aug(nxt)
        j0_next = first_ref[hd, first_sub + nxt]
        for rem in range(UNROLL):
            @pl.when(n_full % UNROLL == rem)
            def _():
                run(q2, i - rem, rem + 1, True, False)
                weighted_values(i, *bufs[rem % 2][1:])
                scores(q2_next, j0_next, s0_ref)
                o_ref[q_rows(sub), :] = (acc_ref[...] / l_ref[...]).astype(o_ref.dtype)


def _first_blocks(stats, nq):
    st = stats.reshape(nq, SUBLANES, LANES)
    qn = jnp.sqrt(st[:, 0, :FOX_HEADS])
    kn = jnp.sqrt(st[:, 1, :FOX_HEADS])
    b_first, b_last = st[:, 2, :FOX_HEADS], st[:, 3, :FOX_HEADS]
    bound = qn[:, None, :] * (kn[None, :, :] + kn[:, None, :]) + b_last[None, :, :] - b_first[:, None, :]
    blocks = jnp.arange(nq, dtype=jnp.int32)
    first_needed = jnp.min(jnp.where(bound <= -SKIP_MARGIN, nq, blocks[None, :, None]), axis=1)
    return jnp.minimum(first_needed, blocks[:, None]).T


def _fox_prompt(q, kp, v, first, tq):
    L = q.shape[0]
    tk = tq
    nsub = min(FLASH_SUBBLOCKS, L // tq)
    return pl.pallas_call(
        functools.partial(_flash_kernel, tq=tq, tk=tk, nsub=nsub),
        grid=(FOX_HEADS, L // (tq * nsub)),
        in_specs=[pl.BlockSpec(memory_space=pltpu.SMEM),
                  pl.BlockSpec((nsub * tq, FOX_HEAD_DIM), lambda h, i: (i, h)),
                  pl.BlockSpec((L, 2 * FOX_HEAD_DIM), lambda h, i: (0, h)),
                  pl.BlockSpec((L, FOX_HEAD_DIM), lambda h, i: (0, h))],
        out_specs=pl.BlockSpec((nsub * tq, FOX_HEAD_DIM), lambda h, i: (i, h)),
        out_shape=jax.ShapeDtypeStruct((L, FOX_HEADS * FOX_HEAD_DIM), BF16),
        scratch_shapes=[pltpu.VMEM((tq, tk), F32)] * 2 + [pltpu.VMEM((tq, tk), BF16)] * 2
        + [pltpu.VMEM((tq, FOX_HEAD_DIM), F32)] * 5,
        compiler_params=_params("parallel", "arbitrary"),
        name="fox_flash",
    )(first, q, kp, v)


def _cached_attn_kernel(q_ref, kn_ref, vn_ref, kc_ref, vc_ref, lpt_ref, lnt_ref, o_ref):
    n = q_ref.shape[0]
    P = kc_ref.shape[0]
    r = lax.broadcasted_iota(jnp.int32, (P, P), 0)
    c = lax.broadcasted_iota(jnp.int32, (P, P), 1)
    G = jnp.dot(lpt_ref[...], (r > c).astype(F32), precision=HIGHEST, preferred_element_type=F32) * LOG2E
    rn = lax.broadcasted_iota(jnp.int32, (n, n), 0)
    cn = lax.broadcasted_iota(jnp.int32, (n, n), 1)
    Hn = jnp.dot(lnt_ref[...], (rn <= cn).astype(F32), precision=HIGHEST, preferred_element_type=F32) * LOG2E
    causal = cn <= rn
    kt = jnp.swapaxes(kc_ref[...], 0, 1)
    vt = jnp.swapaxes(vc_ref[...], 0, 1)
    head = lambda h: slice(h * FOX_HEAD_DIM, (h + 1) * FOX_HEAD_DIM)
    scores = []
    for h in range(FOX_HEADS):
        qh = q_ref[:, head(h)]
        sp = lax.dot_general(qh, kt[h].astype(BF16), _NT, preferred_element_type=F32) + G[h:h + 1, :]
        sn = lax.dot_general(qh, kn_ref[:, head(h)], _NT, preferred_element_type=F32) - Hn[h:h + 1, :]
        scores.append((sp, jnp.where(causal, sn, -jnp.inf)))
    probs = []
    for sp, sn in scores:
        m = jnp.maximum(jnp.max(sp, axis=-1, keepdims=True), jnp.max(sn, axis=-1, keepdims=True))
        pp = jnp.exp2(sp - m)
        pn = jnp.exp2(sn - m)
        l = jnp.sum(pp, axis=-1, keepdims=True) + jnp.sum(pn, axis=-1, keepdims=True)
        probs.append((pp.astype(BF16), pn.astype(BF16), l))
    for h, (pp, pn, l) in enumerate(probs):
        o = (jnp.dot(pp, vt[h].astype(BF16), preferred_element_type=F32)
             + jnp.dot(pn, vn_ref[:, head(h)], preferred_element_type=F32))
        o_ref[:, head(h)] = (o / l).astype(o_ref.dtype)


def _fox_cached(q, kn, vn, kc, vc, lpt, lnt, n):
    B, P = kc.shape[:2]
    W = FOX_HEADS * FOX_HEAD_DIM
    row = pl.BlockSpec((n, W), lambda b: (b, 0))
    cache = pl.BlockSpec((None, P, FOX_HEADS, FOX_HEAD_DIM), lambda b: (b, 0, 0, 0))
    return pl.pallas_call(
        _cached_attn_kernel,
        grid=(B,),
        in_specs=[row, row, row, cache, cache,
                  pl.BlockSpec((None, FOX_HEADS, P), lambda b: (b, 0, 0)),
                  pl.BlockSpec((None, FOX_HEADS, n), lambda b: (b, 0, 0))],
        out_specs=row,
        out_shape=jax.ShapeDtypeStruct((B * n, W), BF16),
        compiler_params=_params("parallel"),
        name="fox_cached",
    )(q, kn, vn, kc, vc, lpt, lnt)


def _ssd_kernel(xbc_ref, z_ref, sm_ref, convp_ref, ssmp_ref, cw_ref, cb_ref, alog_ref, dexp_ref, ng_ref,
                expand_ref, y_ref, convn_ref, ssmn_ref, tail_ref, st_ref, *, Q, nchunks):
    ci = pl.program_id(1)
    PAIR = 2 * SSM_HEAD_DIM
    GW = SSM_STATE

    @pl.when(ci == 0)
    def _():
        st_ref[...] = ssmp_ref[...]
        tail_ref[...] = jnp.zeros((SUBLANES, CONV_DIM), F32)
        tail_ref[SUBLANES - (CONV_WIDTH - 1):SUBLANES, :] = convp_ref[...]

    x3 = xbc_ref[...].reshape(Q // SUBLANES, SUBLANES, CONV_DIM)
    tail = tail_ref[...].reshape(1, SUBLANES, CONV_DIM)
    sub = lax.broadcasted_iota(jnp.int32, x3.shape, 1)
    conv = cb_ref[...]
    for d in range(CONV_WIDTH - 1, 0, -1):
        rot = pltpu.roll(x3, d, 1)
        prev = jnp.concatenate([pltpu.roll(tail, d, 1), rot[:-1]], axis=0)
        conv = conv + jnp.where(sub < d, prev, rot) * cw_ref[CONV_WIDTH - 1 - d:CONV_WIDTH - d, :]
    conv = (conv + x3 * cw_ref[CONV_WIDTH - 1:CONV_WIDTH, :]).reshape(Q, CONV_DIM)
    tail_ref[...] = xbc_ref[Q - SUBLANES:Q, :]
    convn_ref[...] = xbc_ref[Q - (CONV_WIDTH - 1):Q, :]

    xc = _silu(conv)
    xs = xc[:, :SSM_INNER]
    Bm = xc[:, SSM_INNER:SSM_INNER + SSM_GROUPS * GW]
    Cm = xc[:, SSM_INNER + SSM_GROUPS * GW:]

    lane1 = lax.broadcasted_iota(jnp.int32, (Q, LANES), 1)
    head_lanes = (lane1 >= DT_LANE0) & (lane1 < DT_LANE0 + SSM_HEADS)
    dt = jnp.where(head_lanes, sm_ref[...], 0.0)
    a = dt * (-jnp.exp(alog_ref[...]))
    r = lax.broadcasted_iota(jnp.int32, (Q, Q), 0)
    c = lax.broadcasted_iota(jnp.int32, (Q, Q), 1)
    causal = r >= c
    a_cs = jnp.dot(causal.astype(F32), a, precision=HIGHEST, preferred_element_type=F32)
    er = lax.broadcasted_iota(jnp.int32, (SSM_HEADS, LANES), 0)
    ec = lax.broadcasted_iota(jnp.int32, (SSM_HEADS, LANES), 1)
    pick = (ec == er + DT_LANE0).astype(F32)
    acs_t = lax.dot_general(pick, a_cs, _NT, precision=HIGHEST, preferred_element_type=F32)
    a_last = a_cs[Q - 1:Q, :]
    e_last = jnp.exp(a_last)

    def per_channel(x):
        x = jnp.where(head_lanes, x, 0.0)
        hi = x.astype(BF16).astype(F32)
        r1 = x - hi
        mid = r1.astype(BF16).astype(F32)
        packed = hi + pltpu.roll(mid, PIECE_LANES, 1) + pltpu.roll(r1 - mid, 2 * PIECE_LANES, 1)
        return jnp.dot(packed.astype(BF16), expand_ref[...], preferred_element_type=F32)

    xd = xs * per_channel(dt)
    xdb = xd.astype(BF16)
    xw = xd * per_channel(jnp.exp(a_last - a_cs))
    e_cs = per_channel(jnp.exp(a_cs))
    Bb = Bm.astype(BF16)
    Cb = Cm.astype(BF16)

    row_bands = [(0, Q // 2, Q // 2), (Q // 2, Q, Q)] if Q % (2 * LANES) == 0 else [(0, Q, Q)]
    lane_lo = lax.broadcasted_iota(jnp.int32, (Q, PAIR), 1) < SSM_HEAD_DIM
    row_lo = lax.broadcasted_iota(jnp.int32, (PAIR, GW), 0) < SSM_HEAD_DIM

    y_parts = []
    for g in range(SSM_GROUPS):
        Bg = Bb[:, g * GW:(g + 1) * GW]
        Cg = Cb[:, g * GW:(g + 1) * GW]
        cbm = lax.dot_general(Cg, Bg, _NT, preferred_element_type=F32)
        for pr in range(SSM_HEADS // SSM_GROUPS // 2):
            pair = g * (SSM_HEADS // SSM_GROUPS // 2) + pr
            sl = slice(pair * PAIR, (pair + 1) * PAIR)
            s_prev = st_ref[sl, :]
            ys = []
            for hh in (2 * pair, 2 * pair + 1):
                ln = DT_LANE0 + hh
                parts = []
                for r0, r1, nc in row_bands:
                    lm = jnp.exp(jnp.where(causal[r0:r1, :nc], a_cs[r0:r1, ln:ln + 1] - acs_t[hh:hh + 1, :nc],
                                           -jnp.inf))
                    parts.append(jnp.dot((cbm[r0:r1, :nc] * lm).astype(BF16), xdb[:nc, sl],
                                         preferred_element_type=F32))
                ys.append(parts[0] if len(parts) == 1 else jnp.concatenate(parts, axis=0))
            y_off = lax.dot_general(Cg, s_prev.astype(BF16), _NT, preferred_element_type=F32) * e_cs[:, sl]
            ln0 = DT_LANE0 + 2 * pair
            dec = jnp.where(row_lo, e_last[:, ln0:ln0 + 1], e_last[:, ln0 + 1:ln0 + 2])
            st_ref[sl, :] = s_prev * dec + lax.dot_general(xw[:, sl], Bm[:, g * GW:(g + 1) * GW], _TN,
                                                           preferred_element_type=F32)
            y_parts.append(jnp.where(lane_lo, ys[0], ys[1]) + y_off + xs[:, sl] * dexp_ref[:, sl])

    y = jnp.concatenate(y_parts, axis=1) * _silu(z_ref[...].astype(F32))
    gw = SSM_INNER // SSM_GROUPS
    normed = []
    for g in range(SSM_GROUPS):
        yg = y[:, g * gw:(g + 1) * gw]
        normed.append(yg * lax.rsqrt(jnp.mean(yg * yg, axis=-1, keepdims=True) + EPS))
    y_ref[...] = (jnp.concatenate(normed, axis=1) * ng_ref[...]).astype(y_ref.dtype)

    @pl.when(ci == nchunks - 1)
    def _():
        ssmn_ref[...] = st_ref[...]


def _head_expander():
    k = lax.broadcasted_iota(jnp.int32, (LANES, SSM_INNER), 0)
    j = lax.broadcasted_iota(jnp.int32, (LANES, SSM_INNER), 1)
    hit = (k < BIAS_PARTS * PIECE_LANES) & (k % PIECE_LANES == DT_LANE0 + j // SSM_HEAD_DIM)
    return hit.astype(BF16)


def _ssd(xbc, z, small, conv_past, ssm_past, conv_w, conv_b, alog_pad, d_exp, norm_g, batch, Q):
    rows = xbc.shape[0]
    nchunks = rows // batch // Q
    assert Q % SUBLANES == 0 and Q >= SUBLANES and nchunks * Q * batch == rows
    row = lambda w: pl.BlockSpec((Q, w), lambda b, c: (b * nchunks + c, 0))
    state_rows = SSM_HEADS * SSM_HEAD_DIM
    per_seq = lambda a, b_: pl.BlockSpec((None, a, b_), lambda b, c: (b, 0, 0))
    return pl.pallas_call(
        functools.partial(_ssd_kernel, Q=Q, nchunks=nchunks),
        grid=(batch, nchunks),
        in_specs=[row(CONV_DIM), row(SSM_INNER), row(LANES),
                  per_seq(CONV_WIDTH - 1, CONV_DIM), per_seq(state_rows, SSM_STATE),
                  _resident((CONV_WIDTH, CONV_DIM)), _resident((1, CONV_DIM)), _resident((1, LANES)),
                  _resident((1, SSM_INNER)), _resident((1, SSM_INNER)), _resident((LANES, SSM_INNER))],
        out_specs=[row(SSM_INNER), per_seq(CONV_WIDTH - 1, CONV_DIM), per_seq(state_rows, SSM_STATE)],
        out_shape=[jax.ShapeDtypeStruct((rows, SSM_INNER), BF16),
                   jax.ShapeDtypeStruct((batch, CONV_WIDTH - 1, CONV_DIM), F32),
                   jax.ShapeDtypeStruct((batch, state_rows, SSM_STATE), F32)],
        scratch_shapes=[pltpu.VMEM((SUBLANES, CONV_DIM), F32), pltpu.VMEM((state_rows, SSM_STATE), F32)],
        compiler_params=_params("parallel", "arbitrary"),
        name="ssd_scan",
    )(xbc, z, small, conv_past, ssm_past, conv_w, conv_b, alog_pad, d_exp, norm_g, _head_expander())


def _outffn_kernel(ya_ref, ys_ref, ga_ref, gb_ref, x_ref, gt1_ref, sc2_ref, sh2_ref, gt2_ref,
                   gpm_ref, gpf_ref, gqf_ref, wo_ref, wu_ref, wd_ref, o_ref):
    merged = (_sigmoid(ga_ref[...].astype(F32)) * ya_ref[...].astype(F32)
              + _sigmoid(gb_ref[...].astype(F32)) * ys_ref[...].astype(F32))
    m = jnp.dot(merged.astype(BF16), wo_ref[...], preferred_element_type=F32)
    x1 = x_ref[...] + gt1_ref[...] * _rms(m, gpm_ref[...])
    hb = (_rms(x1, gpf_ref[...]) * (1.0 + sc2_ref[...]) + sh2_ref[...]).astype(BF16)
    f = jnp.zeros(x1.shape, F32)
    for cc in range(D_FF // D_MODEL):
        sl = slice(cc * D_MODEL, (cc + 1) * D_MODEL)
        up = jnp.dot(hb, wu_ref[:, sl], preferred_element_type=F32)
        act = jnp.square(jnp.maximum(up, 0.0)).astype(BF16)
        f = f + jnp.dot(act, wd_ref[sl, :], preferred_element_type=F32)
    o_ref[...] = x1 + gt2_ref[...] * _rms(f, gqf_ref[...])


def _out_ffn(ya, ys, ga, gb, x2d, gt1, sc2, sh2, gt2, g_post_mix, g_pre_ffn, g_post_ffn, w_out, w_up, w_down, tm):
    rows = x2d.shape[0]
    row = pl.BlockSpec((tm, D_MODEL), lambda i: (i, 0))
    vec = _resident((1, D_MODEL))
    return pl.pallas_call(
        _outffn_kernel,
        grid=(rows // tm,),
        in_specs=[row, row, row, row, row,
                  _mod_spec(gt1, tm), _mod_spec(sc2, tm), _mod_spec(sh2, tm), _mod_spec(gt2, tm),
                  vec, vec, vec,
                  _resident((D_MODEL, D_MODEL)), _resident((D_MODEL, D_FF)), _resident((D_FF, D_MODEL))],
        out_specs=row,
        out_shape=jax.ShapeDtypeStruct((rows, D_MODEL), F32),
        compiler_params=_params("parallel"),
        name="out_ffn",
    )(ya, ys, ga, gb, x2d, gt1, sc2, sh2, gt2, g_post_mix, g_pre_ffn, g_post_ffn, w_out, w_up, w_down)


def _layer(x, mod, past, wts, *, tm, tq, Q):
    b, L, _ = x.shape
    x2d = x.reshape(b * L, D_MODEL)
    sh1, sc1, gt1, sh2, sc2, gt2 = mod
    q, kf, vf, kb, vb, z, xbc, ga, gb, small, *stats = _in_proj(
        x2d, wts["g_pre_mix"], sc1, sh1, wts["w_qkv"], wts["w_ssm"], wts["w_gate"], wts["w_small"], wts["b_small"],
        tm, past is None)

    if past is None:
        assert b == 1 and tm == tq, "block statistics are per projection row block"
        y_att = _fox_prompt(q, kb, vb, _first_blocks(stats[0], L // tq), tq)
        conv_past = jnp.zeros((b, CONV_WIDTH - 1, CONV_DIM), F32)
        ssm_past = jnp.zeros((b, SSM_INNER, SSM_STATE), F32)
    else:
        k_past, v_past, logf_past, conv_past, ssm_past = past
        P = k_past.shape[1]
        lpt = jnp.swapaxes(logf_past, 1, 2)
        lnt = jnp.swapaxes(small[:, :FOX_HEADS].reshape(b, L, FOX_HEADS), 1, 2)
        y_att = _fox_cached(q, kb, vb, k_past, v_past, lpt, lnt, L)
        ssm_past = ssm_past.reshape(b, SSM_INNER, SSM_STATE)

    y_ssm, conv_new, ssm_new = _ssd(xbc, z, small, conv_past, ssm_past, wts["conv_w"], wts["conv_b"],
                                    wts["alog_pad"], wts["d_exp"], wts["ssm_norm_g"], b, Q)
    y = _out_ffn(y_att, y_ssm, ga, gb, x2d, gt1, sc2, sh2, gt2, wts["g_post_mix"], wts["g_pre_ffn"],
                 wts["g_post_ffn"], wts["w_out"], wts["w_up"], wts["w_down"], tm)
    return (y.reshape(b, L, D_MODEL),
            kf.reshape(b, L, FOX_HEADS, FOX_HEAD_DIM), vf.reshape(b, L, FOX_HEADS, FOX_HEAD_DIM),
            small[:, :FOX_HEADS].reshape(b, L, FOX_HEADS), conv_new,
            ssm_new.reshape(b, SSM_HEADS, SSM_HEAD_DIM, SSM_STATE))


def _prep_weights(w_ada, b_ada, g_pre_mix, g_post_mix, g_pre_ffn, g_post_ffn, w_in, b_f, conv_w, conv_b,
                  dt_bias, a_log, d_skip, ssm_norm_g, w_out, w_up, w_down):
    sizes = (D_MODEL, D_MODEL, D_MODEL, FOX_HEADS, SSM_INNER, CONV_DIM, SSM_HEADS, D_MODEL, D_MODEL)
    offs = [0]
    for s in sizes:
        offs.append(offs[-1] + s)
    piece = lambda i: w_in[:, offs[i]:offs[i + 1]]
    pad = LANES - FOX_HEADS - SSM_HEADS
    row = lambda v: v.reshape(1, -1).astype(F32)
    return {
        "w_qkv": w_in[:, offs[0]:offs[3]].astype(BF16),
        "w_ssm": w_in[:, offs[4]:offs[6]].astype(BF16),
        "w_gate": w_in[:, offs[7]:offs[9]].astype(BF16),
        "w_small": jnp.pad(jnp.concatenate([piece(3), piece(6)], axis=1), ((0, 0), (0, pad))).astype(BF16),
        "b_small": jnp.pad(jnp.concatenate([b_f, dt_bias]), (0, pad)).reshape(1, LANES).astype(F32),
        "alog_pad": jnp.pad(a_log, (DT_LANE0, LANES - DT_LANE0 - SSM_HEADS)).reshape(1, LANES).astype(F32),
        "d_exp": jnp.repeat(d_skip, SSM_HEAD_DIM).reshape(1, SSM_INNER).astype(F32),
        "g_pre_mix": row(g_pre_mix), "g_post_mix": row(g_post_mix),
        "g_pre_ffn": row(g_pre_ffn), "g_post_ffn": row(g_post_ffn),
        "conv_w": conv_w.astype(F32), "conv_b": row(conv_b), "ssm_norm_g": row(ssm_norm_g),
        "w_out": w_out.astype(BF16), "w_up": w_up.astype(BF16), "w_down": w_down.astype(BF16),
    }


def _forward(x_prompt, x_sample, c_prompt, c_sample, cache_fox_k, cache_fox_v, cache_fox_logf,
             state_ssm_conv, state_ssm, w_ada, b_ada, *layer_w, tm_prompt, tq, q_prompt):
    depth = w_ada.shape[0]
    bp, Lp, _ = x_prompt.shape
    bs, Ls, _ = x_sample.shape
    yp, ys = x_prompt, x_sample
    outs_p, outs_s = [], []
    for i in range(depth):
        wts = _prep_weights(w_ada[i], b_ada[i], *[w[i] for w in layer_w])
        mod = _ada(jnp.concatenate([c_prompt, c_sample], axis=0), w_ada[i], b_ada[i].reshape(1, -1))
        mod_p = [m for m in jnp.split(mod[:bp], N_MOD, axis=-1)]
        mod_s = [jnp.repeat(m, Ls, axis=0) for m in jnp.split(mod[bp:], N_MOD, axis=-1)]
        rp = _layer(yp, mod_p, None, wts, tm=tm_prompt, tq=tq, Q=q_prompt)
        rs = _layer(ys, mod_s, (cache_fox_k[i], cache_fox_v[i], cache_fox_logf[i], state_ssm_conv[i], state_ssm[i]),
                    wts, tm=bs * Ls, tq=None, Q=Ls)
        yp, ys = rp[0], rs[0]
        outs_p.append(rp[1:])
        outs_s.append(rs[1:])
    stack = lambda outs, j: jnp.stack([o[j] for o in outs])
    return (yp, ys) + tuple(stack(outs_p, j) for j in range(5)) + tuple(stack(outs_s, j) for j in range(5))


def kernel(x_prompt, x_sample, c_prompt, c_sample, cache_fox_k, cache_fox_v, cache_fox_logf, state_ssm_conv,
           state_ssm, w_ada, b_ada, g_pre_mix, g_post_mix, g_pre_ffn, g_post_ffn, w_in, b_f, conv_w, conv_b,
           dt_bias, a_log, d_skip, ssm_norm_g, w_out, w_up, w_down):
    assert x_prompt.shape[0] == 1, "the prompt path carries one sequence"
    L = x_prompt.shape[1]
    return _forward(x_prompt, x_sample, c_prompt, c_sample, cache_fox_k, cache_fox_v, cache_fox_logf,
                    state_ssm_conv, state_ssm, w_ada, b_ada, g_pre_mix, g_post_mix, g_pre_ffn, g_post_ffn,
                    w_in, b_f, conv_w, conv_b, dt_bias, a_log, d_skip, ssm_norm_g, w_out, w_up, w_down,
                    tm_prompt=min(512, L), tq=min(512, L), q_prompt=min(256, L))
```

```python
import functools

import jax
import jax.numpy as jnp
from jax import lax
from jax.experimental import pallas as pl
from jax.experimental.pallas import tpu as pltpu

F32 = jnp.float32
BF16 = jnp.bfloat16
HIGHEST = lax.Precision.HIGHEST

D_MODEL = 1024
FOX_HEADS = 8
FOX_HEAD_DIM = 128
FOX_SCALE = FOX_HEAD_DIM ** -0.5
SSM_HEADS = 16
SSM_HEAD_DIM = 64
SSM_GROUPS = 4
SSM_STATE = 128
SSM_INNER = SSM_HEADS * SSM_HEAD_DIM
CONV_WIDTH = 4
CONV_DIM = SSM_INNER + 2 * SSM_GROUPS * SSM_STATE
D_FF = 4 * D_MODEL
N_MOD = 6
EPS = 1e-6

LANES = 128
SUBLANES = 8
DT_LANE0 = FOX_HEADS
VMEM_LIMIT = 56 * 1024 * 1024
LOG2E = 1.4426950408889634
BIAS_PARTS = 3
PIECE_LANES = 32
CUM_ROWS = 128
ROW_TILE = 16
UNROLL = 8
FLASH_SUBBLOCKS = 8
SKIP_MARGIN = 152.0


_NT = (((1,), (1,)), ((), ()))
_TN = (((0,), (0,)), ((), ()))


def _params(*sem):
    return pltpu.CompilerParams(dimension_semantics=sem, vmem_limit_bytes=VMEM_LIMIT)


def _resident(shape):
    zeros = (0,) * len(shape)
    return pl.BlockSpec(shape, lambda *_: zeros, pipeline_mode=pl.Buffered(1))


def _rms(x, g):
    return x * lax.rsqrt(jnp.mean(x * x, axis=-1, keepdims=True) + EPS) * g


def _sigmoid(x):
    return 0.5 * jnp.tanh(0.5 * x) + 0.5


def _silu(x):
    h = 0.5 * x
    return h * jnp.tanh(h) + h


def _ada_kernel(c_ref, w_ref, b_ref, o_ref):
    s = _silu(c_ref[...]).astype(BF16)
    o_ref[...] = jnp.dot(s, w_ref[...].astype(BF16), preferred_element_type=F32) + b_ref[...]


def _ada(c_all, w_ada, b_ada):
    rows = c_all.shape[0]
    width = w_ada.shape[1]
    tn = 1536
    return pl.pallas_call(
        _ada_kernel,
        grid=(width // tn,),
        in_specs=[pl.BlockSpec((rows, D_MODEL), lambda j: (0, 0)),
                  pl.BlockSpec((D_MODEL, tn), lambda j: (0, j)),
                  pl.BlockSpec((1, tn), lambda j: (0, j))],
        out_specs=pl.BlockSpec((rows, tn), lambda j: (0, j)),
        out_shape=jax.ShapeDtypeStruct((rows, width), F32),
        compiler_params=_params("parallel"),
        name="ada_mod",
    )(c_all, w_ada, b_ada)


def _inproj_kernel(x_ref, g_ref, sc_ref, sh_ref, wqkv_ref, wssm_ref, wgate_ref, ws_ref, bs_ref, *rest, with_bias):
    if with_bias:
        (sel_ref, q_ref, kf_ref, vf_ref, kb_ref, vb_ref, z_ref, xbc_ref, ga_ref, gb_ref, sm_ref, stats_ref,
         carry_ref) = rest
    else:
        q_ref, kf_ref, vf_ref, kb_ref, vb_ref, z_ref, xbc_ref, ga_ref, gb_ref, sm_ref = rest
    h = _rms(x_ref[...], g_ref[...]) * (1.0 + sc_ref[...]) + sh_ref[...]
    hb = h.astype(BF16)

    def proj(w_ref, lo, width):
        return jnp.dot(hb, w_ref[:, lo:lo + width], preferred_element_type=F32)

    u = jnp.dot(hb, ws_ref[...], preferred_element_type=F32) + bs_ref[...]
    t = jnp.log1p(jnp.exp(-jnp.abs(u)))
    lane = lax.broadcasted_iota(jnp.int32, u.shape, 1)
    sm = jnp.where(lane < DT_LANE0, jnp.minimum(u, 0.0) - t,
                   jnp.where(lane < DT_LANE0 + SSM_HEADS, jnp.maximum(u, 0.0) + t, 0.0))
    sm_ref[...] = sm

    qb = (proj(wqkv_ref, 0, D_MODEL) * (FOX_SCALE * LOG2E)).astype(BF16)
    q_ref[...] = qb
    k = proj(wqkv_ref, D_MODEL, D_MODEL)
    kf_ref[...] = k.reshape(k.shape[0], FOX_HEADS, FOX_HEAD_DIM)
    if with_bias:
        @pl.when(pl.program_id(0) == 0)
        def _():
            carry_ref[...] = jnp.zeros_like(carry_ref)

        r = lax.broadcasted_iota(jnp.int32, (CUM_ROWS, CUM_ROWS), 0)
        c = lax.broadcasted_iota(jnp.int32, (CUM_ROWS, CUM_ROWS), 1)
        tri = (r >= c).astype(F32)
        run = carry_ref[0:1, :]
        parts = []
        for b in range(sm.shape[0] // CUM_ROWS):
            cs = jnp.dot(tri, sm[b * CUM_ROWS:(b + 1) * CUM_ROWS, :], precision=HIGHEST,
                         preferred_element_type=F32) + run
            run = cs[CUM_ROWS - 1:CUM_ROWS, :]
            parts.append(cs)
        carry_ref[0:1, :] = run
        nb = jnp.concatenate(parts, axis=0) * (-LOG2E)
        hi = nb.astype(BF16)
        r1 = nb - hi.astype(F32)
        mid = r1.astype(BF16)
        lo = (r1 - mid.astype(F32)).astype(BF16)
        aug = jnp.dot(jnp.concatenate([hi, mid, lo], axis=1), sel_ref[...], preferred_element_type=F32)
        kb = k.astype(BF16)
        for hd in range(FOX_HEADS):
            sl = slice(hd * FOX_HEAD_DIM, (hd + 1) * FOX_HEAD_DIM)
            kb_ref[:, 2 * hd * FOX_HEAD_DIM:(2 * hd + 1) * FOX_HEAD_DIM] = kb[:, sl]
            kb_ref[:, (2 * hd + 1) * FOX_HEAD_DIM:(2 * hd + 2) * FOX_HEAD_DIM] = aug[:, sl].astype(BF16)

        def max_sq_norms(xb):
            x2 = xb.astype(F32)
            x2 = x2 * x2
            out = jnp.zeros((1, LANES), F32)
            for hd in range(FOX_HEADS):
                n2 = jnp.sum(x2[:, hd * FOX_HEAD_DIM:(hd + 1) * FOX_HEAD_DIM], axis=-1, keepdims=True)
                out = jnp.where(lane[0:1, :] == hd, jnp.max(n2, axis=0, keepdims=True), out)
            return out

        stats_ref[...] = jnp.zeros(stats_ref.shape, F32)
        stats_ref[0:1, :] = max_sq_norms(qb)
        stats_ref[1:2, :] = max_sq_norms(kb)
        stats_ref[2:3, :] = nb[0:1, :]
        stats_ref[3:4, :] = nb[nb.shape[0] - 1:, :]
    else:
        kb_ref[...] = k.astype(BF16)
    v = proj(wqkv_ref, 2 * D_MODEL, D_MODEL)
    vf_ref[...] = v.reshape(v.shape[0], FOX_HEADS, FOX_HEAD_DIM)
    vb_ref[...] = v.astype(BF16)
    z_ref[...] = proj(wssm_ref, 0, SSM_INNER).astype(BF16)
    xbc_ref[...] = proj(wssm_ref, SSM_INNER, CONV_DIM)
    ga_ref[...] = proj(wgate_ref, 0, D_MODEL).astype(BF16)
    gb_ref[...] = proj(wgate_ref, D_MODEL, D_MODEL).astype(BF16)


def _mod_spec(arr, tm):
    if arr.shape[0] == 1:
        return pl.BlockSpec((1, D_MODEL), lambda i: (0, 0))
    return pl.BlockSpec((tm, D_MODEL), lambda i: (i, 0))


def _bias_selector():
    r = lax.broadcasted_iota(jnp.int32, (BIAS_PARTS * LANES, FOX_HEADS * FOX_HEAD_DIM), 0)
    c = lax.broadcasted_iota(jnp.int32, (BIAS_PARTS * LANES, FOX_HEADS * FOX_HEAD_DIM), 1)
    hit = (r % LANES < FOX_HEADS) & (c == (r % LANES) * FOX_HEAD_DIM + r // LANES)
    return hit.astype(BF16)


def _in_proj(x2d, g, sc, sh, w_qkv, w_ssm, w_gate, w_small, b_small, tm, with_bias):
    rows = x2d.shape[0]
    assert not with_bias or tm % CUM_ROWS == 0
    row = lambda w: pl.BlockSpec((tm, w), lambda i: (i, 0))
    shp = lambda w, dt: jax.ShapeDtypeStruct((rows, w), dt)
    kb_width = 2 * D_MODEL if with_bias else D_MODEL
    in_specs = [row(D_MODEL), _resident((1, D_MODEL)), _mod_spec(sc, tm), _mod_spec(sh, tm),
                _resident(w_qkv.shape), _resident(w_ssm.shape), _resident(w_gate.shape),
                _resident((D_MODEL, LANES)), _resident((1, LANES))]
    args = [x2d, g, sc, sh, w_qkv, w_ssm, w_gate, w_small, b_small]
    heads = pl.BlockSpec((tm, FOX_HEADS, FOX_HEAD_DIM), lambda i: (i, 0, 0))
    heads_shape = jax.ShapeDtypeStruct((rows, FOX_HEADS, FOX_HEAD_DIM), F32)
    out_specs = [row(D_MODEL), heads, heads, row(kb_width), row(D_MODEL),
                 row(SSM_INNER), row(CONV_DIM), row(D_MODEL), row(D_MODEL), row(LANES)]
    out_shape = [shp(D_MODEL, BF16), heads_shape, heads_shape, shp(kb_width, BF16),
                 shp(D_MODEL, BF16), shp(SSM_INNER, BF16), shp(CONV_DIM, F32), shp(D_MODEL, BF16),
                 shp(D_MODEL, BF16), shp(LANES, F32)]
    if with_bias:
        in_specs.append(_resident((BIAS_PARTS * LANES, FOX_HEADS * FOX_HEAD_DIM)))
        args.append(_bias_selector())
        out_specs.append(pl.BlockSpec((SUBLANES, LANES), lambda i: (i, 0)))
        out_shape.append(jax.ShapeDtypeStruct((rows // tm * SUBLANES, LANES), F32))
    return pl.pallas_call(
        functools.partial(_inproj_kernel, with_bias=with_bias),
        grid=(rows // tm,),
        in_specs=in_specs,
        out_specs=out_specs,
        out_shape=out_shape,
        scratch_shapes=[pltpu.VMEM((SUBLANES, LANES), F32)] if with_bias else [],
        compiler_params=_params("arbitrary" if with_bias else "parallel"),
        name="in_proj",
    )(*args)


def _flash_kernel(first_ref, q_ref, k_ref, v_ref, o_ref, s0_ref, s1_ref, p0_ref, p1_ref, m_ref, l_ref, alpha0_ref,
                  alpha1_ref, acc_ref, *, tq, tk, nsub):
    hd = pl.program_id(0)
    first_sub = pl.program_id(1) * nsub
    nlt = tk // LANES
    lane = lax.broadcasted_iota(jnp.int32, (tq, FOX_HEAD_DIM), 1)
    ones = jnp.where(lane < BIAS_PARTS, 1.0, 0.0).astype(BF16)

    def q_rows(sub):
        return pl.ds(pl.multiple_of(sub * tq, tq), tq)

    def q_aug(sub):
        return jnp.concatenate([q_ref[q_rows(sub), :], ones], axis=1)

    def scores(q2, j, s_ref):
        off = pl.multiple_of(j * tk, tk)
        s_ref[...] = lax.dot_general(q2, k_ref[pl.ds(off, tk), :], _NT, preferred_element_type=F32)

    def softmax(s_ref, p_ref, alpha_ref, mask_shift):
        for rt in range(tq // ROW_TILE):
            rows = slice(rt * ROW_TILE, (rt + 1) * ROW_TILE)
            cols = [s_ref[rows, c * LANES:(c + 1) * LANES] for c in range(nlt)]
            if mask_shift is not None:
                ri = lax.broadcasted_iota(jnp.int32, (ROW_TILE, LANES), 0) + rt * ROW_TILE
                ci = lax.broadcasted_iota(jnp.int32, (ROW_TILE, LANES), 1) + mask_shift
                cols = [jnp.where(ci + c * LANES <= ri, cols[c], -jnp.inf) for c in range(nlt)]
            mx = functools.reduce(jnp.maximum, cols)
            m_old = m_ref[rows, :]
            m_new = jnp.maximum(m_old, jnp.max(mx, axis=-1, keepdims=True))
            alpha = jnp.exp2(m_old - m_new)
            ps = [jnp.exp2(cc - m_new) for cc in cols]
            l_ref[rows, :] = alpha * l_ref[rows, :] + jnp.sum(functools.reduce(jnp.add, ps), axis=-1, keepdims=True)
            m_ref[rows, :] = m_new
            alpha_ref[rows, :] = alpha
            for c in range(nlt):
                p_ref[rows, c * LANES:(c + 1) * LANES] = ps[c].astype(BF16)

    def weighted_values(j, p_ref, alpha_ref):
        off = pl.multiple_of(j * tk, tk)
        acc_ref[...] = alpha_ref[...] * acc_ref[...] + jnp.dot(p_ref[...], v_ref[pl.ds(off, tk), :],
                                                               preferred_element_type=F32)

    bufs = ((s0_ref, p0_ref, alpha0_ref), (s1_ref, p1_ref, alpha1_ref))

    def run(q2, first, n, diagonal_last, next_scores):
        for k in range(n):
            s_ref, p_ref, alpha_ref = bufs[k % 2]
            s_nxt, p_prv, alpha_prv = bufs[(k + 1) % 2]
            if k > 0:
                weighted_values(first + k - 1, p_prv, alpha_prv)
            if k + 1 < n or next_scores:
                scores(q2, first + k + 1, s_nxt)
            softmax(s_ref, p_ref, alpha_ref, 0 if (diagonal_last and k == n - 1) else None)
        weighted_values(first + n - 1, *bufs[(n - 1) % 2][1:])

    scores(q_aug(0), first_ref[hd, first_sub], s0_ref)

    @pl.loop(0, nsub)
    def _(sub):
        i = first_sub + sub
        q2 = q_aug(sub)
        m_ref[...] = jnp.full(m_ref.shape, -jnp.inf, F32)
        l_ref[...] = jnp.zeros(l_ref.shape, F32)
        acc_ref[...] = jnp.zeros(acc_ref.shape, F32)
        j0 = first_ref[hd, i]
        n_full = i - j0

        @pl.loop(0, n_full // UNROLL)
        def _(t):
            run(q2, j0 + UNROLL * t, UNROLL, False, True)

        nxt = jnp.minimum(sub + 1, nsub - 1)
        q2_next = q_aug(nxt)
        j0_next = first_ref[hd, first_sub + nxt]
        for rem in range(UNROLL):
            @pl.when(n_full % UNROLL == rem)
            def _():
                run(q2, i - rem, rem + 1, True, False)
                scores(q2_next, j0_next, s0_ref)
                o_ref[q_rows(sub), :] = (acc_ref[...] / l_ref[...]).astype(o_ref.dtype)


def _first_blocks(stats, nq):
    st = stats.reshape(nq, SUBLANES, LANES)
    qn = jnp.sqrt(st[:, 0, :FOX_HEADS])
    kn = jnp.sqrt(st[:, 1, :FOX_HEADS])
    b_first, b_last = st[:, 2, :FOX_HEADS], st[:, 3, :FOX_HEADS]
    bound = qn[:, None, :] * (kn[None, :, :] + kn[:, None, :]) + b_last[None, :, :] - b_first[:, None, :]
    blocks = jnp.arange(nq, dtype=jnp.int32)
    first_needed = jnp.min(jnp.where(bound <= -SKIP_MARGIN, nq, blocks[None, :, None]), axis=1)
    return jnp.minimum(first_needed, blocks[:, None]).T


def _fox_prompt(q, kp, v, first, tq):
    L = q.shape[0]
    tk = tq
    nsub = min(FLASH_SUBBLOCKS, L // tq)
    return pl.pallas_call(
        functools.partial(_flash_kernel, tq=tq, tk=tk, nsub=nsub),
        grid=(FOX_HEADS, L // (tq * nsub)),
        in_specs=[pl.BlockSpec(memory_space=pltpu.SMEM),
                  pl.BlockSpec((nsub * tq, FOX_HEAD_DIM), lambda h, i: (i, h)),
                  pl.BlockSpec((L, 2 * FOX_HEAD_DIM), lambda h, i: (0, h)),
                  pl.BlockSpec((L, FOX_HEAD_DIM), lambda h, i: (0, h))],
        out_specs=pl.BlockSpec((nsub * tq, FOX_HEAD_DIM), lambda h, i: (i, h)),
        out_shape=jax.ShapeDtypeStruct((L, FOX_HEADS * FOX_HEAD_DIM), BF16),
        scratch_shapes=[pltpu.VMEM((tq, tk), F32)] * 2 + [pltpu.VMEM((tq, tk), BF16)] * 2
        + [pltpu.VMEM((tq, FOX_HEAD_DIM), F32)] * 5,
        compiler_params=_params("parallel", "arbitrary"),
        name="fox_flash",
    )(first, q, kp, v)


def _cached_attn_kernel(q_ref, kn_ref, vn_ref, kc_ref, vc_ref, lpt_ref, lnt_ref, o_ref):
    n = q_ref.shape[0]
    P = kc_ref.shape[0]
    r = lax.broadcasted_iota(jnp.int32, (P, P), 0)
    c = lax.broadcasted_iota(jnp.int32, (P, P), 1)
    G = jnp.dot(lpt_ref[...], (r > c).astype(F32), precision=HIGHEST, preferred_element_type=F32) * LOG2E
    rn = lax.broadcasted_iota(jnp.int32, (n, n), 0)
    cn = lax.broadcasted_iota(jnp.int32, (n, n), 1)
    Hn = jnp.dot(lnt_ref[...], (rn <= cn).astype(F32), precision=HIGHEST, preferred_element_type=F32) * LOG2E
    causal = cn <= rn
    kt = jnp.swapaxes(kc_ref[...], 0, 1)
    vt = jnp.swapaxes(vc_ref[...], 0, 1)
    head = lambda h: slice(h * FOX_HEAD_DIM, (h + 1) * FOX_HEAD_DIM)
    scores = []
    for h in range(FOX_HEADS):
        qh = q_ref[:, head(h)]
        sp = lax.dot_general(qh, kt[h].astype(BF16), _NT, preferred_element_type=F32) + G[h:h + 1, :]
        sn = lax.dot_general(qh, kn_ref[:, head(h)], _NT, preferred_element_type=F32) - Hn[h:h + 1, :]
        scores.append((sp, jnp.where(causal, sn, -jnp.inf)))
    probs = []
    for sp, sn in scores:
        m = jnp.maximum(jnp.max(sp, axis=-1, keepdims=True), jnp.max(sn, axis=-1, keepdims=True))
        pp = jnp.exp2(sp - m)
        pn = jnp.exp2(sn - m)
        l = jnp.sum(pp, axis=-1, keepdims=True) + jnp.sum(pn, axis=-1, keepdims=True)
        probs.append((pp.astype(BF16), pn.astype(BF16), l))
    for h, (pp, pn, l) in enumerate(probs):
        o = (jnp.dot(pp, vt[h].astype(BF16), preferred_element_type=F32)
             + jnp.dot(pn, vn_ref[:, head(h)], preferred_element_type=F32))
        o_ref[:, head(h)] = (o / l).astype(o_ref.dtype)


def _fox_cached(q, kn, vn, kc, vc, lpt, lnt, n):
    B, P = kc.shape[:2]
    W = FOX_HEADS * FOX_HEAD_DIM
    row = pl.BlockSpec((n, W), lambda b: (b, 0))
    cache = pl.BlockSpec((None, P, FOX_HEADS, FOX_HEAD_DIM), lambda b: (b, 0, 0, 0))
    return pl.pallas_call(
        _cached_attn_kernel,
        grid=(B,),
        in_specs=[row, row, row, cache, cache,
                  pl.BlockSpec((None, FOX_HEADS, P), lambda b: (b, 0, 0)),
                  pl.BlockSpec((None, FOX_HEADS, n), lambda b: (b, 0, 0))],
        out_specs=row,
        out_shape=jax.ShapeDtypeStruct((B * n, W), BF16),
        compiler_params=_params("parallel"),
        name="fox_cached",
    )(q, kn, vn, kc, vc, lpt, lnt)


def _ssd_kernel(xbc_ref, z_ref, sm_ref, convp_ref, ssmp_ref, cw_ref, cb_ref, alog_ref, dexp_ref, ng_ref,
                expand_ref, y_ref, convn_ref, ssmn_ref, tail_ref, st_ref, *, Q, nchunks):
    ci = pl.program_id(1)
    PAIR = 2 * SSM_HEAD_DIM
    GW = SSM_STATE

    @pl.when(ci == 0)
    def _():
        st_ref[...] = ssmp_ref[...]
        tail_ref[...] = jnp.zeros((SUBLANES, CONV_DIM), F32)
        tail_ref[SUBLANES - (CONV_WIDTH - 1):SUBLANES, :] = convp_ref[...]

    x3 = xbc_ref[...].reshape(Q // SUBLANES, SUBLANES, CONV_DIM)
    tail = tail_ref[...].reshape(1, SUBLANES, CONV_DIM)
    sub = lax.broadcasted_iota(jnp.int32, x3.shape, 1)
    conv = cb_ref[...]
    for d in range(CONV_WIDTH - 1, 0, -1):
        rot = pltpu.roll(x3, d, 1)
        prev = jnp.concatenate([pltpu.roll(tail, d, 1), rot[:-1]], axis=0)
        conv = conv + jnp.where(sub < d, prev, rot) * cw_ref[CONV_WIDTH - 1 - d:CONV_WIDTH - d, :]
    conv = (conv + x3 * cw_ref[CONV_WIDTH - 1:CONV_WIDTH, :]).reshape(Q, CONV_DIM)
    tail_ref[...] = xbc_ref[Q - SUBLANES:Q, :]
    convn_ref[...] = xbc_ref[Q - (CONV_WIDTH - 1):Q, :]

    xc = _silu(conv)
    xs = xc[:, :SSM_INNER]
    Bm = xc[:, SSM_INNER:SSM_INNER + SSM_GROUPS * GW]
    Cm = xc[:, SSM_INNER + SSM_GROUPS * GW:]

    lane1 = lax.broadcasted_iota(jnp.int32, (Q, LANES), 1)
    head_lanes = (lane1 >= DT_LANE0) & (lane1 < DT_LANE0 + SSM_HEADS)
    dt = jnp.where(head_lanes, sm_ref[...], 0.0)
    a = dt * (-jnp.exp(alog_ref[...]))
    r = lax.broadcasted_iota(jnp.int32, (Q, Q), 0)
    c = lax.broadcasted_iota(jnp.int32, (Q, Q), 1)
    causal = r >= c
    a_cs = jnp.dot(causal.astype(F32), a, precision=HIGHEST, preferred_element_type=F32)
    er = lax.broadcasted_iota(jnp.int32, (SSM_HEADS, LANES), 0)
    ec = lax.broadcasted_iota(jnp.int32, (SSM_HEADS, LANES), 1)
    pick = (ec == er + DT_LANE0).astype(F32)
    acs_t = lax.dot_general(pick, a_cs, _NT, precision=HIGHEST, preferred_element_type=F32)
    a_last = a_cs[Q - 1:Q, :]
    e_last = jnp.exp(a_last)

    def per_channel(x):
        x = jnp.where(head_lanes, x, 0.0)
        hi = x.astype(BF16).astype(F32)
        r1 = x - hi
        mid = r1.astype(BF16).astype(F32)
        packed = hi + pltpu.roll(mid, PIECE_LANES, 1) + pltpu.roll(r1 - mid, 2 * PIECE_LANES, 1)
        return jnp.dot(packed.astype(BF16), expand_ref[...], preferred_element_type=F32)

    xd = xs * per_channel(dt)
    xdb = xd.astype(BF16)
    xw = xd * per_channel(jnp.exp(a_last - a_cs))
    e_cs = per_channel(jnp.exp(a_cs))
    Bb = Bm.astype(BF16)
    Cb = Cm.astype(BF16)

    row_bands = [(0, Q // 2, Q // 2), (Q // 2, Q, Q)] if Q % (2 * LANES) == 0 else [(0, Q, Q)]
    lane_lo = lax.broadcasted_iota(jnp.int32, (Q, PAIR), 1) < SSM_HEAD_DIM
    row_lo = lax.broadcasted_iota(jnp.int32, (PAIR, GW), 0) < SSM_HEAD_DIM

    y_parts = []
    for g in range(SSM_GROUPS):
        Bg = Bb[:, g * GW:(g + 1) * GW]
        Cg = Cb[:, g * GW:(g + 1) * GW]
        cbm = lax.dot_general(Cg, Bg, _NT, preferred_element_type=F32)
        for pr in range(SSM_HEADS // SSM_GROUPS // 2):
            pair = g * (SSM_HEADS // SSM_GROUPS // 2) + pr
            sl = slice(pair * PAIR, (pair + 1) * PAIR)
            s_prev = st_ref[sl, :]
            ys = []
            for hh in (2 * pair, 2 * pair + 1):
                ln = DT_LANE0 + hh
                parts = []
                for r0, r1, nc in row_bands:
                    lm = jnp.exp(jnp.where(causal[r0:r1, :nc], a_cs[r0:r1, ln:ln + 1] - acs_t[hh:hh + 1, :nc],
                                           -jnp.inf))
                    parts.append(jnp.dot((cbm[r0:r1, :nc] * lm).astype(BF16), xdb[:nc, sl],
                                         preferred_element_type=F32))
                ys.append(parts[0] if len(parts) == 1 else jnp.concatenate(parts, axis=0))
            y_off = lax.dot_general(Cg, s_prev.astype(BF16), _NT, preferred_element_type=F32) * e_cs[:, sl]
            ln0 = DT_LANE0 + 2 * pair
            dec = jnp.where(row_lo, e_last[:, ln0:ln0 + 1], e_last[:, ln0 + 1:ln0 + 2])
            st_ref[sl, :] = s_prev * dec + lax.dot_general(xw[:, sl], Bm[:, g * GW:(g + 1) * GW], _TN,
                                                           preferred_element_type=F32)
            y_parts.append(jnp.where(lane_lo, ys[0], ys[1]) + y_off + xs[:, sl] * dexp_ref[:, sl])

    y = jnp.concatenate(y_parts, axis=1) * _silu(z_ref[...].astype(F32))
    gw = SSM_INNER // SSM_GROUPS
    normed = []
    for g in range(SSM_GROUPS):
        yg = y[:, g * gw:(g + 1) * gw]
        normed.append(yg * lax.rsqrt(jnp.mean(yg * yg, axis=-1, keepdims=True) + EPS))
    y_ref[...] = (jnp.concatenate(normed, axis=1) * ng_ref[...]).astype(y_ref.dtype)

    @pl.when(ci == nchunks - 1)
    def _():
        ssmn_ref[...] = st_ref[...]


def _head_expander():
    k = lax.broadcasted_iota(jnp.int32, (LANES, SSM_INNER), 0)
    j = lax.broadcasted_iota(jnp.int32, (LANES, SSM_INNER), 1)
    hit = (k < BIAS_PARTS * PIECE_LANES) & (k % PIECE_LANES == DT_LANE0 + j // SSM_HEAD_DIM)
    return hit.astype(BF16)


def _ssd(xbc, z, small, conv_past, ssm_past, conv_w, conv_b, alog_pad, d_exp, norm_g, batch, Q):
    rows = xbc.shape[0]
    nchunks = rows // batch // Q
    assert Q % SUBLANES == 0 and Q >= SUBLANES and nchunks * Q * batch == rows
    row = lambda w: pl.BlockSpec((Q, w), lambda b, c: (b * nchunks + c, 0))
    state_rows = SSM_HEADS * SSM_HEAD_DIM
    per_seq = lambda a, b_: pl.BlockSpec((None, a, b_), lambda b, c: (b, 0, 0))
    return pl.pallas_call(
        functools.partial(_ssd_kernel, Q=Q, nchunks=nchunks),
        grid=(batch, nchunks),
        in_specs=[row(CONV_DIM), row(SSM_INNER), row(LANES),
                  per_seq(CONV_WIDTH - 1, CONV_DIM), per_seq(state_rows, SSM_STATE),
                  _resident((CONV_WIDTH, CONV_DIM)), _resident((1, CONV_DIM)), _resident((1, LANES)),
                  _resident((1, SSM_INNER)), _resident((1, SSM_INNER)), _resident((LANES, SSM_INNER))],
        out_specs=[row(SSM_INNER), per_seq(CONV_WIDTH - 1, CONV_DIM), per_seq(state_rows, SSM_STATE)],
        out_shape=[jax.ShapeDtypeStruct((rows, SSM_INNER), BF16),
                   jax.ShapeDtypeStruct((batch, CONV_WIDTH - 1, CONV_DIM), F32),
                   jax.ShapeDtypeStruct((batch, state_rows, SSM_STATE), F32)],
        scratch_shapes=[pltpu.VMEM((SUBLANES, CONV_DIM), F32), pltpu.VMEM((state_rows, SSM_STATE), F32)],
        compiler_params=_params("parallel", "arbitrary"),
        name="ssd_scan",
    )(xbc, z, small, conv_past, ssm_past, conv_w, conv_b, alog_pad, d_exp, norm_g, _head_expander())


def _outffn_kernel(ya_ref, ys_ref, ga_ref, gb_ref, x_ref, gt1_ref, sc2_ref, sh2_ref, gt2_ref,
                   gpm_ref, gpf_ref, gqf_ref, wo_ref, wu_ref, wd_ref, o_ref):
    merged = (_sigmoid(ga_ref[...].astype(F32)) * ya_ref[...].astype(F32)
              + _sigmoid(gb_ref[...].astype(F32)) * ys_ref[...].astype(F32))
    m = jnp.dot(merged.astype(BF16), wo_ref[...], preferred_element_type=F32)
    x1 = x_ref[...] + gt1_ref[...] * _rms(m, gpm_ref[...])
    hb = (_rms(x1, gpf_ref[...]) * (1.0 + sc2_ref[...]) + sh2_ref[...]).astype(BF16)
    f = jnp.zeros(x1.shape, F32)
    for cc in range(D_FF // D_MODEL):
        sl = slice(cc * D_MODEL, (cc + 1) * D_MODEL)
        up = jnp.dot(hb, wu_ref[:, sl], preferred_element_type=F32)
        act = jnp.square(jnp.maximum(up, 0.0)).astype(BF16)
        f = f + jnp.dot(act, wd_ref[sl, :], preferred_element_type=F32)
    o_ref[...] = x1 + gt2_ref[...] * _rms(f, gqf_ref[...])


def _out_ffn(ya, ys, ga, gb, x2d, gt1, sc2, sh2, gt2, g_post_mix, g_pre_ffn, g_post_ffn, w_out, w_up, w_down, tm):
    rows = x2d.shape[0]
    row = pl.BlockSpec((tm, D_MODEL), lambda i: (i, 0))
    vec = _resident((1, D_MODEL))
    return pl.pallas_call(
        _outffn_kernel,
        grid=(rows // tm,),
        in_specs=[row, row, row, row, row,
                  _mod_spec(gt1, tm), _mod_spec(sc2, tm), _mod_spec(sh2, tm), _mod_spec(gt2, tm),
                  vec, vec, vec,
                  _resident((D_MODEL, D_MODEL)), _resident((D_MODEL, D_FF)), _resident((D_FF, D_MODEL))],
        out_specs=row,
        out_shape=jax.ShapeDtypeStruct((rows, D_MODEL), F32),
        compiler_params=_params("parallel"),
        name="out_ffn",
    )(ya, ys, ga, gb, x2d, gt1, sc2, sh2, gt2, g_post_mix, g_pre_ffn, g_post_ffn, w_out, w_up, w_down)


def _layer(x, mod, past, wts, *, tm, tq, Q):
    b, L, _ = x.shape
    x2d = x.reshape(b * L, D_MODEL)
    sh1, sc1, gt1, sh2, sc2, gt2 = mod
    q, kf, vf, kb, vb, z, xbc, ga, gb, small, *stats = _in_proj(
        x2d, wts["g_pre_mix"], sc1, sh1, wts["w_qkv"], wts["w_ssm"], wts["w_gate"], wts["w_small"], wts["b_small"],
        tm, past is None)

    if past is None:
        assert b == 1 and tm == tq, "block statistics are per projection row block"
        y_att = _fox_prompt(q, kb, vb, _first_blocks(stats[0], L // tq), tq)
        conv_past = jnp.zeros((b, CONV_WIDTH - 1, CONV_DIM), F32)
        ssm_past = jnp.zeros((b, SSM_INNER, SSM_STATE), F32)
    else:
        k_past, v_past, logf_past, conv_past, ssm_past = past
        P = k_past.shape[1]
        lpt = jnp.swapaxes(logf_past, 1, 2)
        lnt = jnp.swapaxes(small[:, :FOX_HEADS].reshape(b, L, FOX_HEADS), 1, 2)
        y_att = _fox_cached(q, kb, vb, k_past, v_past, lpt, lnt, L)
        ssm_past = ssm_past.reshape(b, SSM_INNER, SSM_STATE)

    y_ssm, conv_new, ssm_new = _ssd(xbc, z, small, conv_past, ssm_past, wts["conv_w"], wts["conv_b"],
                                    wts["alog_pad"], wts["d_exp"], wts["ssm_norm_g"], b, Q)
    y = _out_ffn(y_att, y_ssm, ga, gb, x2d, gt1, sc2, sh2, gt2, wts["g_post_mix"], wts["g_pre_ffn"],
                 wts["g_post_ffn"], wts["w_out"], wts["w_up"], wts["w_down"], tm)
    return (y.reshape(b, L, D_MODEL),
            kf.reshape(b, L, FOX_HEADS, FOX_HEAD_DIM), vf.reshape(b, L, FOX_HEADS, FOX_HEAD_DIM),
            small[:, :FOX_HEADS].reshape(b, L, FOX_HEADS), conv_new,
            ssm_new.reshape(b, SSM_HEADS, SSM_HEAD_DIM, SSM_STATE))


def _prep_weights(w_ada, b_ada, g_pre_mix, g_post_mix, g_pre_ffn, g_post_ffn, w_in, b_f, conv_w, conv_b,
                  dt_bias, a_log, d_skip, ssm_norm_g, w_out, w_up, w_down):
    sizes = (D_MODEL, D_MODEL, D_MODEL, FOX_HEADS, SSM_INNER, CONV_DIM, SSM_HEADS, D_MODEL, D_MODEL)
    offs = [0]
    for s in sizes:
        offs.append(offs[-1] + s)
    piece = lambda i: w_in[:, offs[i]:offs[i + 1]]
    pad = LANES - FOX_HEADS - SSM_HEADS
    row = lambda v: v.reshape(1, -1).astype(F32)
    return {
        "w_qkv": w_in[:, offs[0]:offs[3]].astype(BF16),
        "w_ssm": w_in[:, offs[4]:offs[6]].astype(BF16),
        "w_gate": w_in[:, offs[7]:offs[9]].astype(BF16),
        "w_small": jnp.pad(jnp.concatenate([piece(3), piece(6)], axis=1), ((0, 0), (0, pad))).astype(BF16),
        "b_small": jnp.pad(jnp.concatenate([b_f, dt_bias]), (0, pad)).reshape(1, LANES).astype(F32),
        "alog_pad": jnp.pad(a_log, (DT_LANE0, LANES - DT_LANE0 - SSM_HEADS)).reshape(1, LANES).astype(F32),
        "d_exp": jnp.repeat(d_skip, SSM_HEAD_DIM).reshape(1, SSM_INNER).astype(F32),
        "g_pre_mix": row(g_pre_mix), "g_post_mix": row(g_post_mix),
        "g_pre_ffn": row(g_pre_ffn), "g_post_ffn": row(g_post_ffn),
        "conv_w": conv_w.astype(F32), "conv_b": row(conv_b), "ssm_norm_g": row(ssm_norm_g),
        "w_out": w_out.astype(BF16), "w_up": w_up.astype(BF16), "w_down": w_down.astype(BF16),
    }


def _forward(x_prompt, x_sample, c_prompt, c_sample, cache_fox_k, cache_fox_v, cache_fox_logf,
             state_ssm_conv, state_ssm, w_ada, b_ada, *layer_w, tm_prompt, tq, q_prompt):
    depth = w_ada.shape[0]
    bp, Lp, _ = x_prompt.shape
    bs, Ls, _ = x_sample.shape
    yp, ys = x_prompt, x_sample
    outs_p, outs_s = [], []
    for i in range(depth):
        wts = _prep_weights(w_ada[i], b_ada[i], *[w[i] for w in layer_w])
        mod = _ada(jnp.concatenate([c_prompt, c_sample], axis=0), w_ada[i], b_ada[i].reshape(1, -1))
        mod_p = [m for m in jnp.split(mod[:bp], N_MOD, axis=-1)]
        mod_s = [jnp.repeat(m, Ls, axis=0) for m in jnp.split(mod[bp:], N_MOD, axis=-1)]
        rp = _layer(yp, mod_p, None, wts, tm=tm_prompt, tq=tq, Q=q_prompt)
        rs = _layer(ys, mod_s, (cache_fox_k[i], cache_fox_v[i], cache_fox_logf[i], state_ssm_conv[i], state_ssm[i]),
                    wts, tm=bs * Ls, tq=None, Q=Ls)
        yp, ys = rp[0], rs[0]
        outs_p.append(rp[1:])
        outs_s.append(rs[1:])
    stack = lambda outs, j: jnp.stack([o[j] for o in outs])
    return (yp, ys) + tuple(stack(outs_p, j) for j in range(5)) + tuple(stack(outs_s, j) for j in range(5))


def kernel(x_prompt, x_sample, c_prompt, c_sample, cache_fox_k, cache_fox_v, cache_fox_logf, state_ssm_conv,
           state_ssm, w_ada, b_ada, g_pre_mix, g_post_mix, g_pre_ffn, g_post_ffn, w_in, b_f, conv_w, conv_b,
           dt_bias, a_log, d_skip, ssm_norm_g, w_out, w_up, w_down):
    assert x_prompt.shape[0] == 1, "the prompt path carries one sequence"
    L = x_prompt.shape[1]
    return _forward(x_prompt, x_sample, c_prompt, c_sample, cache_fox_k, cache_fox_v, cache_fox_logf,
                    state_ssm_conv, state_ssm, w_ada, b_ada, g_pre_mix, g_post_mix, g_pre_ffn, g_post_ffn,
                    w_in, b_f, conv_w, conv_b, dt_bias, a_log, d_skip, ssm_norm_g, w_out, w_up, w_down,
                    tm_prompt=min(512, L), tq=min(512, L), q_prompt=min(256, L))
```

```python
import functools

import jax
import jax.numpy as jnp
from jax import lax
from jax.experimental import pallas as pl
from jax.experimental.pallas import tpu as pltpu

F32 = jnp.float32
BF16 = jnp.bfloat16
HIGHEST = lax.Precision.HIGHEST

D_MODEL = 1024
FOX_HEADS = 8
FOX_HEAD_DIM = 128
FOX_SCALE = FOX_HEAD_DIM ** -0.5
SSM_HEADS = 16
SSM_HEAD_DIM = 64
SSM_GROUPS = 4
SSM_STATE = 128
SSM_INNER = SSM_HEADS * SSM_HEAD_DIM
CONV_WIDTH = 4
CONV_DIM = SSM_INNER + 2 * SSM_GROUPS * SSM_STATE
D_FF = 4 * D_MODEL
N_MOD = 6
EPS = 1e-6

LANES = 128
SUBLANES = 8
DT_LANE0 = FOX_HEADS
VMEM_LIMIT = 56 * 1024 * 1024
LOG2E = 1.4426950408889634
BIAS_PARTS = 3
PIECE_LANES = 32
CUM_ROWS = 128
ROW_TILE = 16
UNROLL = 8
FLASH_SUBBLOCKS = 8
SKIP_MARGIN = 152.0


_NT = (((1,), (1,)), ((), ()))
_TN = (((0,), (0,)), ((), ()))


def _params(*sem):
    return pltpu.CompilerParams(dimension_semantics=sem, vmem_limit_bytes=VMEM_LIMIT)


def _resident(shape):
    zeros = (0,) * len(shape)
    return pl.BlockSpec(shape, lambda *_: zeros, pipeline_mode=pl.Buffered(1))


def _rms(x, g):
    return x * lax.rsqrt(jnp.mean(x * x, axis=-1, keepdims=True) + EPS) * g


def _sigmoid(x):
    return 0.5 * jnp.tanh(0.5 * x) + 0.5


def _silu(x):
    h = 0.5 * x
    return h * jnp.tanh(h) + h


def _ada_kernel(c_ref, w_ref, b_ref, o_ref):
    s = _silu(c_ref[...]).astype(BF16)
    o_ref[...] = jnp.dot(s, w_ref[...].astype(BF16), preferred_element_type=F32) + b_ref[...]


def _ada(c_all, w_ada, b_ada):
    rows = c_all.shape[0]
    width = w_ada.shape[1]
    tn = 1536
    return pl.pallas_call(
        _ada_kernel,
        grid=(width // tn,),
        in_specs=[pl.BlockSpec((rows, D_MODEL), lambda j: (0, 0)),
                  pl.BlockSpec((D_MODEL, tn), lambda j: (0, j)),
                  pl.BlockSpec((1, tn), lambda j: (0, j))],
        out_specs=pl.BlockSpec((rows, tn), lambda j: (0, j)),
        out_shape=jax.ShapeDtypeStruct((rows, width), F32),
        compiler_params=_params("parallel"),
        name="ada_mod",
    )(c_all, w_ada, b_ada)


def _inproj_kernel(x_ref, g_ref, sc_ref, sh_ref, wqkv_ref, wssm_ref, wgate_ref, ws_ref, bs_ref, *rest, with_bias):
    if with_bias:
        (sel_ref, q_ref, kf_ref, vf_ref, kb_ref, vb_ref, z_ref, xbc_ref, ga_ref, gb_ref, sm_ref, stats_ref,
         carry_ref) = rest
    else:
        q_ref, kf_ref, vf_ref, kb_ref, vb_ref, z_ref, xbc_ref, ga_ref, gb_ref, sm_ref = rest
    h = _rms(x_ref[...], g_ref[...]) * (1.0 + sc_ref[...]) + sh_ref[...]
    hb = h.astype(BF16)

    def proj(w_ref, lo, width):
        return jnp.dot(hb, w_ref[:, lo:lo + width], preferred_element_type=F32)

    u = jnp.dot(hb, ws_ref[...], preferred_element_type=F32) + bs_ref[...]
    t = jnp.log1p(jnp.exp(-jnp.abs(u)))
    lane = lax.broadcasted_iota(jnp.int32, u.shape, 1)
    sm = jnp.where(lane < DT_LANE0, jnp.minimum(u, 0.0) - t,
                   jnp.where(lane < DT_LANE0 + SSM_HEADS, jnp.maximum(u, 0.0) + t, 0.0))
    sm_ref[...] = sm

    qb = (proj(wqkv_ref, 0, D_MODEL) * (FOX_SCALE * LOG2E)).astype(BF16)
    q_ref[...] = qb
    k = proj(wqkv_ref, D_MODEL, D_MODEL)
    kf_ref[...] = k.reshape(k.shape[0], FOX_HEADS, FOX_HEAD_DIM)
    if with_bias:
        @pl.when(pl.program_id(0) == 0)
        def _():
            carry_ref[...] = jnp.zeros_like(carry_ref)

        r = lax.broadcasted_iota(jnp.int32, (CUM_ROWS, CUM_ROWS), 0)
        c = lax.broadcasted_iota(jnp.int32, (CUM_ROWS, CUM_ROWS), 1)
        tri = (r >= c).astype(F32)
        run = carry_ref[0:1, :]
        parts = []
        for b in range(sm.shape[0] // CUM_ROWS):
            cs = jnp.dot(tri, sm[b * CUM_ROWS:(b + 1) * CUM_ROWS, :], precision=HIGHEST,
                         preferred_element_type=F32) + run
            run = cs[CUM_ROWS - 1:CUM_ROWS, :]
            parts.append(cs)
        carry_ref[0:1, :] = run
        nb = jnp.concatenate(parts, axis=0) * (-LOG2E)
        hi = nb.astype(BF16)
        r1 = nb - hi.astype(F32)
        mid = r1.astype(BF16)
        lo = (r1 - mid.astype(F32)).astype(BF16)
        aug = jnp.dot(jnp.concatenate([hi, mid, lo], axis=1), sel_ref[...], preferred_element_type=F32)
        kb = k.astype(BF16)
        for hd in range(FOX_HEADS):
            sl = slice(hd * FOX_HEAD_DIM, (hd + 1) * FOX_HEAD_DIM)
            kb_ref[:, 2 * hd * FOX_HEAD_DIM:(2 * hd + 1) * FOX_HEAD_DIM] = kb[:, sl]
            kb_ref[:, (2 * hd + 1) * FOX_HEAD_DIM:(2 * hd + 2) * FOX_HEAD_DIM] = aug[:, sl].astype(BF16)

        def max_sq_norms(xb):
            x2 = xb.astype(F32)
            x2 = x2 * x2
            out = jnp.zeros((1, LANES), F32)
            for hd in range(FOX_HEADS):
                n2 = jnp.sum(x2[:, hd * FOX_HEAD_DIM:(hd + 1) * FOX_HEAD_DIM], axis=-1, keepdims=True)
                out = jnp.where(lane[0:1, :] == hd, jnp.max(n2, axis=0, keepdims=True), out)
            return out

        stats_ref[...] = jnp.zeros(stats_ref.shape, F32)
        stats_ref[0:1, :] = max_sq_norms(qb)
        stats_ref[1:2, :] = max_sq_norms(kb)
        stats_ref[2:3, :] = nb[0:1, :]
        stats_ref[3:4, :] = nb[nb.shape[0] - 1:, :]
    else:
        kb_ref[...] = k.astype(BF16)
    v = proj(wqkv_ref, 2 * D_MODEL, D_MODEL)
    vf_ref[...] = v.reshape(v.shape[0], FOX_HEADS, FOX_HEAD_DIM)
    vb_ref[...] = v.astype(BF16)
    z_ref[...] = proj(wssm_ref, 0, SSM_INNER).astype(BF16)
    xbc_ref[...] = proj(wssm_ref, SSM_INNER, CONV_DIM)
    ga_ref[...] = proj(wgate_ref, 0, D_MODEL).astype(BF16)
    gb_ref[...] = proj(wgate_ref, D_MODEL, D_MODEL).astype(BF16)


def _mod_spec(arr, tm):
    if arr.shape[0] == 1:
        return pl.BlockSpec((1, D_MODEL), lambda i: (0, 0))
    return pl.BlockSpec((tm, D_MODEL), lambda i: (i, 0))


def _bias_selector():
    r = lax.broadcasted_iota(jnp.int32, (BIAS_PARTS * LANES, FOX_HEADS * FOX_HEAD_DIM), 0)
    c = lax.broadcasted_iota(jnp.int32, (BIAS_PARTS * LANES, FOX_HEADS * FOX_HEAD_DIM), 1)
    hit = (r % LANES < FOX_HEADS) & (c == (r % LANES) * FOX_HEAD_DIM + r // LANES)
    return hit.astype(BF16)


def _in_proj(x2d, g, sc, sh, w_qkv, w_ssm, w_gate, w_small, b_small, tm, with_bias):
    rows = x2d.shape[0]
    assert not with_bias or tm % CUM_ROWS == 0
    row = lambda w: pl.BlockSpec((tm, w), lambda i: (i, 0))
    shp = lambda w, dt: jax.ShapeDtypeStruct((rows, w), dt)
    kb_width = 2 * D_MODEL if with_bias else D_MODEL
    in_specs = [row(D_MODEL), _resident((1, D_MODEL)), _mod_spec(sc, tm), _mod_spec(sh, tm),
                _resident(w_qkv.shape), _resident(w_ssm.shape), _resident(w_gate.shape),
                _resident((D_MODEL, LANES)), _resident((1, LANES))]
    args = [x2d, g, sc, sh, w_qkv, w_ssm, w_gate, w_small, b_small]
    heads = pl.BlockSpec((tm, FOX_HEADS, FOX_HEAD_DIM), lambda i: (i, 0, 0))
    heads_shape = jax.ShapeDtypeStruct((rows, FOX_HEADS, FOX_HEAD_DIM), F32)
    out_specs = [row(D_MODEL), heads, heads, row(kb_width), row(D_MODEL),
                 row(SSM_INNER), row(CONV_DIM), row(D_MODEL), row(D_MODEL), row(LANES)]
    out_shape = [shp(D_MODEL, BF16), heads_shape, heads_shape, shp(kb_width, BF16),
                 shp(D_MODEL, BF16), shp(SSM_INNER, BF16), shp(CONV_DIM, F32), shp(D_MODEL, BF16),
                 shp(D_MODEL, BF16), shp(LANES, F32)]
    if with_bias:
        in_specs.append(_resident((BIAS_PARTS * LANES, FOX_HEADS * FOX_HEAD_DIM)))
        args.append(_bias_selector())
        out_specs.append(pl.BlockSpec((SUBLANES, LANES), lambda i: (i, 0)))
        out_shape.append(jax.ShapeDtypeStruct((rows // tm * SUBLANES, LANES), F32))
    return pl.pallas_call(
        functools.partial(_inproj_kernel, with_bias=with_bias),
        grid=(rows // tm,),
        in_specs=in_specs,
        out_specs=out_specs,
        out_shape=out_shape,
        scratch_shapes=[pltpu.VMEM((SUBLANES, LANES), F32)] if with_bias else [],
        compiler_params=_params("arbitrary" if with_bias else "parallel"),
        name="in_proj",
    )(*args)


def _flash_kernel(first_ref, q_ref, k_ref, v_ref, o_ref, s0_ref, s1_ref, p0_ref, p1_ref, m_ref, l_ref, alpha0_ref,
                  alpha1_ref, acc_ref, *, tq, tk, nsub):
    hd = pl.program_id(0)
    first_sub = pl.program_id(1) * nsub
    nlt = tk // LANES
    lane = lax.broadcasted_iota(jnp.int32, (tq, FOX_HEAD_DIM), 1)
    ones = jnp.where(lane < BIAS_PARTS, 1.0, 0.0).astype(BF16)

    def q_rows(sub):
        return pl.ds(pl.multiple_of(sub * tq, tq), tq)

    def q_aug(sub):
        return jnp.concatenate([q_ref[q_rows(sub), :], ones], axis=1)

    def scores(q2, j, s_ref):
        off = pl.multiple_of(j * tk, tk)
        s_ref[...] = lax.dot_general(q2, k_ref[pl.ds(off, tk), :], _NT, preferred_element_type=F32)

    def softmax(s_ref, p_ref, alpha_ref, mask_shift):
        for rt in range(tq // ROW_TILE):
            rows = slice(rt * ROW_TILE, (rt + 1) * ROW_TILE)
            cols = [s_ref[rows, c * LANES:(c + 1) * LANES] for c in range(nlt)]
            if mask_shift is not None:
                ri = lax.broadcasted_iota(jnp.int32, (ROW_TILE, LANES), 0) + rt * ROW_TILE
                ci = lax.broadcasted_iota(jnp.int32, (ROW_TILE, LANES), 1) + mask_shift
                cols = [jnp.where(ci + c * LANES <= ri, cols[c], -jnp.inf) for c in range(nlt)]
            mx = functools.reduce(jnp.maximum, cols)
            m_old = m_ref[rows, :]
            m_new = jnp.maximum(m_old, jnp.max(mx, axis=-1, keepdims=True))
            alpha = jnp.exp2(m_old - m_new)
            ps = [jnp.exp2(cc - m_new) for cc in cols]
            l_ref[rows, :] = alpha * l_ref[rows, :] + jnp.sum(functools.reduce(jnp.add, ps), axis=-1, keepdims=True)
            m_ref[rows, :] = m_new
            alpha_ref[rows, :] = alpha
            for c in range(nlt):
                p_ref[rows, c * LANES:(c + 1) * LANES] = ps[c].astype(BF16)

    def weighted_values(j, p_ref, alpha_ref):
        off = pl.multiple_of(j * tk, tk)
        acc_ref[...] = alpha_ref[...] * acc_ref[...] + jnp.dot(p_ref[...], v_ref[pl.ds(off, tk), :],
                                                               preferred_element_type=F32)

    bufs = ((s0_ref, p0_ref, alpha0_ref), (s1_ref, p1_ref, alpha1_ref))

    def run(q2, first, n, diagonal_last, next_scores):
        for k in range(n):
            s_ref, p_ref, alpha_ref = bufs[k % 2]
            s_nxt, p_prv, alpha_prv = bufs[(k + 1) % 2]
            if k > 0:
                weighted_values(first + k - 1, p_prv, alpha_prv)
            if k + 1 < n or next_scores:
                scores(q2, first + k + 1, s_nxt)
            softmax(s_ref, p_ref, alpha_ref, 0 if (diagonal_last and k == n - 1) else None)
        weighted_values(first + n - 1, *bufs[(n - 1) % 2][1:])

    scores(q_aug(0), first_ref[hd, first_sub], s0_ref)

    @pl.loop(0, nsub)
    def _(sub):
        i = first_sub + sub
        q2 = q_aug(sub)
        m_ref[...] = jnp.full(m_ref.shape, -jnp.inf, F32)
        l_ref[...] = jnp.zeros(l_ref.shape, F32)
        acc_ref[...] = jnp.zeros(acc_ref.shape, F32)
        j0 = first_ref[hd, i]
        n_full = i - j0

        @pl.loop(0, n_full // UNROLL)
        def _(t):
            run(q2, j0 + UNROLL * t, UNROLL, False, True)

        nxt = jnp.minimum(sub + 1, nsub - 1)
        q2_next = q_aug(nxt)
        j0_next = first_ref[hd, first_sub + nxt]
        for rem in range(UNROLL):
            @pl.when(n_full % UNROLL == rem)
            def _():
                run(q2, i - rem, rem + 1, True, False)
                scores(q2_next, j0_next, s0_ref)
                o_ref[q_rows(sub), :] = (acc_ref[...] / l_ref[...]).astype(o_ref.dtype)


def _first_blocks(stats, nq):
    st = stats.reshape(nq, SUBLANES, LANES)
    qn = jnp.sqrt(st[:, 0, :FOX_HEADS])
    kn = jnp.sqrt(st[:, 1, :FOX_HEADS])
    b_first, b_last = st[:, 2, :FOX_HEADS], st[:, 3, :FOX_HEADS]
    bound = qn[:, None, :] * (kn[None, :, :] + kn[:, None, :]) + b_last[None, :, :] - b_first[:, None, :]
    blocks = jnp.arange(nq, dtype=jnp.int32)
    first_needed = jnp.min(jnp.where(bound <= -SKIP_MARGIN, nq, blocks[None, :, None]), axis=1)
    return jnp.minimum(first_needed, blocks[:, None]).T


def _fox_prompt(q, kp, v, first, tq):
    L = q.shape[0]
    tk = tq
    nsub = min(FLASH_SUBBLOCKS, L // tq)
    return pl.pallas_call(
        functools.partial(_flash_kernel, tq=tq, tk=tk, nsub=nsub),
        grid=(FOX_HEADS, L // (tq * nsub)),
        in_specs=[pl.BlockSpec(memory_space=pltpu.SMEM),
                  pl.BlockSpec((nsub * tq, FOX_HEAD_DIM), lambda h, i: (i, h)),
                  pl.BlockSpec((L, 2 * FOX_HEAD_DIM), lambda h, i: (0, h)),
                  pl.BlockSpec((L, FOX_HEAD_DIM), lambda h, i: (0, h))],
        out_specs=pl.BlockSpec((nsub * tq, FOX_HEAD_DIM), lambda h, i: (i, h)),
        out_shape=jax.ShapeDtypeStruct((L, FOX_HEADS * FOX_HEAD_DIM), BF16),
        scratch_shapes=[pltpu.VMEM((tq, tk), F32)] * 2 + [pltpu.VMEM((tq, tk), BF16)] * 2
        + [pltpu.VMEM((tq, FOX_HEAD_DIM), F32)] * 5,
        compiler_params=_params("parallel", "arbitrary"),
        name="fox_flash",
    )(first, q, kp, v)


def _cached_attn_kernel(q_ref, kn_ref, vn_ref, kc_ref, vc_ref, lpt_ref, lnt_ref, o_ref):
    n = q_ref.shape[0]
    P = kc_ref.shape[0]
    r = lax.broadcasted_iota(jnp.int32, (P, P), 0)
    c = lax.broadcasted_iota(jnp.int32, (P, P), 1)
    G = jnp.dot(lpt_ref[...], (r > c).astype(F32), precision=HIGHEST, preferred_element_type=F32) * LOG2E
    rn = lax.broadcasted_iota(jnp.int32, (n, n), 0)
    cn = lax.broadcasted_iota(jnp.int32, (n, n), 1)
    Hn = jnp.dot(lnt_ref[...], (rn <= cn).astype(F32), precision=HIGHEST, preferred_element_type=F32) * LOG2E
    causal = cn <= rn
    kt = jnp.swapaxes(kc_ref[...], 0, 1)
    vt = jnp.swapaxes(vc_ref[...], 0, 1)
    head = lambda h: slice(h * FOX_HEAD_DIM, (h + 1) * FOX_HEAD_DIM)
    scores = []
    for h in range(FOX_HEADS):
        qh = q_ref[:, head(h)]
        sp = lax.dot_general(qh, kt[h].astype(BF16), _NT, preferred_element_type=F32) + G[h:h + 1, :]
        sn = lax.dot_general(qh, kn_ref[:, head(h)], _NT, preferred_element_type=F32) - Hn[h:h + 1, :]
        scores.append((sp, jnp.where(causal, sn, -jnp.inf)))
    probs = []
    for sp, sn in scores:
        m = jnp.maximum(jnp.max(sp, axis=-1, keepdims=True), jnp.max(sn, axis=-1, keepdims=True))
        pp = jnp.exp2(sp - m)
        pn = jnp.exp2(sn - m)
        l = jnp.sum(pp, axis=-1, keepdims=True) + jnp.sum(pn, axis=-1, keepdims=True)
        probs.append((pp.astype(BF16), pn.astype(BF16), l))
    for h, (pp, pn, l) in enumerate(probs):
        o = (jnp.dot(pp, vt[h].astype(BF16), preferred_element_type=F32)
             + jnp.dot(pn, vn_ref[:, head(h)], preferred_element_type=F32))
        o_ref[:, head(h)] = (o / l).astype(o_ref.dtype)


def _fox_cached(q, kn, vn, kc, vc, lpt, lnt, n):
    B, P = kc.shape[:2]
    W = FOX_HEADS * FOX_HEAD_DIM
    row = pl.BlockSpec((n, W), lambda b: (b, 0))
    cache = pl.BlockSpec((None, P, FOX_HEADS, FOX_HEAD_DIM), lambda b: (b, 0, 0, 0))
    return pl.pallas_call(
        _cached_attn_kernel,
        grid=(B,),
        in_specs=[row, row, row, cache, cache,
                  pl.BlockSpec((None, FOX_HEADS, P), lambda b: (b, 0, 0)),
                  pl.BlockSpec((None, FOX_HEADS, n), lambda b: (b, 0, 0))],
        out_specs=row,
        out_shape=jax.ShapeDtypeStruct((B * n, W), BF16),
        compiler_params=_params("parallel"),
        name="fox_cached",
    )(q, kn, vn, kc, vc, lpt, lnt)


def _ssd_kernel(xbc_ref, z_ref, sm_ref, convp_ref, ssmp_ref, cw_ref, cb_ref, alog_ref, dexp_ref, ng_ref,
                expand_ref, y_ref, convn_ref, ssmn_ref, tail_ref, st_ref, *, Q, nchunks):
    ci = pl.program_id(1)
    PAIR = 2 * SSM_HEAD_DIM
    GW = SSM_STATE

    @pl.when(ci == 0)
    def _():
        st_ref[...] = ssmp_ref[...]
        tail_ref[...] = jnp.zeros((SUBLANES, CONV_DIM), F32)
        tail_ref[SUBLANES - (CONV_WIDTH - 1):SUBLANES, :] = convp_ref[...]

    x3 = xbc_ref[...].reshape(Q // SUBLANES, SUBLANES, CONV_DIM)
    tail = tail_ref[...].reshape(1, SUBLANES, CONV_DIM)
    sub = lax.broadcasted_iota(jnp.int32, x3.shape, 1)
    conv = cb_ref[...]
    for d in range(CONV_WIDTH - 1, 0, -1):
        rot = pltpu.roll(x3, d, 1)
        prev = jnp.concatenate([pltpu.roll(tail, d, 1), rot[:-1]], axis=0)
        conv = conv + jnp.where(sub < d, prev, rot) * cw_ref[CONV_WIDTH - 1 - d:CONV_WIDTH - d, :]
    conv = (conv + x3 * cw_ref[CONV_WIDTH - 1:CONV_WIDTH, :]).reshape(Q, CONV_DIM)
    tail_ref[...] = xbc_ref[Q - SUBLANES:Q, :]
    convn_ref[...] = xbc_ref[Q - (CONV_WIDTH - 1):Q, :]

    xc = _silu(conv)
    xs = xc[:, :SSM_INNER]
    Bm = xc[:, SSM_INNER:SSM_INNER + SSM_GROUPS * GW]
    Cm = xc[:, SSM_INNER + SSM_GROUPS * GW:]

    lane1 = lax.broadcasted_iota(jnp.int32, (Q, LANES), 1)
    head_lanes = (lane1 >= DT_LANE0) & (lane1 < DT_LANE0 + SSM_HEADS)
    dt = jnp.where(head_lanes, sm_ref[...], 0.0)
    a = dt * (-jnp.exp(alog_ref[...]))
    r = lax.broadcasted_iota(jnp.int32, (Q, Q), 0)
    c = lax.broadcasted_iota(jnp.int32, (Q, Q), 1)
    causal = r >= c
    a_cs = jnp.dot(causal.astype(F32), a, precision=HIGHEST, preferred_element_type=F32)
    er = lax.broadcasted_iota(jnp.int32, (SSM_HEADS, LANES), 0)
    ec = lax.broadcasted_iota(jnp.int32, (SSM_HEADS, LANES), 1)
    pick = (ec == er + DT_LANE0).astype(F32)
    acs_t = lax.dot_general(pick, a_cs, _NT, precision=HIGHEST, preferred_element_type=F32)
    a_last = a_cs[Q - 1:Q, :]
    e_last = jnp.exp(a_last)

    def per_channel(x):
        x = jnp.where(head_lanes, x, 0.0)
        hi = x.astype(BF16).astype(F32)
        r1 = x - hi
        mid = r1.astype(BF16).astype(F32)
        packed = hi + pltpu.roll(mid, PIECE_LANES, 1) + pltpu.roll(r1 - mid, 2 * PIECE_LANES, 1)
        return jnp.dot(packed.astype(BF16), expand_ref[...], preferred_element_type=F32)

    xd = xs * per_channel(dt)
    xdb = xd.astype(BF16)
    xw = xd * per_channel(jnp.exp(a_last - a_cs))
    e_cs = per_channel(jnp.exp(a_cs))
    Bb = Bm.astype(BF16)
    Cb = Cm.astype(BF16)

    row_bands = [(0, Q // 2, Q // 2), (Q // 2, Q, Q)] if Q % (2 * LANES) == 0 else [(0, Q, Q)]
    lane_lo = lax.broadcasted_iota(jnp.int32, (Q, PAIR), 1) < SSM_HEAD_DIM
    row_lo = lax.broadcasted_iota(jnp.int32, (PAIR, GW), 0) < SSM_HEAD_DIM

    y_parts = []
    for g in range(SSM_GROUPS):
        Bg = Bb[:, g * GW:(g + 1) * GW]
        Cg = Cb[:, g * GW:(g + 1) * GW]
        cbm = lax.dot_general(Cg, Bg, _NT, preferred_element_type=F32)
        for pr in range(SSM_HEADS // SSM_GROUPS // 2):
            pair = g * (SSM_HEADS // SSM_GROUPS // 2) + pr
            sl = slice(pair * PAIR, (pair + 1) * PAIR)
            s_prev = st_ref[sl, :]
            ys = []
            for hh in (2 * pair, 2 * pair + 1):
                ln = DT_LANE0 + hh
                parts = []
                for r0, r1, nc in row_bands:
                    lm = jnp.exp(jnp.where(causal[r0:r1, :nc], a_cs[r0:r1, ln:ln + 1] - acs_t[hh:hh + 1, :nc],
                                           -jnp.inf))
                    parts.append(jnp.dot((cbm[r0:r1, :nc] * lm).astype(BF16), xdb[:nc, sl],
                                         preferred_element_type=F32))
                ys.append(parts[0] if len(parts) == 1 else jnp.concatenate(parts, axis=0))
            y_off = lax.dot_general(Cg, s_prev.astype(BF16), _NT, preferred_element_type=F32) * e_cs[:, sl]
            ln0 = DT_LANE0 + 2 * pair
            dec = jnp.where(row_lo, e_last[:, ln0:ln0 + 1], e_last[:, ln0 + 1:ln0 + 2])
            st_ref[sl, :] = s_prev * dec + lax.dot_general(xw[:, sl], Bm[:, g * GW:(g + 1) * GW], _TN,
                                                           preferred_element_type=F32)
            y_parts.append(jnp.where(lane_lo, ys[0], ys[1]) + y_off + xs[:, sl] * dexp_ref[:, sl])

    y = jnp.concatenate(y_parts, axis=1) * _silu(z_ref[...].astype(F32))
    gw = SSM_INNER // SSM_GROUPS
    normed = []
    for g in range(SSM_GROUPS):
        yg = y[:, g * gw:(g + 1) * gw]
        normed.append(yg * lax.rsqrt(jnp.mean(yg * yg, axis=-1, keepdims=True) + EPS))
    y_ref[...] = (jnp.concatenate(normed, axis=1) * ng_ref[...]).astype(y_ref.dtype)

    @pl.when(ci == nchunks - 1)
    def _():
        ssmn_ref[...] = st_ref[...]


def _head_expander():
    k = lax.broadcasted_iota(jnp.int32, (LANES, SSM_INNER), 0)
    j = lax.broadcasted_iota(jnp.int32, (LANES, SSM_INNER), 1)
    hit = (k < BIAS_PARTS * PIECE_LANES) & (k % PIECE_LANES == DT_LANE0 + j // SSM_HEAD_DIM)
    return hit.astype(BF16)


def _ssd(xbc, z, small, conv_past, ssm_past, conv_w, conv_b, alog_pad, d_exp, norm_g, batch, Q):
    rows = xbc.shape[0]
    nchunks = rows // batch // Q
    assert Q % SUBLANES == 0 and Q >= SUBLANES and nchunks * Q * batch == rows
    row = lambda w: pl.BlockSpec((Q, w), lambda b, c: (b * nchunks + c, 0))
    state_rows = SSM_HEADS * SSM_HEAD_DIM
    per_seq = lambda a, b_: pl.BlockSpec((None, a, b_), lambda b, c: (b, 0, 0))
    return pl.pallas_call(
        functools.partial(_ssd_kernel, Q=Q, nchunks=nchunks),
        grid=(batch, nchunks),
        in_specs=[row(CONV_DIM), row(SSM_INNER), row(LANES),
                  per_seq(CONV_WIDTH - 1, CONV_DIM), per_seq(state_rows, SSM_STATE),
                  _resident((CONV_WIDTH, CONV_DIM)), _resident((1, CONV_DIM)), _resident((1, LANES)),
                  _resident((1, SSM_INNER)), _resident((1, SSM_INNER)), _resident((LANES, SSM_INNER))],
        out_specs=[row(SSM_INNER), per_seq(CONV_WIDTH - 1, CONV_DIM), per_seq(state_rows, SSM_STATE)],
        out_shape=[jax.ShapeDtypeStruct((rows, SSM_INNER), BF16),
                   jax.ShapeDtypeStruct((batch, CONV_WIDTH - 1, CONV_DIM), F32),
                   jax.ShapeDtypeStruct((batch, state_rows, SSM_STATE), F32)],
        scratch_shapes=[pltpu.VMEM((SUBLANES, CONV_DIM), F32), pltpu.VMEM((state_rows, SSM_STATE), F32)],
        compiler_params=_params("parallel", "arbitrary"),
        name="ssd_scan",
    )(xbc, z, small, conv_past, ssm_past, conv_w, conv_b, alog_pad, d_exp, norm_g, _head_expander())


def _outffn_kernel(ya_ref, ys_ref, ga_ref, gb_ref, x_ref, gt1_ref, sc2_ref, sh2_ref, gt2_ref,
                   gpm_ref, gpf_ref, gqf_ref, wo_ref, wu_ref, wd_ref, o_ref):
    tm = x_ref.shape[0]
    halves = (slice(0, tm // 2), slice(tm // 2, tm))
    rows_of = lambda ref, sl: ref[...] if ref.shape[0] == 1 else ref[sl, :]
    ms = []
    for sl in halves:
        merged = (_sigmoid(ga_ref[sl, :].astype(F32)) * ya_ref[sl, :].astype(F32)
                  + _sigmoid(gb_ref[sl, :].astype(F32)) * ys_ref[sl, :].astype(F32))
        ms.append(jnp.dot(merged.astype(BF16), wo_ref[...], preferred_element_type=F32))
    for sl, m in zip(halves, ms):
        x1 = x_ref[sl, :] + rows_of(gt1_ref, sl) * _rms(m, gpm_ref[...])
        hb = (_rms(x1, gpf_ref[...]) * (1.0 + rows_of(sc2_ref, sl)) + rows_of(sh2_ref, sl)).astype(BF16)
        f = jnp.zeros(x1.shape, F32)
        for cc in range(D_FF // D_MODEL):
            cs = slice(cc * D_MODEL, (cc + 1) * D_MODEL)
            up = jnp.dot(hb, wu_ref[:, cs], preferred_element_type=F32)
            act = jnp.square(jnp.maximum(up, 0.0)).astype(BF16)
            f = f + jnp.dot(act, wd_ref[cs, :], preferred_element_type=F32)
        o_ref[sl, :] = x1 + rows_of(gt2_ref, sl) * _rms(f, gqf_ref[...])


def _out_ffn(ya, ys, ga, gb, x2d, gt1, sc2, sh2, gt2, g_post_mix, g_pre_ffn, g_post_ffn, w_out, w_up, w_down, tm):
    rows = x2d.shape[0]
    row = pl.BlockSpec((tm, D_MODEL), lambda i: (i, 0))
    vec = _resident((1, D_MODEL))
    return pl.pallas_call(
        _outffn_kernel,
        grid=(rows // tm,),
        in_specs=[row, row, row, row, row,
                  _mod_spec(gt1, tm), _mod_spec(sc2, tm), _mod_spec(sh2, tm), _mod_spec(gt2, tm),
                  vec, vec, vec,
                  _resident((D_MODEL, D_MODEL)), _resident((D_MODEL, D_FF)), _resident((D_FF, D_MODEL))],
        out_specs=row,
        out_shape=jax.ShapeDtypeStruct((rows, D_MODEL), F32),
        compiler_params=_params("parallel"),
        name="out_ffn",
    )(ya, ys, ga, gb, x2d, gt1, sc2, sh2, gt2, g_post_mix, g_pre_ffn, g_post_ffn, w_out, w_up, w_down)


def _layer(x, mod, past, wts, *, tm, tq, Q):
    b, L, _ = x.shape
    x2d = x.reshape(b * L, D_MODEL)
    sh1, sc1, gt1, sh2, sc2, gt2 = mod
    q, kf, vf, kb, vb, z, xbc, ga, gb, small, *stats = _in_proj(
        x2d, wts["g_pre_mix"], sc1, sh1, wts["w_qkv"], wts["w_ssm"], wts["w_gate"], wts["w_small"], wts["b_small"],
        tm, past is None)

    if past is None:
        assert b == 1 and tm == tq, "block statistics are per projection row block"
        y_att = _fox_prompt(q, kb, vb, _first_blocks(stats[0], L // tq), tq)
        conv_past = jnp.zeros((b, CONV_WIDTH - 1, CONV_DIM), F32)
        ssm_past = jnp.zeros((b, SSM_INNER, SSM_STATE), F32)
    else:
        k_past, v_past, logf_past, conv_past, ssm_past = past
        P = k_past.shape[1]
        lpt = jnp.swapaxes(logf_past, 1, 2)
        lnt = jnp.swapaxes(small[:, :FOX_HEADS].reshape(b, L, FOX_HEADS), 1, 2)
        y_att = _fox_cached(q, kb, vb, k_past, v_past, lpt, lnt, L)
        ssm_past = ssm_past.reshape(b, SSM_INNER, SSM_STATE)

    y_ssm, conv_new, ssm_new = _ssd(xbc, z, small, conv_past, ssm_past, wts["conv_w"], wts["conv_b"],
                                    wts["alog_pad"], wts["d_exp"], wts["ssm_norm_g"], b, Q)
    y = _out_ffn(y_att, y_ssm, ga, gb, x2d, gt1, sc2, sh2, gt2, wts["g_post_mix"], wts["g_pre_ffn"],
                 wts["g_post_ffn"], wts["w_out"], wts["w_up"], wts["w_down"], tm)
    return (y.reshape(b, L, D_MODEL),
            kf.reshape(b, L, FOX_HEADS, FOX_HEAD_DIM), vf.reshape(b, L, FOX_HEADS, FOX_HEAD_DIM),
            small[:, :FOX_HEADS].reshape(b, L, FOX_HEADS), conv_new,
            ssm_new.reshape(b, SSM_HEADS, SSM_HEAD_DIM, SSM_STATE))


def _split_kernel(w_ref, qkv_ref, ssm_ref, gate_ref, *, offs):
    qkv_ref[...] = w_ref[:, offs[0]:offs[3]].astype(BF16)
    ssm_ref[...] = w_ref[:, offs[4]:offs[6]].astype(BF16)
    gate_ref[...] = w_ref[:, offs[7]:offs[9]].astype(BF16)


def _split_w_in(w_in, offs):
    rows = w_in.shape[0]
    widths = (offs[3] - offs[0], offs[6] - offs[4], offs[9] - offs[7])
    tr = 128
    return pl.pallas_call(
        functools.partial(_split_kernel, offs=tuple(offs)),
        grid=(rows // tr,),
        in_specs=[pl.BlockSpec((tr, w_in.shape[1]), lambda i: (i, 0))],
        out_specs=[pl.BlockSpec((tr, w), lambda i: (i, 0)) for w in widths],
        out_shape=[jax.ShapeDtypeStruct((rows, w), BF16) for w in widths],
        compiler_params=_params("parallel"),
        name="split_w_in",
    )(w_in)


def _prep_weights(w_ada, b_ada, g_pre_mix, g_post_mix, g_pre_ffn, g_post_ffn, w_in, b_f, conv_w, conv_b,
                  dt_bias, a_log, d_skip, ssm_norm_g, w_out, w_up, w_down):
    sizes = (D_MODEL, D_MODEL, D_MODEL, FOX_HEADS, SSM_INNER, CONV_DIM, SSM_HEADS, D_MODEL, D_MODEL)
    offs = [0]
    for s in sizes:
        offs.append(offs[-1] + s)
    piece = lambda i: w_in[:, offs[i]:offs[i + 1]]
    pad = LANES - FOX_HEADS - SSM_HEADS
    row = lambda v: v.reshape(1, -1).astype(F32)
    return {
        **dict(zip(("w_qkv", "w_ssm", "w_gate"), _split_w_in(w_in, offs))),
        "w_small": jnp.pad(jnp.concatenate([piece(3), piece(6)], axis=1), ((0, 0), (0, pad))).astype(BF16),
        "b_small": jnp.pad(jnp.concatenate([b_f, dt_bias]), (0, pad)).reshape(1, LANES).astype(F32),
        "alog_pad": jnp.pad(a_log, (DT_LANE0, LANES - DT_LANE0 - SSM_HEADS)).reshape(1, LANES).astype(F32),
        "d_exp": jnp.repeat(d_skip, SSM_HEAD_DIM).reshape(1, SSM_INNER).astype(F32),
        "g_pre_mix": row(g_pre_mix), "g_post_mix": row(g_post_mix),
        "g_pre_ffn": row(g_pre_ffn), "g_post_ffn": row(g_post_ffn),
        "conv_w": conv_w.astype(F32), "conv_b": row(conv_b), "ssm_norm_g": row(ssm_norm_g),
        "w_out": w_out.astype(BF16), "w_up": w_up.astype(BF16), "w_down": w_down.astype(BF16),
    }


def _forward(x_prompt, x_sample, c_prompt, c_sample, cache_fox_k, cache_fox_v, cache_fox_logf,
             state_ssm_conv, state_ssm, w_ada, b_ada, *layer_w, tm_prompt, tq, q_prompt):
    depth = w_ada.shape[0]
    bp, Lp, _ = x_prompt.shape
    bs, Ls, _ = x_sample.shape
    yp, ys = x_prompt, x_sample
    outs_p, outs_s = [], []
    for i in range(depth):
        wts = _prep_weights(w_ada[i], b_ada[i], *[w[i] for w in layer_w])
        mod = _ada(jnp.concatenate([c_prompt, c_sample], axis=0), w_ada[i], b_ada[i].reshape(1, -1))
        mod_p = [m for m in jnp.split(mod[:bp], N_MOD, axis=-1)]
        mod_s = [jnp.repeat(m, Ls, axis=0) for m in jnp.split(mod[bp:], N_MOD, axis=-1)]
        rp = _layer(yp, mod_p, None, wts, tm=tm_prompt, tq=tq, Q=q_prompt)
        rs = _layer(ys, mod_s, (cache_fox_k[i], cache_fox_v[i], cache_fox_logf[i], state_ssm_conv[i], state_ssm[i]),
                    wts, tm=bs * Ls, tq=None, Q=Ls)
        yp, ys = rp[0], rs[0]
        outs_p.append(rp[1:])
        outs_s.append(rs[1:])
    stack = lambda outs, j: jnp.stack([o[j] for o in outs])
    return (yp, ys) + tuple(stack(outs_p, j) for j in range(5)) + tuple(stack(outs_s, j) for j in range(5))


def kernel(x_prompt, x_sample, c_prompt, c_sample, cache_fox_k, cache_fox_v, cache_fox_logf, state_ssm_conv,
           state_ssm, w_ada, b_ada, g_pre_mix, g_post_mix, g_pre_ffn, g_post_ffn, w_in, b_f, conv_w, conv_b,
           dt_bias, a_log, d_skip, ssm_norm_g, w_out, w_up, w_down):
    assert x_prompt.shape[0] == 1, "the prompt path carries one sequence"
    L = x_prompt.shape[1]
    return _forward(x_prompt, x_sample, c_prompt, c_sample, cache_fox_k, cache_fox_v, cache_fox_logf,
                    state_ssm_conv, state_ssm, w_ada, b_ada, g_pre_mix, g_post_mix, g_pre_ffn, g_post_ffn,
                    w_in, b_f, conv_w, conv_b, dt_bias, a_log, d_skip, ssm_norm_g, w_out, w_up, w_down,
                    tm_prompt=min(512, L), tq=min(512, L), q_prompt=min(256, L))
```
